```python
import math
import jax
import jax.numpy as jnp
from jax import lax
import numpy as np

D_MODEL = 1024
BATCH = 8
SEQ = 2048
DEPTH = 2
DEC_BATCH = 128
DEC_SEQ = 8
PAST_LEN = 2048
PAGE_SIZE = 128

N_EVEN = (DEPTH + 1) // 2
N_ODD = DEPTH // 2
H_A = 8
HD_A = 64
ROT_DIM = HD_A // 4
ROPE_THETA = 500000.0
MOBA_BLOCK = 256
MOBA_TOPK = 3
Q_CHUNK = 64
H_R = 8
DK_R = 64
DV_R = 128
RET_CHUNK = 128
S5_GROUP = 16
S5_GROUPS = D_MODEL // S5_GROUP
S5_STATE = 64
MOE_GROUPS = 4
MOE_PER_GROUP = 8
N_EXPERTS = MOE_GROUPS * MOE_PER_GROUP
MOE_TOPK = 2
D_EXPERT = D_MODEL // 2
MOE_BLOCK = 128
A_W = H_A * HD_A
R_QK = H_R * DK_R
R_V = H_R * DV_R
IN_EVEN = 3 * A_W + 2 * R_QK + 2 * R_V
MIX_EVEN = A_W + R_V
SPLIT_EVEN = (A_W, 2 * A_W, 3 * A_W, 3 * A_W + R_QK, 3 * A_W + 2 * R_QK, 3 * A_W + 2 * R_QK + R_V)
NEG_INF = -1e30
EPS = 1e-6

kernel_name = 'hybrid_moba_retention_s5_hmoe_step'


def _rms(x, g):
    xf = x.astype(jnp.float32)
    y = xf * lax.rsqrt(jnp.mean(xf * xf, axis=-1, keepdims=True) + EPS)
    return (y * g.astype(jnp.float32)).astype(x.dtype)


def _partial_rope(x, pos):
    half = ROT_DIM // 2
    inv = ROPE_THETA ** (-jnp.arange(half, dtype=jnp.float32) / half)
    ang = pos.astype(jnp.float32)[:, None] * inv[None, :]
    cos = jnp.cos(ang)[None, :, None, :]
    sin = jnp.sin(ang)[None, :, None, :]
    xf = x.astype(jnp.float32)
    x1, x2, rest = xf[..., :half], xf[..., half:ROT_DIM], xf[..., ROT_DIM:]
    return jnp.concatenate([x1 * cos - x2 * sin, x2 * cos + x1 * sin, rest], axis=-1).astype(x.dtype)


def _retnet_rotate(x, pos):
    n = DK_R // 2
    inv = 1.0 / (10000.0 ** jnp.linspace(0.0, 1.0, n, dtype=jnp.float32))
    ang = pos.astype(jnp.float32)[:, None] * inv[None, :]
    cos = jnp.cos(ang)[None, :, None, :]
    sin = jnp.sin(ang)[None, :, None, :]
    xf = x.astype(jnp.float32).reshape(x.shape[:-1] + (n, 2))
    xe, xo = xf[..., 0], xf[..., 1]
    return jnp.stack([xe * cos - xo * sin, xo * cos + xe * sin], axis=-1).reshape(x.shape)


def _moba_attention(q, k, v, start):
    n, tq, h, hd = q.shape
    qc = min(Q_CHUNK, tq)
    n_qc = -(-tq // qc)
    tq_pad = n_qc * qc
    q = jnp.pad(q, ((0, 0), (0, tq_pad - tq), (0, 0), (0, 0)))
    pos = (start + jnp.arange(tq_pad)).reshape(n_qc, qc)
    seq_len = k.shape[1]
    n_blk = -(-seq_len // MOBA_BLOCK)
    pad = n_blk * MOBA_BLOCK - seq_len
    def blocks(a):
        a = jnp.pad(a, ((0, 0), (0, pad), (0, 0), (0, 0)))
        return a.reshape(n, n_blk, MOBA_BLOCK, h, hd).transpose(0, 3, 1, 2, 4)
    kb, vb = blocks(k), blocks(v)
    qs = q.reshape(n, n_qc, qc, h, hd)
    scale = hd ** -0.5
    n_cols_pad = max(0, MOBA_TOPK - n_blk)
    hidx = jnp.arange(h)[:, None, None]

    def per_seq(args):
        q_n, kb_n, vb_n = args
        kmean = jnp.mean(kb_n.astype(jnp.float32), axis=2)

        def per_chunk(cargs):
            qq, pp = cargs
            bq = pp // MOBA_BLOCK
            s = jnp.einsum('qhd,hjd->hqj', qq.astype(jnp.float32), kmean)
            s = jnp.pad(s, ((0, 0), (0, 0), (0, n_cols_pad)), constant_values=NEG_INF)
            j = jnp.arange(s.shape[-1])
            s = jnp.where(j[None, None, :] < bq[None, :, None], s, NEG_INF)
            _, top = lax.top_k(s, MOBA_TOPK)
            own = jnp.broadcast_to(jnp.minimum(bq, n_blk - 1)[None, :, None], (h, qc, 1))
            sel = jnp.concatenate([jnp.minimum(top, n_blk - 1), own], axis=-1)
            kg = kb_n[hidx, sel]
            vg = vb_n[hidx, sel]
            sc = jnp.einsum('qhd,hqrkd->hqrk', qq, kg, preferred_element_type=jnp.float32) * scale
            slot_ok = jnp.arange(MOBA_TOPK)[None, :] < bq[:, None]
            own_ok = jnp.arange(MOBA_BLOCK)[None, :] <= (pp - bq * MOBA_BLOCK)[:, None]
            mask = jnp.concatenate([jnp.broadcast_to(slot_ok[:, :, None], (qc, MOBA_TOPK, MOBA_BLOCK)),
                                    own_ok[:, None, :]], axis=1)
            sc = jnp.where(mask[None], sc, NEG_INF)
            p = jax.nn.softmax(sc.reshape(h, qc, -1), axis=-1).reshape(sc.shape)
            return jnp.einsum('hqrk,hqrke->qhe', p, vg.astype(jnp.float32))

        return lax.map(per_chunk, (q_n, pos))

    o = lax.map(per_seq, (qs, kb, vb))
    return o.reshape(n, tq_pad, h, hd)[:, :tq]


def _retention(q, k, v, s0):
    n, t, h, _ = q.shape
    c = min(RET_CHUNK, t)
    n_chunks = t // c
    log_g = jnp.log(1.0 - 2.0 ** (-5.0 - jnp.arange(H_R, dtype=jnp.float32)))
    i = jnp.arange(c, dtype=jnp.float32)
    diff = i[:, None] - i[None, :]
    dmat = jnp.where(diff >= 0, jnp.exp(jnp.maximum(diff, 0.0)[None] * log_g[:, None, None]), 0.0)
    dq = jnp.exp((i + 1.0)[:, None] * log_g[None, :])
    dk = jnp.exp((c - 1.0 - i)[:, None] * log_g[None, :])
    dc = jnp.exp(c * log_g)

    def step(s, chunk):
        qc, kc, vc = chunk
        att = jnp.einsum('nihd,njhd->nhij', qc, kc) * dmat[None]
        o = (jnp.einsum('nhij,njhe->nihe', att, vc)
             + jnp.einsum('nihd,nhde->nihe', qc * dq[None, :, :, None], s))
        s = s * dc[None, :, None, None] + jnp.einsum('njhd,njhe->nhde', kc * dk[None, :, :, None], vc)
        return s, o

    def to_chunks(a):
        return a.reshape(n, n_chunks, c, h, a.shape[-1]).transpose(1, 0, 2, 3, 4)

    s_fin, o = lax.scan(step, s0, (to_chunks(q), to_chunks(k), to_chunks(v)))
    return o.transpose(1, 0, 2, 3, 4).reshape(n, t, h, v.shape[-1]), s_fin


def _even_mixer(xn, start, past_k, past_v, s0, w_in, w_out):
    n, t, _ = xn.shape
    pos = start + jnp.arange(t)
    z = xn @ w_in
    q_a, k_a, v_a, q_r, k_r, v_r, g_r = jnp.split(z, SPLIT_EVEN, axis=-1)
    q_a = _partial_rope(q_a.reshape(n, t, H_A, HD_A), pos)
    k_a = _partial_rope(k_a.reshape(n, t, H_A, HD_A), pos)
    v_a = v_a.reshape(n, t, H_A, HD_A)
    k_all = k_a if past_k is None else jnp.concatenate([past_k.astype(k_a.dtype), k_a], axis=1)
    v_all = v_a if past_v is None else jnp.concatenate([past_v.astype(v_a.dtype), v_a], axis=1)
    o_a = _moba_attention(q_a, k_all, v_all, start)
    q_r = _retnet_rotate(q_r.reshape(n, t, H_R, DK_R), pos)
    k_r = _retnet_rotate(k_r.reshape(n, t, H_R, DK_R), pos) * (DK_R ** -0.5)
    v_r = v_r.reshape(n, t, H_R, DV_R).astype(jnp.float32)
    o_r, s_new = _retention(q_r, k_r, v_r, s0.astype(jnp.float32))
    o_r = o_r * lax.rsqrt(jnp.mean(o_r * o_r, axis=-1, keepdims=True) + EPS)
    o_r = o_r.reshape(n, t, R_V) * jax.nn.silu(g_r.astype(jnp.float32))
    mixed = jnp.concatenate([o_a.reshape(n, t, A_W), o_r], axis=-1).astype(xn.dtype)
    return mixed @ w_out, k_a, v_a, s_new


def _complex_affine_combine(e1, e2):
    a1r, a1i, b1r, b1i = e1
    a2r, a2i, b2r, b2i = e2
    return (a2r * a1r - a2i * a1i,
            a2r * a1i + a2i * a1r,
            a2r * b1r - a2i * b1i + b2r,
            a2r * b1i + a2i * b1r + b2i)


def _s5_mixer(u, s0_re, s0_im, lam_re, lam_im, log_dt, b_re, b_im, c_re, c_im, d_skip, w_glu, b_glu):
    n, t, _ = u.shape
    f32 = jnp.float32
    uf = u.astype(f32)
    ug = uf.reshape(n, t, S5_GROUPS, S5_GROUP)
    lr = lam_re.astype(f32)
    li = lam_im.astype(f32)
    dt = jnp.exp(log_dt.astype(f32))[:, None]
    mag = jnp.exp(lr * dt)
    ang = li * dt
    ab_re = mag * jnp.cos(ang)
    ab_im = mag * jnp.sin(ang)
    den = lr * lr + li * li
    nr = ab_re - 1.0
    co_re = (nr * lr + ab_im * li) / den
    co_im = (ab_im * lr - nr * li) / den
    br = b_re.astype(f32)
    bi = b_im.astype(f32)
    bb_re = co_re[..., None] * br - co_im[..., None] * bi
    bb_im = co_re[..., None] * bi + co_im[..., None] * br
    bu_re = jnp.einsum('ntgc,gpc->ntgp', ug, bb_re)
    bu_im = jnp.einsum('ntgc,gpc->ntgp', ug, bb_im)
    a_re = jnp.broadcast_to(ab_re, (1, t) + ab_re.shape)
    a_im = jnp.broadcast_to(ab_im, (1, t) + ab_im.shape)
    A_re, A_im, X_re, X_im = lax.associative_scan(_complex_affine_combine, (a_re, a_im, bu_re, bu_im), axis=1)
    sr = s0_re.astype(f32)[:, None]
    si = s0_im.astype(f32)[:, None]
    x_re = X_re + A_re * sr - A_im * si
    x_im = X_im + A_re * si + A_im * sr
    y = (jnp.einsum('gcp,ntgp->ntgc', c_re.astype(f32), x_re)
         - jnp.einsum('gcp,ntgp->ntgc', c_im.astype(f32), x_im)).reshape(n, t, D_MODEL)
    y = y + d_skip.astype(f32) * uf
    zg = jax.nn.gelu(y)
    hg = zg @ w_glu.astype(f32) + b_glu.astype(f32)
    out = hg[..., :D_MODEL] * jax.nn.sigmoid(hg[..., D_MODEL:])
    return out.astype(u.dtype), x_re[:, -1], x_im[:, -1]


def _hier_moe(x, w_group, b_group, w_expert, b_expert, w13, w2):
    n, t, d = x.shape
    xf = x.reshape(n * t, d)
    n_tok = n * t
    pg = jax.nn.softmax((xf @ w_group).astype(jnp.float32) + b_group.astype(jnp.float32), axis=-1)
    g = jnp.argmax(pg, axis=-1)
    wg = jnp.take_along_axis(pg, g[:, None], axis=1)[:, 0]
    le = ((xf @ w_expert).astype(jnp.float32) + b_expert.astype(jnp.float32)).reshape(n_tok, MOE_GROUPS, MOE_PER_GROUP)
    le_g = jnp.take_along_axis(le, g[:, None, None], axis=1)[:, 0]
    tv, ti = lax.top_k(le_g, MOE_TOPK)
    we = jax.nn.softmax(tv, axis=-1) * wg[:, None]
    eid = g[:, None] * MOE_PER_GROUP + ti
    n_assign = n_tok * MOE_TOPK
    e_flat = eid.reshape(-1)
    tok_flat = jnp.repeat(jnp.arange(n_tok), MOE_TOPK)
    w_flat = we.reshape(-1)
    order = jnp.argsort(e_flat)
    e_s, tok_s, w_s = e_flat[order], tok_flat[order], w_flat[order]
    counts = jnp.bincount(e_flat, length=N_EXPERTS)
    pc = (counts + MOE_BLOCK - 1) // MOE_BLOCK * MOE_BLOCK
    pend = jnp.cumsum(pc)
    pstart = pend - pc
    start = jnp.cumsum(counts) - counts
    dest = pstart[e_s] + jnp.arange(n_assign) - start[e_s]
    n_blocks = -(-n_assign // MOE_BLOCK) + N_EXPERTS
    rows = n_blocks * MOE_BLOCK
    buf_tok = jnp.zeros((rows,), jnp.int32).at[dest].set(tok_s.astype(jnp.int32))
    buf_w = jnp.zeros((rows,), jnp.float32).at[dest].set(w_s)
    blk_e = jnp.minimum(jnp.searchsorted(pend, jnp.arange(n_blocks) * MOE_BLOCK, side='right'), N_EXPERTS - 1)

    def run_block(args):
        toks, e = args
        hb = xf[toks] @ w13[e]
        return (jax.nn.silu(hb[:, :D_EXPERT]) * hb[:, D_EXPERT:]) @ w2[e]

    yb = lax.map(run_block, (buf_tok.reshape(n_blocks, MOE_BLOCK), blk_e))
    y = jnp.zeros((n_tok, d), jnp.float32).at[buf_tok].add(yb.reshape(rows, d).astype(jnp.float32) * buf_w[:, None])
    return y.astype(x.dtype).reshape(n, t, d)


def setup_inputs(seed: int = 0) -> dict:
    key = jax.random.key(seed)
    keys = jax.random.split(key, 32)
    f32 = jnp.float32
    n_pages = PAST_LEN // PAGE_SIZE
    n_pool = (DEC_BATCH * n_pages * 5) // 4

    def nrm(i, shape, scale):
        return jax.random.normal(keys[i], shape, f32) * scale

    page_table = jax.random.permutation(keys[7], n_pool)[: DEC_BATCH * n_pages].reshape(DEC_BATCH, n_pages).astype(jnp.int32)
    log_lo, log_hi = math.log(0.001), math.log(0.1)
    return {
        'x_prompt': nrm(0, (BATCH, SEQ, D_MODEL), 1.0),
        'x_sample': nrm(1, (DEC_BATCH, DEC_SEQ, D_MODEL), 1.0),
        'cache_k': nrm(2, (N_EVEN, n_pool, PAGE_SIZE, H_A, HD_A), 1.0),
        'cache_v': nrm(3, (N_EVEN, n_pool, PAGE_SIZE, H_A, HD_A), 1.0),
        'state_ret': nrm(4, (N_EVEN, DEC_BATCH, H_R, DK_R, DV_R), 0.5),
        'state_s5_re': nrm(5, (N_ODD, DEC_BATCH, S5_GROUPS, S5_STATE), 0.5),
        'state_s5_im': nrm(6, (N_ODD, DEC_BATCH, S5_GROUPS, S5_STATE), 0.5),
        'page_table': page_table,
        'norm1': 1.0 + nrm(8, (DEPTH, D_MODEL), 0.02),
        'norm2': 1.0 + nrm(9, (DEPTH, D_MODEL), 0.02),
        'norm_f': 1.0 + nrm(10, (D_MODEL,), 0.02),
        'w_in_even': nrm(11, (N_EVEN, D_MODEL, IN_EVEN), D_MODEL ** -0.5),
        'w_out_even': nrm(12, (N_EVEN, MIX_EVEN, D_MODEL), MIX_EVEN ** -0.5),
        's5_lam_re': -0.5 + nrm(13, (N_ODD, S5_GROUPS, S5_STATE), 0.01),
        's5_lam_im': math.pi * jnp.arange(S5_STATE, dtype=f32)[None, None, :] + nrm(14, (N_ODD, S5_GROUPS, S5_STATE), 0.01),
        's5_log_dt': log_lo + jax.random.uniform(keys[15], (N_ODD, S5_GROUPS), f32) * (log_hi - log_lo),
        's5_b_re': nrm(16, (N_ODD, S5_GROUPS, S5_STATE, S5_GROUP), (2.0 * S5_GROUP) ** -0.5),
        's5_b_im': nrm(17, (N_ODD, S5_GROUPS, S5_STATE, S5_GROUP), (2.0 * S5_GROUP) ** -0.5),
        's5_c_re': nrm(18, (N_ODD, S5_GROUPS, S5_GROUP, S5_STATE), (2.0 * S5_STATE) ** -0.5),
        's5_c_im': nrm(19, (N_ODD, S5_GROUPS, S5_GROUP, S5_STATE), (2.0 * S5_STATE) ** -0.5),
        's5_d': nrm(20, (N_ODD, D_MODEL), 1.0),
        's5_w_glu': nrm(21, (N_ODD, D_MODEL, 2 * D_MODEL), D_MODEL ** -0.5),
        's5_b_glu': nrm(22, (N_ODD, 2 * D_MODEL), 0.01),
        'moe_w_group': nrm(23, (DEPTH, D_MODEL, MOE_GROUPS), D_MODEL ** -0.5),
        'moe_b_group': nrm(24, (DEPTH, MOE_GROUPS), 0.01),
        'moe_w_expert': nrm(25, (DEPTH, D_MODEL, N_EXPERTS), D_MODEL ** -0.5),
        'moe_b_expert': nrm(26, (DEPTH, N_EXPERTS), 0.01),
        'moe_w13': nrm(27, (DEPTH, N_EXPERTS, D_MODEL, 2 * D_EXPERT), D_MODEL ** -0.5),
        'moe_w2': nrm(28, (DEPTH, N_EXPERTS, D_EXPERT, D_MODEL), D_EXPERT ** -0.5),
    }


def reference(x_prompt, x_sample, cache_k, cache_v, state_ret, state_s5_re, state_s5_im, page_table,
              norm1, norm2, norm_f, w_in_even, w_out_even,
              s5_lam_re, s5_lam_im, s5_log_dt, s5_b_re, s5_b_im, s5_c_re, s5_c_im, s5_d, s5_w_glu, s5_b_glu,
              moe_w_group, moe_b_group, moe_w_expert, moe_b_expert, moe_w13, moe_w2):
    n_p, seq, _ = x_prompt.shape
    n_s = x_sample.shape[0]
    page = cache_k.shape[2]
    past_len = page_table.shape[1] * page
    hp, hs = x_prompt, x_sample
    k_p, v_p, k_s, v_s, r_p, r_s = [], [], [], [], [], []
    sr_p, si_p, sr_s, si_s = [], [], [], []
    for li in range(DEPTH):
        xn_p = _rms(hp, norm1[li])
        xn_s = _rms(hs, norm1[li])
        if li % 2 == 0:
            e = li // 2
            past_k = cache_k[e, page_table].reshape(n_s, past_len, H_A, HD_A)
            past_v = cache_v[e, page_table].reshape(n_s, past_len, H_A, HD_A)
            zero_ret = jnp.zeros((n_p, H_R, DK_R, DV_R), jnp.float32)
            o_p, kk, vv, st = _even_mixer(xn_p, 0, None, None, zero_ret, w_in_even[e], w_out_even[e])
            k_p.append(kk.reshape(n_p, seq // page, page, H_A, HD_A))
            v_p.append(vv.reshape(n_p, seq // page, page, H_A, HD_A))
            r_p.append(st)
            o_s, kk, vv, st = _even_mixer(xn_s, past_len, past_k, past_v, state_ret[e], w_in_even[e], w_out_even[e])
            k_s.append(kk)
            v_s.append(vv)
            r_s.append(st)
        else:
            o = li // 2
            prm = (s5_lam_re[o], s5_lam_im[o], s5_log_dt[o], s5_b_re[o], s5_b_im[o], s5_c_re[o], s5_c_im[o],
                   s5_d[o], s5_w_glu[o], s5_b_glu[o])
            zero_s5 = jnp.zeros((n_p, S5_GROUPS, S5_STATE), jnp.float32)
            o_p, a_re, a_im = _s5_mixer(xn_p, zero_s5, zero_s5, *prm)
            sr_p.append(a_re)
            si_p.append(a_im)
            o_s, a_re, a_im = _s5_mixer(xn_s, state_s5_re[o], state_s5_im[o], *prm)
            sr_s.append(a_re)
            si_s.append(a_im)
        hp = hp + o_p
        hs = hs + o_s
        hp = hp + _hier_moe(_rms(hp, norm2[li]), moe_w_group[li], moe_b_group[li], moe_w_expert[li],
                            moe_b_expert[li], moe_w13[li], moe_w2[li])
        hs = hs + _hier_moe(_rms(hs, norm2[li]), moe_w_group[li], moe_b_group[li], moe_w_expert[li],
                            moe_b_expert[li], moe_w13[li], moe_w2[li])
    y_prompt = _rms(hp, norm_f)
    y_sample = _rms(hs, norm_f)
    return (y_prompt, y_sample, jnp.stack(k_p), jnp.stack(v_p), jnp.stack(k_s), jnp.stack(v_s),
            jnp.stack(r_p), jnp.stack(r_s), jnp.stack(sr_p), jnp.stack(si_p), jnp.stack(sr_s), jnp.stack(si_s))
```

```python
import functools
import math

import jax
import jax.numpy as jnp
from jax import lax
from jax.experimental import pallas as pl
from jax.experimental.pallas import tpu as pltpu

F32 = jnp.float32
BF16 = jnp.bfloat16

H_A, HD_A = 8, 64
ROT_DIM = HD_A // 4
ROPE_THETA = 500000.0
MOBA_BLOCK = 256
MOBA_TOPK = 3
H_R, DK_R, DV_R = 8, 64, 128
RET_CHUNK = 128
S5_GROUP, S5_STATE = 16, 64
MOE_GROUPS, MOE_PER_GROUP, MOE_TOPK = 4, 8, 2
N_EXPERTS = MOE_GROUPS * MOE_PER_GROUP
A_W = H_A * HD_A
R_QK = H_R * DK_R
R_V = H_R * DV_R
NEG_INF = -1e30
EPS = 1e-6

LANES = 128
SUBLANES = 8
VMEM_LIMIT = 56 * 1024 * 1024

ROW_TILE = 256
MOE_ROWS = 256


def _nt(a, b):
    return lax.dot_general(a, b, (((1,), (1,)), ((), ())), preferred_element_type=F32)


def _tn(a, b):
    return lax.dot_general(a, b, (((0,), (0,)), ((), ())), preferred_element_type=F32)


def _mm(a, b):
    return jnp.dot(a, b, preferred_element_type=F32)


def _rms_rows(x, g):
    return x * lax.rsqrt(jnp.mean(x * x, axis=-1, keepdims=True) + EPS) * g


def _split_bf16(x):
    hi = x.astype(BF16)
    lo = (x - hi.astype(F32)).astype(BF16)
    return hi, lo


def _params(sem):
    return pltpu.CompilerParams(dimension_semantics=sem, vmem_limit_bytes=VMEM_LIMIT)


def _rotate_into(out_ref, z, c_ref, s_ref, shift, first, scale):
    for j in range(z.shape[1] // LANES):
        sl = slice(j * LANES, (j + 1) * LANES)
        zc = z[:, sl]
        up = pltpu.roll(zc, LANES - shift, axis=1)
        dn = pltpu.roll(zc, shift, axis=1)
        r = zc * c_ref[:, sl] + jnp.where(first, up, dn) * s_ref[:, sl]
        out_ref[:, sl] = r if scale is None else r * scale


def _in_proj_kernel(x_ref, g_ref, w_ref, ca_ref, sa_ref, cr_ref, sr_ref,
                    qa_ref, ka_ref, va_ref, qr_ref, kr_ref, vr_ref, gr_ref):
    xn = _rms_rows(x_ref[...], g_ref[...]).astype(BF16)
    lane = lax.broadcasted_iota(jnp.int32, (1, LANES), 1)
    first_a = (lane % HD_A) < (ROT_DIM // 2)
    first_r = (lane % 2) == 0

    def sec(lo, width):
        return _mm(xn, w_ref[:, lo:lo + width])

    _rotate_into(qa_ref, sec(0, A_W), ca_ref, sa_ref, ROT_DIM // 2, first_a, None)
    _rotate_into(ka_ref, sec(A_W, A_W), ca_ref, sa_ref, ROT_DIM // 2, first_a, None)
    va_ref[...] = sec(2 * A_W, A_W)
    _rotate_into(qr_ref, sec(3 * A_W, R_QK), cr_ref, sr_ref, 1, first_r, None)
    _rotate_into(kr_ref, sec(3 * A_W + R_QK, R_QK), cr_ref, sr_ref, 1, first_r, DK_R ** -0.5)
    vr_ref[...] = sec(3 * A_W + 2 * R_QK, R_V)
    gr_ref[...] = sec(3 * A_W + 2 * R_QK + R_V, R_V)


def _rope_tables(pos):
    half = ROT_DIM // 2
    inv = ROPE_THETA ** (-jnp.arange(half, dtype=F32) / half)
    ang = pos.astype(F32)[:, None] * inv[None, :]
    cos, sin = jnp.cos(ang), jnp.sin(ang)
    rest = HD_A - ROT_DIM
    c = jnp.concatenate([cos, cos, jnp.ones((pos.shape[0], rest), F32)], axis=-1)
    s = jnp.concatenate([-sin, sin, jnp.zeros((pos.shape[0], rest), F32)], axis=-1)
    return jnp.tile(c, (1, H_A)), jnp.tile(s, (1, H_A))


def _retnet_tables(pos):
    n = DK_R // 2
    inv = 1.0 / (10000.0 ** jnp.linspace(0.0, 1.0, n, dtype=F32))
    ang = pos.astype(F32)[:, None] * inv[None, :]
    cos, sin = jnp.cos(ang), jnp.sin(ang)
    c = jnp.repeat(cos, 2, axis=-1)
    s = jnp.stack([-sin, sin], axis=-1).reshape(pos.shape[0], DK_R)
    return jnp.tile(c, (1, H_R)), jnp.tile(s, (1, H_R))


def _in_proj(h, g, w_bf16, seq, n_prompt_rows, dec_seq, past_len):
    n, d = h.shape
    tm = ROW_TILE
    tiles_per_seq = seq // tm
    n_ptiles = n_prompt_rows // tm
    pos = jnp.concatenate([jnp.arange(seq), past_len + (jnp.arange(tm) % dec_seq)])
    ca, sa = _rope_tables(pos)
    cr, sr = _retnet_tables(pos)

    def tab_map(i):
        return (jnp.where(i < n_ptiles, i % tiles_per_seq, tiles_per_seq), 0)

    row = lambda w: pl.BlockSpec((tm, w), lambda i: (i, 0))
    tab = pl.BlockSpec((tm, A_W), tab_map)
    widths = (A_W, A_W, A_W, R_QK, R_QK, R_V, R_V)
    return pl.pallas_call(
        _in_proj_kernel,
        grid=(n // tm,),
        in_specs=[row(d), pl.BlockSpec((1, d), lambda i: (0, 0)),
                  pl.BlockSpec(w_bf16.shape, lambda i: (0, 0)), tab, tab, tab, tab],
        out_specs=[row(w) for w in widths],
        out_shape=[jax.ShapeDtypeStruct((n, w), F32) for w in widths],
        compiler_params=_params(("arbitrary",)),
        name="in_proj",
    )(h, g.reshape(1, d), w_bf16, ca, sa, cr, sr)


def _moba_select(q_f32, kmean, n_valid):
    n_blk = kmean.shape[0]
    qh, ql = _split_bf16(q_f32)
    kh, kl = _split_bf16(kmean)
    st = _nt(jnp.concatenate([kh, kl, kh], axis=1), jnp.concatenate([qh, qh, ql], axis=1))
    jrow = lax.broadcasted_iota(jnp.int32, st.shape, 0)
    rank = jnp.zeros(st.shape, F32)
    for jp in range(n_blk):
        sj = st[jp:jp + 1, :]
        beats = (sj > st) | ((sj == st) & (jp < jrow))
        rank = rank + jnp.where(beats & (jp < n_valid), 1.0, 0.0)
    sel_t = jnp.where((jrow < n_valid) & (rank < MOBA_TOPK), 1.0, 0.0).astype(BF16)
    rows = q_f32.shape[0]
    eye = (lax.broadcasted_iota(jnp.int32, (rows, rows), 0)
           == lax.broadcasted_iota(jnp.int32, (rows, rows), 1))
    return _nt(jnp.where(eye, 1.0, 0.0).astype(BF16), sel_t)


def _softmax_step(carry, s, v_bf16):
    m, l, acc = carry
    m_new = jnp.maximum(m, jnp.max(s, axis=1, keepdims=True))
    alpha = jnp.exp(m - m_new)
    p = jnp.exp(s - m_new)
    l = alpha * l + jnp.sum(p, axis=1, keepdims=True)
    acc = alpha * acc + _mm(p.astype(BF16), v_bf16)
    return m_new, l, acc


def _moba_prompt_kernel(q_ref, k_ref, v_ref, o_ref, kb_ref, vb_ref, km_ref, sel_ref):
    b = pl.program_id(2)
    blk = MOBA_BLOCK
    n_blk = k_ref.shape[0] // blk

    @pl.when(b == 0)
    def _():
        kb_ref[...] = k_ref[...].astype(BF16)
        vb_ref[...] = v_ref[...].astype(BF16)
        for j in range(n_blk):
            km_ref[j:j + 1, :] = jnp.mean(k_ref[j * blk:(j + 1) * blk, :], axis=0, keepdims=True)

    q = q_ref[...]
    lane = lax.broadcasted_iota(jnp.int32, (1, LANES), 1)
    causal = (lax.broadcasted_iota(jnp.int32, (blk, blk), 1)
              <= lax.broadcasted_iota(jnp.int32, (blk, blk), 0))
    own = pl.ds(pl.multiple_of(b * blk, blk), blk)
    out = jnp.zeros(q.shape, F32)
    for hh in range(LANES // HD_A):
        hm = (lane // HD_A) == hh
        qh = jnp.where(hm, q, 0.0)
        sel = _moba_select(qh, km_ref[...], b)
        for j in range(n_blk):
            sel_ref[j] = sel[:, j:j + 1]
        qs = (qh * (HD_A ** -0.5)).astype(BF16)

        s0 = jnp.where(causal, _nt(qs, kb_ref[own, :]), NEG_INF)
        init = (jnp.full((blk, 1), NEG_INF, F32), jnp.zeros((blk, 1), F32), jnp.zeros(q.shape, F32))
        carry = _softmax_step(init, s0, vb_ref[own, :])

        def past(j, carry):
            rows = pl.ds(pl.multiple_of(j * blk, blk), blk)
            s = jnp.where(sel_ref[j] > 0.5, _nt(qs, kb_ref[rows, :]), NEG_INF)
            return _softmax_step(carry, s, vb_ref[rows, :])

        _, l, acc = lax.fori_loop(0, b, past, carry)
        out = out + jnp.where(hm, acc / l, 0.0)
    o_ref[...] = out


def _moba_prompt(q_a, k_a, v_a, n_seq, seq):
    blk = MOBA_BLOCK
    n_blk = seq // blk
    qspec = pl.BlockSpec((blk, LANES), lambda s, h, b: (s * n_blk + b, h))
    kspec = pl.BlockSpec((seq, LANES), lambda s, h, b: (s, h))
    return pl.pallas_call(
        _moba_prompt_kernel,
        grid=(n_seq, A_W // LANES, n_blk),
        in_specs=[qspec, kspec, kspec],
        out_specs=qspec,
        out_shape=jax.ShapeDtypeStruct((n_seq * seq, A_W), F32),
        scratch_shapes=[pltpu.VMEM((seq, LANES), BF16), pltpu.VMEM((seq, LANES), BF16),
                        pltpu.VMEM((n_blk, LANES), F32), pltpu.VMEM((n_blk, blk, 1), F32)],
        compiler_params=_params(("arbitrary", "arbitrary", "arbitrary")),
        name="moba_prompt",
    )(q_a, k_a, v_a)


def _ret_decay_tables(c):
    log_g = jnp.log(1.0 - 2.0 ** (-5.0 - jnp.arange(H_R, dtype=F32)))
    i = jnp.arange(c, dtype=F32)
    diff = i[:, None] - i[None, :]
    dmat = jnp.where(diff >= 0, jnp.exp(jnp.maximum(diff, 0.0)[None] * log_g[:, None, None]), 0.0)
    dq = jnp.exp((i + 1.0)[None, :] * log_g[:, None])
    dk = jnp.exp((c - 1.0 - i)[None, :] * log_g[:, None])
    dc = jnp.exp(c * log_g)
    return dmat, dq, dk, dc


def _ret_chunk(q, k, v, g, s, dmat, dq, dk, dc):
    att = _nt(q.astype(BF16), k.astype(BF16)) * dmat
    o = _mm(att.astype(BF16), v.astype(BF16)) + _mm((q * dq).astype(BF16), s.astype(BF16))
    s = s * dc + _tn((k * dk).astype(BF16), v.astype(BF16))
    o = o * lax.rsqrt(jnp.mean(o * o, axis=-1, keepdims=True) + EPS)
    return o * (g * jax.nn.sigmoid(g)), s


def _ret_prompt_kernel(q_ref, k_ref, v_ref, g_ref, dmat_ref, dq_ref, dk_ref, dc_ref, o_ref, st_ref):
    c = RET_CHUNK
    hh = pl.program_id(1) % (LANES // DK_R)
    lane = lax.broadcasted_iota(jnp.int32, (1, LANES), 1)
    hm = (lane // DK_R) == hh
    dmat, dq, dk, dc = dmat_ref[...], dq_ref[...], dk_ref[...], dc_ref[...]

    def chunk(i, s):
        rows = pl.ds(pl.multiple_of(i * c, c), c)
        q = jnp.where(hm, q_ref[rows, :], 0.0)
        k = jnp.where(hm, k_ref[rows, :], 0.0)
        o, s = _ret_chunk(q, k, v_ref[rows, :], g_ref[rows, :], s, dmat, dq, dk, dc)
        o_ref[rows, :] = o
        return s

    s = lax.fori_loop(0, q_ref.shape[0] // c, chunk, jnp.zeros((LANES, DV_R), F32))
    st_ref[...] = jnp.where(hh == 0, s[:DK_R, :], s[DK_R:, :])


def _ret_tables_bcast(c, rows):
    dmat, dq, dk, dc = _ret_decay_tables(c)
    pad = rows - c
    dmat = jnp.pad(dmat, ((0, 0), (0, pad), (0, pad)))
    dq = jnp.broadcast_to(jnp.pad(dq, ((0, 0), (0, pad)))[:, :, None], (H_R, rows, LANES))
    dk = jnp.broadcast_to(jnp.pad(dk, ((0, 0), (0, pad)))[:, :, None], (H_R, rows, LANES))
    dc = jnp.broadcast_to(dc[:, None, None], (H_R, 1, LANES))
    return dmat, dq, dk, dc


def _ret_prompt(q_r, k_r, v_r, g_r, n_seq, seq):
    c = RET_CHUNK
    dmat, dq, dk, dc = _ret_tables_bcast(c, c)
    qk = pl.BlockSpec((seq, LANES), lambda s, h: (s, h // (LANES // DK_R)))
    vg = pl.BlockSpec((seq, DV_R), lambda s, h: (s, h))
    tab = lambda r: pl.BlockSpec((None, r, LANES), lambda s, h: (h, 0, 0))
    return pl.pallas_call(
        _ret_prompt_kernel,
        grid=(n_seq, H_R),
        in_specs=[qk, qk, vg, vg, tab(c), tab(c), tab(c), tab(1)],
        out_specs=[vg, pl.BlockSpec((None, None, DK_R, DV_R), lambda s, h: (s, h, 0, 0))],
        out_shape=[jax.ShapeDtypeStruct((n_seq * seq, R_V), F32),
                   jax.ShapeDtypeStruct((n_seq, H_R, DK_R, DV_R), F32)],
        compiler_params=_params(("arbitrary", "arbitrary")),
        name="ret_prompt",
    )(q_r, k_r, v_r, g_r, dmat, dq, dk, dc)


def _pad_rows(x, rows):
    return jnp.concatenate([x, jnp.zeros((rows - x.shape[0], x.shape[1]), x.dtype)], axis=0)


def _moba_sample_kernel(pt_ref, q_ref, kn_ref, vn_ref, *rest, n_pages):
    k_refs, v_refs, o_ref = rest[:n_pages], rest[n_pages:2 * n_pages], rest[2 * n_pages]
    q = q_ref[...]
    t = q.shape[0]
    page = k_refs[0].shape[0]
    per_blk = MOBA_BLOCK // page
    n_blk = n_pages // per_blk
    lane = lax.broadcasted_iota(jnp.int32, (1, A_W), 1)
    qbd = jnp.concatenate([jnp.where((lane // HD_A) == h, q, 0.0) for h in range(H_A)], axis=0)
    km = jnp.concatenate(
        [sum(jnp.sum(k_refs[j * per_blk + i][...], axis=0, keepdims=True) for i in range(per_blk))
         * (1.0 / MOBA_BLOCK) for j in range(n_blk)], axis=0)
    sel = _moba_select(qbd, km, n_blk)
    qs = (qbd * (HD_A ** -0.5)).astype(BF16)

    rows = qbd.shape[0]
    qi = lax.broadcasted_iota(jnp.int32, (rows, LANES), 0) % t
    causal = lax.broadcasted_iota(jnp.int32, (rows, LANES), 1) <= qi
    s0 = jnp.where(causal, _nt(qs, _pad_rows(kn_ref[...], LANES).astype(BF16)), NEG_INF)
    init = (jnp.full((rows, 1), NEG_INF, F32), jnp.zeros((rows, 1), F32), jnp.zeros((rows, A_W), F32))
    carry = _softmax_step(init, s0, _pad_rows(vn_ref[...], LANES).astype(BF16))
    for p in range(n_pages):
        j = p // per_blk
        s = jnp.where(sel[:, j:j + 1] > 0.5, _nt(qs, k_refs[p][...].astype(BF16)), NEG_INF)
        carry = _softmax_step(carry, s, v_refs[p][...].astype(BF16))
    _, l, acc = carry
    o = acc / l
    out = jnp.zeros((t, A_W), F32)
    for h in range(H_A):
        out = out + jnp.where((lane // HD_A) == h, o[h * t:(h + 1) * t, :], 0.0)
    o_ref[...] = out


def _moba_sample(q_a, k_a, v_a, cache_k, cache_v, page_table, n_prompt_rows, dec_seq):
    n_s, n_pages = page_table.shape
    pool, page = cache_k.shape[0], cache_k.shape[1]
    base = n_prompt_rows // dec_seq
    new = pl.BlockSpec((dec_seq, A_W), lambda n, pt: (base + n, 0))
    pages = [pl.BlockSpec((None, page, A_W), lambda n, pt, p=p: (pt[n * n_pages + p], 0, 0))
             for p in range(n_pages)]
    grid_spec = pltpu.PrefetchScalarGridSpec(
        num_scalar_prefetch=1,
        grid=(n_s,),
        in_specs=[new, new, new] + pages + pages,
        out_specs=pl.BlockSpec((dec_seq, A_W), lambda n, pt: (n, 0)),
    )
    return pl.pallas_call(
        functools.partial(_moba_sample_kernel, n_pages=n_pages),
        grid_spec=grid_spec,
        out_shape=jax.ShapeDtypeStruct((n_s * dec_seq, A_W), F32),
        compiler_params=_params(("arbitrary",)),
        name="moba_sample",
    )(page_table.reshape(-1), q_a, k_a, v_a, *([cache_k] * n_pages), *([cache_v] * n_pages))


def _ret_sample_kernel(q_ref, k_ref, v_ref, g_ref, s0_ref, dmat_ref, dq_ref, dk_ref, dc_ref, o_ref, st_ref):
    t = q_ref.shape[0]
    lane = lax.broadcasted_iota(jnp.int32, (1, LANES), 1)
    per = LANES // DK_R
    zero_half = jnp.zeros((DK_R, DV_R), F32)
    for h in range(H_R):
        hh = h % per
        hm = (lane // DK_R) == hh
        qk_l = slice((h // per) * LANES, (h // per + 1) * LANES)
        v_l = slice(h * DV_R, (h + 1) * DV_R)
        q = _pad_rows(jnp.where(hm, q_ref[:, qk_l], 0.0), LANES)
        k = _pad_rows(jnp.where(hm, k_ref[:, qk_l], 0.0), LANES)
        v = _pad_rows(v_ref[:, v_l], LANES)
        g = _pad_rows(g_ref[:, v_l], LANES)
        halves = [zero_half] * per
        halves[hh] = s0_ref[h]
        o, s = _ret_chunk(q, k, v, g, jnp.concatenate(halves, axis=0),
                          dmat_ref[h], dq_ref[h], dk_ref[h], dc_ref[h])
        o_ref[:, v_l] = o[:t, :]
        st_ref[h] = s[hh * DK_R:(hh + 1) * DK_R, :]


def _ret_sample(q_r, k_r, v_r, g_r, s0, n_prompt_rows, dec_seq):
    n_s = s0.shape[0]
    base = n_prompt_rows // dec_seq
    dmat, dq, dk, dc = _ret_tables_bcast(dec_seq, LANES)
    qk = pl.BlockSpec((dec_seq, R_QK), lambda n: (base + n, 0))
    vg = pl.BlockSpec((dec_seq, R_V), lambda n: (base + n, 0))
    st = pl.BlockSpec((None, H_R, DK_R, DV_R), lambda n: (n, 0, 0, 0))
    full = lambda a: pl.BlockSpec(a.shape, lambda n: (0,) * a.ndim)
    return pl.pallas_call(
        _ret_sample_kernel,
        grid=(n_s,),
        in_specs=[qk, qk, vg, vg, st, full(dmat), full(dq), full(dk), full(dc)],
        out_specs=[pl.BlockSpec((dec_seq, R_V), lambda n: (n, 0)), st],
        out_shape=[jax.ShapeDtypeStruct((n_s * dec_seq, R_V), F32), jax.ShapeDtypeStruct(s0.shape, F32)],
        compiler_params=_params(("arbitrary",)),
        name="ret_sample",
    )(q_r, k_r, v_r, g_r, s0, dmat, dq, dk, dc)


ROUTER_ROWS = SUBLANES + N_EXPERTS


def _route(xn, wr_ref, br_ref, eid_ref, we_ref):
    xh, xl = _split_bf16(xn)
    lt = _nt(wr_ref[...], jnp.concatenate([xh, xh, xl], axis=1)) + br_ref[...]
    tm = lt.shape[1]
    r8 = lax.broadcasted_iota(jnp.int32, (SUBLANES, tm), 0)
    lg = jnp.where(r8 < MOE_GROUPS, lt[:SUBLANES, :], NEG_INF)
    mg = jnp.max(lg, axis=0, keepdims=True)
    wg = 1.0 / jnp.sum(jnp.exp(lg - mg), axis=0, keepdims=True)
    gidx = jnp.min(jnp.where(lg == mg, r8, SUBLANES), axis=0, keepdims=True)
    le = jnp.zeros((MOE_PER_GROUP, tm), F32)
    for gi in range(MOE_GROUPS):
        lo = SUBLANES + gi * MOE_PER_GROUP
        le = le + jnp.where(gidx == gi, lt[lo:lo + MOE_PER_GROUP, :], 0.0)
    v1 = jnp.max(le, axis=0, keepdims=True)
    i1 = jnp.min(jnp.where(le == v1, r8, MOE_PER_GROUP), axis=0, keepdims=True)
    le2 = jnp.where(r8 == i1, -jnp.inf, le)
    v2 = jnp.max(le2, axis=0, keepdims=True)
    i2 = jnp.min(jnp.where(le2 == v2, r8, MOE_PER_GROUP), axis=0, keepdims=True)
    e21 = jnp.exp(v2 - v1)
    w1 = wg / (1.0 + e21)
    eid_ref[0:1, :] = gidx * MOE_PER_GROUP + i1
    eid_ref[1:2, :] = gidx * MOE_PER_GROUP + i2
    we_ref[0:1, :] = w1
    we_ref[1:2, :] = w1 * e21


def _proj_router_kernel(*refs, n_in, glu):
    a_refs, w_refs = refs[:n_in], refs[n_in:2 * n_in]
    k = 2 * n_in
    b_ref = refs[k] if glu else None
    k += int(glu)
    h_ref, g_ref, wr_ref, br_ref, ho_ref, xn_ref, eid_ref, we_ref = refs[k:k + 8]
    acc = None
    for a_ref, w_ref in zip(a_refs, w_refs):
        part = _mm(a_ref[...].astype(BF16), w_ref[...])
        acc = part if acc is None else acc + part
    if glu:
        acc = acc + b_ref[...]
        half = acc.shape[1] // 2
        acc = acc[:, :half] * jax.nn.sigmoid(acc[:, half:])
    hn = h_ref[...] + acc
    ho_ref[...] = hn
    xn = _rms_rows(hn, g_ref[...])
    xn_ref[...] = xn
    _route(xn, wr_ref, br_ref, eid_ref, we_ref)


def _router_weights(w_group, b_group, w_expert, b_expert):
    d = w_group.shape[0]
    wt = jnp.concatenate([w_group.T, jnp.zeros((SUBLANES - MOE_GROUPS, d), F32), w_expert.T], axis=0)
    hi = wt.astype(BF16)
    lo = (wt - hi.astype(F32)).astype(BF16)
    bias = jnp.concatenate([b_group, jnp.zeros((SUBLANES - MOE_GROUPS,), F32), b_expert]).reshape(-1, 1)
    return jnp.concatenate([hi, lo, hi], axis=1), bias


def _proj_router(acts, weights, bias, h, g, router, glu):
    n, d = h.shape
    tm = ROW_TILE
    wr, br = router
    row = lambda w: pl.BlockSpec((tm, w), lambda i: (i, 0))
    full = lambda a: pl.BlockSpec(a.shape, lambda i: (0,) * a.ndim)
    tok = pl.BlockSpec((MOE_TOPK, tm), lambda i: (0, i))
    operands = list(acts) + list(weights) + ([bias.reshape(1, -1)] if glu else []) + [h, g.reshape(1, d), wr, br]
    in_specs = ([row(a.shape[1]) for a in acts] + [full(w) for w in weights]
                + ([pl.BlockSpec((1, bias.shape[0]), lambda i: (0, 0))] if glu else [])
                + [row(d), pl.BlockSpec((1, d), lambda i: (0, 0)), full(wr), full(br)])
    return pl.pallas_call(
        functools.partial(_proj_router_kernel, n_in=len(acts), glu=glu),
        grid=(n // tm,),
        in_specs=in_specs,
        out_specs=[row(d), row(d), tok, tok],
        out_shape=[jax.ShapeDtypeStruct((n, d), F32), jax.ShapeDtypeStruct((n, d), F32),
                   jax.ShapeDtypeStruct((MOE_TOPK, n), jnp.int32), jax.ShapeDtypeStruct((MOE_TOPK, n), F32)],
        compiler_params=_params(("arbitrary",)),
        name="glu_router" if glu else "out_proj_router",
    )(*operands)


def _moe_plan(eid, we):
    n_tok = eid.shape[1]
    n_assign = MOE_TOPK * n_tok
    blk = MOE_ROWS
    n_blocks = -(-n_assign // blk) + N_EXPERTS
    rows = n_blocks * blk
    e_flat = eid.reshape(-1)
    onehot = (e_flat[:, None] == jnp.arange(N_EXPERTS, dtype=jnp.int32)[None, :]).astype(jnp.int32)
    csum = jnp.cumsum(onehot, axis=0)
    pos = jnp.take_along_axis(csum, e_flat[:, None], axis=1)[:, 0] - 1
    counts = csum[-1]
    pc = (counts + blk - 1) // blk * blk
    pend = jnp.cumsum(pc)
    dest = (pend - pc)[e_flat] + pos
    a = jnp.arange(n_assign, dtype=jnp.int32)
    src_tok = jnp.zeros((rows,), jnp.int32).at[dest].set(a % n_tok)
    dst_row = jnp.full((rows,), -1, jnp.int32).at[dest].set(a)
    w_row = jnp.zeros((rows,), F32).at[dest].set(we.reshape(-1))
    blk_e = jnp.minimum(jnp.searchsorted(pend, jnp.arange(n_blocks, dtype=jnp.int32) * blk, side='right'),
                        N_EXPERTS - 1).astype(jnp.int32)
    n_used = (pend[-1] // blk).astype(jnp.int32).reshape(1)
    return blk_e, n_used, src_tok, dst_row, w_row.reshape(rows, 1), n_blocks


def _moe_kernel(blk_e_ref, n_used_ref, src_ref, dst_ref, x_hbm, w13_ref, w2_ref, wrow_ref, y_hbm,
                xbuf, ybuf, gsem, ssem):
    b = pl.program_id(0)
    n_used = n_used_ref[0]
    blk = MOE_ROWS
    slot = b % 2

    def gather_row(block, slot, r):
        return pltpu.make_async_copy(x_hbm.at[pl.ds(src_ref[block * blk + r], 1), :],
                                     xbuf.at[slot, pl.ds(r, 1), :], gsem.at[slot])

    def scatter_row(block, slot, r, dst):
        return pltpu.make_async_copy(ybuf.at[slot, pl.ds(r, 1), :],
                                     y_hbm.at[pl.ds(dst, 1), :], ssem.at[slot])

    def start_gather(block, slot):
        def body(r, c):
            gather_row(block, slot, r).start()
            return c
        lax.fori_loop(0, blk, body, 0)

    def wait_gather(block, slot):
        def body(r, c):
            gather_row(block, slot, r).wait()
            return c
        lax.fori_loop(0, blk, body, 0)

    def for_valid_rows(block, slot, fn):
        def body(r, c):
            dst = dst_ref[block * blk + r]

            @pl.when(dst >= 0)
            def _():
                fn(scatter_row(block, slot, r, dst))
            return c
        lax.fori_loop(0, blk, body, 0)

    @pl.when((b == 0) & (n_used > 0))
    def _():
        start_gather(0, 0)

    @pl.when(b + 1 < n_used)
    def _():
        start_gather(b + 1, 1 - slot)

    @pl.when(b < n_used)
    def _():
        wait_gather(b, slot)

        @pl.when(b >= 2)
        def _():
            for_valid_rows(b - 2, slot, lambda cp: cp.wait())

        hb = _mm(xbuf[slot].astype(BF16), w13_ref[...])
        half = hb.shape[1] // 2
        gate = hb[:, :half]
        act = gate * jax.nn.sigmoid(gate) * hb[:, half:]
        ybuf[slot] = _mm(act.astype(BF16), w2_ref[...]) * wrow_ref[...]
        for_valid_rows(b, slot, lambda cp: cp.start())

    @pl.when(b == pl.num_programs(0) - 1)
    def _():
        for back in (2, 1):
            last = n_used - back

            @pl.when(last >= 0)
            def _():
                for_valid_rows(last, last % 2, lambda cp: cp.wait())


def _moe_experts(xn, eid, we, w13_bf16, w2_bf16):
    n_tok, d = xn.shape
    blk = MOE_ROWS
    blk_e, n_used, src_tok, dst_row, w_row, n_blocks = _moe_plan(eid, we)
    grid_spec = pltpu.PrefetchScalarGridSpec(
        num_scalar_prefetch=4,
        grid=(n_blocks,),
        in_specs=[pl.BlockSpec(memory_space=pl.ANY),
                  pl.BlockSpec((None,) + w13_bf16.shape[1:], lambda b, be, nu, s, t: (be[b], 0, 0)),
                  pl.BlockSpec((None,) + w2_bf16.shape[1:], lambda b, be, nu, s, t: (be[b], 0, 0)),
                  pl.BlockSpec((blk, 1), lambda b, be, nu, s, t: (b, 0))],
        out_specs=pl.BlockSpec(memory_space=pl.ANY),
        scratch_shapes=[pltpu.VMEM((2, blk, d), F32), pltpu.VMEM((2, blk, d), F32),
                        pltpu.SemaphoreType.DMA((2,)), pltpu.SemaphoreType.DMA((2,))],
    )
    return pl.pallas_call(
        _moe_kernel,
        grid_spec=grid_spec,
        out_shape=jax.ShapeDtypeStruct((MOE_TOPK * n_tok, d), F32),
        compiler_params=_params(("arbitrary",)),
        name="moe_experts",
    )(blk_e, n_used, src_tok, dst_row, xn, w13_bf16, w2_bf16, w_row)


def _combine_kernel(h_ref, y0_ref, y1_ref, g_ref, *out_refs, norm):
    h = h_ref[...] + (y0_ref[...] + y1_ref[...])
    out_refs[0][...] = _rms_rows(h, g_ref[...]) if norm else h


def _moe_combine(h, y, g, norm):
    n, d = h.shape
    tm = ROW_TILE
    row = pl.BlockSpec((tm, d), lambda i: (i, 0))
    return pl.pallas_call(
        functools.partial(_combine_kernel, norm=norm),
        grid=(n // tm,),
        in_specs=[row, row, pl.BlockSpec((tm, d), lambda i: (n // tm + i, 0)),
                  pl.BlockSpec((1, d), lambda i: (0, 0))],
        out_specs=row,
        out_shape=jax.ShapeDtypeStruct((n, d), F32),
        compiler_params=_params(("arbitrary",)),
        name="moe_combine_norm" if norm else "moe_combine",
    )(h, y, y, g.reshape(1, d))


S5_CH = LANES
S5_ST = (LANES // S5_GROUP) * S5_STATE
S5_SCAN_LANES = 512


def _s5_kernel(h_ref, g_ref, bre_ref, bim_ref, are_ref, aim_ref, cre_ref, cim_ref, d_ref, s0r_ref, s0i_ref,
               zg_ref, fr_ref, fi_ref, xr, xi, sr, si, *, s, steps):
    c = pl.program_id(1)

    @pl.when(c == 0)
    def _():
        sr[...] = s0r_ref[...]
        si[...] = s0i_ref[...]

    u = _rms_rows(h_ref[...], g_ref[...])
    ub = u.astype(BF16)
    n_ch = u.shape[1] // S5_CH
    for j in range(n_ch):
        uj = ub[:, j * S5_CH:(j + 1) * S5_CH]
        xr[:, j * S5_ST:(j + 1) * S5_ST] = _mm(uj, bre_ref[j])
        xi[:, j * S5_ST:(j + 1) * S5_ST] = _mm(uj, bim_ref[j])

    for lo in range(0, xr.shape[1], S5_SCAN_LANES):
        ls = slice(lo, lo + S5_SCAN_LANES)
        ar = jnp.broadcast_to(are_ref[:, ls], (s, S5_SCAN_LANES))
        ai = jnp.broadcast_to(aim_ref[:, ls], (s, S5_SCAN_LANES))

        def step(t, carry):
            pr, pi = carry
            rows = pl.ds(pl.multiple_of(t * s, s), s)
            nr = ar * pr - ai * pi + xr[rows, ls]
            ni = ar * pi + ai * pr + xi[rows, ls]
            xr[rows, ls] = nr
            xi[rows, ls] = ni
            return nr, ni

        pr, pi = lax.fori_loop(0, steps, step, (sr[:, ls], si[:, ls]))
        sr[:, ls] = pr
        si[:, ls] = pi

    for j in range(n_ch):
        cs = slice(j * S5_CH, (j + 1) * S5_CH)
        ss = slice(j * S5_ST, (j + 1) * S5_ST)
        y = _mm(xr[:, ss].astype(BF16), cre_ref[j]) - _mm(xi[:, ss].astype(BF16), cim_ref[j])
        zg_ref[:, cs] = jax.nn.gelu(y + d_ref[:, cs] * u[:, cs])

    @pl.when(c == pl.num_programs(1) - 1)
    def _():
        fr_ref[...] = sr[...]
        fi_ref[...] = si[...]


def _s5_weights(lam_re, lam_im, log_dt, b_re, b_im, c_re, c_im):
    dt = jnp.exp(log_dt)[:, None]
    mag = jnp.exp(lam_re * dt)
    ang = lam_im * dt
    ab_re = mag * jnp.cos(ang)
    ab_im = mag * jnp.sin(ang)
    den = lam_re * lam_re + lam_im * lam_im
    nr = ab_re - 1.0
    co_re = (nr * lam_re + ab_im * lam_im) / den
    co_im = (ab_im * lam_re - nr * lam_im) / den
    bb_re = co_re[..., None] * b_re - co_im[..., None] * b_im
    bb_im = co_re[..., None] * b_im + co_im[..., None] * b_re
    per = S5_CH // S5_GROUP
    n_ch = lam_re.shape[0] // per
    eye = jnp.eye(per, dtype=F32)

    def in_blocks(bb):
        w = bb.transpose(0, 2, 1).reshape(n_ch, per, S5_GROUP, S5_STATE)
        return jnp.einsum('jgcp,gh->jgchp', w, eye).reshape(n_ch, S5_CH, S5_ST).astype(BF16)

    def out_blocks(cc):
        w = cc.transpose(0, 2, 1).reshape(n_ch, per, S5_STATE, S5_GROUP)
        return jnp.einsum('jgpc,gh->jgphc', w, eye).reshape(n_ch, S5_ST, S5_CH).astype(BF16)

    return (in_blocks(bb_re), in_blocks(bb_im), ab_re.reshape(1, -1), ab_im.reshape(1, -1),
            out_blocks(c_re), out_blocks(c_im))


def _s5_scan(h, g, weights, d_skip, s0_re, s0_im, s, steps):
    n, d = h.shape
    n_groups = s0_re.shape[0] // s
    rows = steps * s
    chunks = n // (n_groups * rows)
    bre, bim, are, aim, cre, cim = weights
    n_state = are.shape[1]
    full = lambda a: pl.BlockSpec(a.shape, lambda gi, c: (0,) * a.ndim)
    row = pl.BlockSpec((rows, d), lambda gi, c: (gi * chunks + c, 0))
    state = pl.BlockSpec((s, n_state), lambda gi, c: (gi, 0))
    return pl.pallas_call(
        functools.partial(_s5_kernel, s=s, steps=steps),
        grid=(n_groups, chunks),
        in_specs=[row, pl.BlockSpec((1, d), lambda gi, c: (0, 0)), full(bre), full(bim), full(are), full(aim),
                  full(cre), full(cim), pl.BlockSpec((1, d), lambda gi, c: (0, 0)), state, state],
        out_specs=[row, state, state],
        out_shape=[jax.ShapeDtypeStruct((n, d), F32), jax.ShapeDtypeStruct(s0_re.shape, F32),
                   jax.ShapeDtypeStruct(s0_re.shape, F32)],
        scratch_shapes=[pltpu.VMEM((rows, n_state), F32), pltpu.VMEM((rows, n_state), F32),
                        pltpu.VMEM((s, n_state), F32), pltpu.VMEM((s, n_state), F32)],
        compiler_params=_params(("arbitrary", "arbitrary")),
        name="s5_scan",
    )(h, g.reshape(1, d), bre, bim, are, aim, cre, cim, d_skip.reshape(1, d), s0_re, s0_im)


S5_PROMPT_STEPS = 32
S5_SAMPLE_SEQS = 32


def kernel(x_prompt, x_sample, cache_k, cache_v, state_ret, state_s5_re, state_s5_im, page_table, norm1, norm2, norm_f, w_in_even, w_out_even, s5_lam_re, s5_lam_im, s5_log_dt, s5_b_re, s5_b_im, s5_c_re, s5_c_im, s5_d, s5_w_glu, s5_b_glu, moe_w_group, moe_b_group, moe_w_expert, moe_b_expert, moe_w13, moe_w2):
    n_p, seq, d = x_prompt.shape
    n_s, dec_seq, _ = x_sample.shape
    pool, page = cache_k.shape[1], cache_k.shape[2]
    past_len = page_table.shape[1] * page
    np_rows, ns_rows = n_p * seq, n_s * dec_seq
    routers = [_router_weights(moe_w_group[li], moe_b_group[li], moe_w_expert[li], moe_b_expert[li])
               for li in range(2)]

    def moe(li, xn, eid, we):
        return _moe_experts(xn, eid, we, moe_w13[li].astype(BF16), moe_w2[li].astype(BF16))

    h = jnp.concatenate([x_prompt.reshape(np_rows, d), x_sample.reshape(ns_rows, d)], axis=0)
    qa, ka, va, qr, kr, vr, gr = _in_proj(h, norm1[0], w_in_even[0].astype(BF16), seq, np_rows, dec_seq, past_len)
    oa_p = _moba_prompt(qa, ka, va, n_p, seq)
    or_p, ret_p = _ret_prompt(qr, kr, vr, gr, n_p, seq)
    oa_s = _moba_sample(qa, ka, va, cache_k[0].reshape(pool, page, A_W), cache_v[0].reshape(pool, page, A_W),
                        page_table, np_rows, dec_seq)
    or_s, ret_s = _ret_sample(qr, kr, vr, gr, state_ret[0], np_rows, dec_seq)
    o_a = jnp.concatenate([oa_p, oa_s], axis=0)
    o_r = jnp.concatenate([or_p, or_s], axis=0)
    w_out = w_out_even[0].astype(BF16)
    h, xn, eid, we = _proj_router([o_a, o_r], [w_out[:A_W], w_out[A_W:]], None, h, norm2[0], routers[0], False)
    h = _moe_combine(h, moe(0, xn, eid, we), norm2[0], False)

    sg = S5_SAMPLE_SEQS
    hp = h[:np_rows].reshape(n_p, seq, d).transpose(1, 0, 2).reshape(np_rows, d)
    hs = h[np_rows:].reshape(n_s // sg, sg, dec_seq, d).transpose(0, 2, 1, 3).reshape(ns_rows, d)
    s5w = _s5_weights(s5_lam_re[0], s5_lam_im[0], s5_log_dt[0], s5_b_re[0], s5_b_im[0], s5_c_re[0], s5_c_im[0])
    n_state = s5_lam_re.shape[1] * s5_lam_re.shape[2]
    zeros = jnp.zeros((n_p, n_state), F32)
    zg_p, s5r_p, s5i_p = _s5_scan(hp, norm1[1], s5w, s5_d[0], zeros, zeros, n_p, S5_PROMPT_STEPS)
    zg_s, s5r_s, s5i_s = _s5_scan(hs, norm1[1], s5w, s5_d[0], state_s5_re[0].reshape(n_s, n_state),
                                  state_s5_im[0].reshape(n_s, n_state), sg, dec_seq)
    h = jnp.concatenate([hp, hs], axis=0)
    zg = jnp.concatenate([zg_p, zg_s], axis=0)
    h, xn, eid, we = _proj_router([zg], [s5_w_glu[0].astype(BF16)], s5_b_glu[0], h, norm2[1], routers[1], True)
    y = _moe_combine(h, moe(1, xn, eid, we), norm_f, True)
    y_prompt = y[:np_rows].reshape(seq, n_p, d).transpose(1, 0, 2)
    y_sample = y[np_rows:].reshape(n_s // sg, dec_seq, sg, d).transpose(0, 2, 1, 3).reshape(n_s, dec_seq, d)

    kv_p = lambda a: a[:np_rows].reshape(1, n_p, seq // page, page, H_A, HD_A)
    kv_s = lambda a: a[np_rows:].reshape(1, n_s, dec_seq, H_A, HD_A)
    st = lambda a, n: a.reshape((1, n) + s5_lam_re.shape[1:])
    return (y_prompt, y_sample, kv_p(ka), kv_p(va), kv_s(ka), kv_s(va), ret_p[None], ret_s[None],
            st(s5r_p, n_p), st(s5i_p, n_p), st(s5r_s, n_s), st(s5i_s, n_s))
```

```python
import functools
import math

import jax
import jax.numpy as jnp
from jax import lax
from jax.experimental import pallas as pl
from jax.experimental.pallas import tpu as pltpu

F32 = jnp.float32
BF16 = jnp.bfloat16

H_A, HD_A = 8, 64
ROT_DIM = HD_A // 4
ROPE_THETA = 500000.0
MOBA_BLOCK = 256
MOBA_TOPK = 3
H_R, DK_R, DV_R = 8, 64, 128
RET_CHUNK = 128
S5_GROUP, S5_STATE = 16, 64
MOE_GROUPS, MOE_PER_GROUP, MOE_TOPK = 4, 8, 2
N_EXPERTS = MOE_GROUPS * MOE_PER_GROUP
A_W = H_A * HD_A
R_QK = H_R * DK_R
R_V = H_R * DV_R
NEG_INF = -1e30
EPS = 1e-6

LANES = 128
SUBLANES = 8
VMEM_LIMIT = 56 * 1024 * 1024

ROW_TILE = 256
MOE_ROWS = 256


def _nt(a, b):
    return lax.dot_general(a, b, (((1,), (1,)), ((), ())), preferred_element_type=F32)


def _tn(a, b):
    return lax.dot_general(a, b, (((0,), (0,)), ((), ())), preferred_element_type=F32)


def _mm(a, b):
    return jnp.dot(a, b, preferred_element_type=F32)


def _rms_rows(x, g):
    return x * lax.rsqrt(jnp.mean(x * x, axis=-1, keepdims=True) + EPS) * g


def _split_bf16(x):
    hi = x.astype(BF16)
    lo = (x - hi.astype(F32)).astype(BF16)
    return hi, lo


def _params(sem):
    return pltpu.CompilerParams(dimension_semantics=sem, vmem_limit_bytes=VMEM_LIMIT)


def _rotate_into(out_ref, z, c_ref, s_ref, shift, first, scale):
    for j in range(z.shape[1] // LANES):
        sl = slice(j * LANES, (j + 1) * LANES)
        zc = z[:, sl]
        up = pltpu.roll(zc, LANES - shift, axis=1)
        dn = pltpu.roll(zc, shift, axis=1)
        r = zc * c_ref[:, sl] + jnp.where(first, up, dn) * s_ref[:, sl]
        out_ref[:, sl] = r if scale is None else r * scale


def _in_proj_kernel(x_ref, g_ref, w_ref, ca_ref, sa_ref, cr_ref, sr_ref,
                    qa_ref, ka_ref, va_ref, qr_ref, kr_ref, vr_ref, gr_ref):
    xn = _rms_rows(x_ref[...], g_ref[...]).astype(BF16)
    lane = lax.broadcasted_iota(jnp.int32, (1, LANES), 1)
    first_a = (lane % HD_A) < (ROT_DIM // 2)
    first_r = (lane % 2) == 0

    def sec(lo, width):
        return _mm(xn, w_ref[:, lo:lo + width])

    _rotate_into(qa_ref, sec(0, A_W), ca_ref, sa_ref, ROT_DIM // 2, first_a, None)
    _rotate_into(ka_ref, sec(A_W, A_W), ca_ref, sa_ref, ROT_DIM // 2, first_a, None)
    va_ref[...] = sec(2 * A_W, A_W)
    _rotate_into(qr_ref, sec(3 * A_W, R_QK), cr_ref, sr_ref, 1, first_r, None)
    _rotate_into(kr_ref, sec(3 * A_W + R_QK, R_QK), cr_ref, sr_ref, 1, first_r, DK_R ** -0.5)
    vr_ref[...] = sec(3 * A_W + 2 * R_QK, R_V)
    gr_ref[...] = sec(3 * A_W + 2 * R_QK + R_V, R_V)


def _rope_tables(pos):
    half = ROT_DIM // 2
    inv = ROPE_THETA ** (-jnp.arange(half, dtype=F32) / half)
    ang = pos.astype(F32)[:, None] * inv[None, :]
    cos, sin = jnp.cos(ang), jnp.sin(ang)
    rest = HD_A - ROT_DIM
    c = jnp.concatenate([cos, cos, jnp.ones((pos.shape[0], rest), F32)], axis=-1)
    s = jnp.concatenate([-sin, sin, jnp.zeros((pos.shape[0], rest), F32)], axis=-1)
    return jnp.tile(c, (1, H_A)), jnp.tile(s, (1, H_A))


def _retnet_tables(pos):
    n = DK_R // 2
    inv = 1.0 / (10000.0 ** jnp.linspace(0.0, 1.0, n, dtype=F32))
    ang = pos.astype(F32)[:, None] * inv[None, :]
    cos, sin = jnp.cos(ang), jnp.sin(ang)
    c = jnp.repeat(cos, 2, axis=-1)
    s = jnp.stack([-sin, sin], axis=-1).reshape(pos.shape[0], DK_R)
    return jnp.tile(c, (1, H_R)), jnp.tile(s, (1, H_R))


def _in_proj(h, g, w_bf16, seq, n_prompt_rows, dec_seq, past_len):
    n, d = h.shape
    tm = ROW_TILE
    tiles_per_seq = seq // tm
    n_ptiles = n_prompt_rows // tm
    pos = jnp.concatenate([jnp.arange(seq), past_len + (jnp.arange(tm) % dec_seq)])
    ca, sa = _rope_tables(pos)
    cr, sr = _retnet_tables(pos)

    def tab_map(i):
        return (jnp.where(i < n_ptiles, i % tiles_per_seq, tiles_per_seq), 0)

    row = lambda w: pl.BlockSpec((tm, w), lambda i: (i, 0))
    tab = pl.BlockSpec((tm, A_W), tab_map)
    widths = (A_W, A_W, A_W, R_QK, R_QK, R_V, R_V)
    return pl.pallas_call(
        _in_proj_kernel,
        grid=(n // tm,),
        in_specs=[row(d), pl.BlockSpec((1, d), lambda i: (0, 0)),
                  pl.BlockSpec(w_bf16.shape, lambda i: (0, 0)), tab, tab, tab, tab],
        out_specs=[row(w) for w in widths],
        out_shape=[jax.ShapeDtypeStruct((n, w), F32) for w in widths],
        compiler_params=_params(("arbitrary",)),
        name="in_proj",
    )(h, g.reshape(1, d), w_bf16, ca, sa, cr, sr)


def _moba_select(q_f32, kmean, n_valid):
    n_blk = kmean.shape[0]
    qh, ql = _split_bf16(q_f32)
    kh, kl = _split_bf16(kmean)
    st = _nt(jnp.concatenate([kh, kl, kh], axis=1), jnp.concatenate([qh, qh, ql], axis=1))
    jrow = lax.broadcasted_iota(jnp.int32, st.shape, 0)
    rank = jnp.zeros(st.shape, F32)
    for jp in range(n_blk):
        sj = st[jp:jp + 1, :]
        beats = (sj > st) | ((sj == st) & (jp < jrow))
        rank = rank + jnp.where(beats & (jp < n_valid), 1.0, 0.0)
    sel_t = jnp.where((jrow < n_valid) & (rank < MOBA_TOPK), 1.0, 0.0).astype(BF16)
    rows = q_f32.shape[0]
    eye = (lax.broadcasted_iota(jnp.int32, (rows, rows), 0)
           == lax.broadcasted_iota(jnp.int32, (rows, rows), 1))
    return _nt(jnp.where(eye, 1.0, 0.0).astype(BF16), sel_t)


def _softmax_step(carry, s, v_bf16):
    m, l, acc = carry
    m_new = jnp.maximum(m, jnp.max(s, axis=1, keepdims=True))
    alpha = jnp.exp(m - m_new)
    p = jnp.exp(s - m_new)
    l = alpha * l + jnp.sum(p, axis=1, keepdims=True)
    acc = alpha * acc + _mm(p.astype(BF16), v_bf16)
    return m_new, l, acc


def _moba_prompt_kernel(q_ref, k_ref, v_ref, o_ref, kb_ref, vb_ref, km_ref, sel_ref):
    b = pl.program_id(2)
    blk = MOBA_BLOCK
    n_blk = k_ref.shape[0] // blk

    @pl.when(b == 0)
    def _():
        kb_ref[...] = k_ref[...].astype(BF16)
        vb_ref[...] = v_ref[...].astype(BF16)
        for j in range(n_blk):
            km_ref[j:j + 1, :] = jnp.mean(k_ref[j * blk:(j + 1) * blk, :], axis=0, keepdims=True)

    q = q_ref[...]
    lane = lax.broadcasted_iota(jnp.int32, (1, LANES), 1)
    causal = (lax.broadcasted_iota(jnp.int32, (blk, blk), 1)
              <= lax.broadcasted_iota(jnp.int32, (blk, blk), 0))
    own = pl.ds(pl.multiple_of(b * blk, blk), blk)
    heads = range(LANES // HD_A)
    hms, qss, carries = [], [], []
    for hh in heads:
        hm = (lane // HD_A) == hh
        qh = jnp.where(hm, q, 0.0)
        sel = _moba_select(qh, km_ref[...], b)
        for j in range(n_blk):
            sel_ref[hh, j] = sel[:, j:j + 1]
        qs = (qh * (HD_A ** -0.5)).astype(BF16)
        s0 = jnp.where(causal, _nt(qs, kb_ref[own, :]), NEG_INF)
        init = (jnp.full((blk, 1), NEG_INF, F32), jnp.zeros((blk, 1), F32), jnp.zeros(q.shape, F32))
        hms.append(hm)
        qss.append(qs)
        carries.append(_softmax_step(init, s0, vb_ref[own, :]))

    def past(j, carries):
        rows = pl.ds(pl.multiple_of(j * blk, blk), blk)
        kj, vj = kb_ref[rows, :], vb_ref[rows, :]
        return tuple(
            _softmax_step(carries[hh], jnp.where(sel_ref[hh, j] > 0.5, _nt(qss[hh], kj), NEG_INF), vj)
            for hh in heads)

    carries = lax.fori_loop(0, b, past, tuple(carries))
    out = jnp.zeros(q.shape, F32)
    for hh in heads:
        _, l, acc = carries[hh]
        out = out + jnp.where(hms[hh], acc / l, 0.0)
    o_ref[...] = out


def _moba_prompt(q_a, k_a, v_a, n_seq, seq):
    blk = MOBA_BLOCK
    n_blk = seq // blk
    qspec = pl.BlockSpec((blk, LANES), lambda s, h, b: (s * n_blk + b, h))
    kspec = pl.BlockSpec((seq, LANES), lambda s, h, b: (s, h))
    return pl.pallas_call(
        _moba_prompt_kernel,
        grid=(n_seq, A_W // LANES, n_blk),
        in_specs=[qspec, kspec, kspec],
        out_specs=qspec,
        out_shape=jax.ShapeDtypeStruct((n_seq * seq, A_W), F32),
        scratch_shapes=[pltpu.VMEM((seq, LANES), BF16), pltpu.VMEM((seq, LANES), BF16),
                        pltpu.VMEM((n_blk, LANES), F32), pltpu.VMEM((LANES // HD_A, n_blk, blk, 1), F32)],
        compiler_params=_params(("arbitrary", "arbitrary", "arbitrary")),
        name="moba_prompt",
    )(q_a, k_a, v_a)


def _ret_decay_tables(c):
    log_g = jnp.log(1.0 - 2.0 ** (-5.0 - jnp.arange(H_R, dtype=F32)))
    i = jnp.arange(c, dtype=F32)
    diff = i[:, None] - i[None, :]
    dmat = jnp.where(diff >= 0, jnp.exp(jnp.maximum(diff, 0.0)[None] * log_g[:, None, None]), 0.0)
    dq = jnp.exp((i + 1.0)[None, :] * log_g[:, None])
    dk = jnp.exp((c - 1.0 - i)[None, :] * log_g[:, None])
    dc = jnp.exp(c * log_g)
    return dmat, dq, dk, dc


def _ret_chunk(q, k, v, g, s, dmat, dq, dk, dc):
    att = _nt(q.astype(BF16), k.astype(BF16)) * dmat
    o = _mm(att.astype(BF16), v.astype(BF16)) + _mm((q * dq).astype(BF16), s.astype(BF16))
    s = s * dc + _tn((k * dk).astype(BF16), v.astype(BF16))
    o = o * lax.rsqrt(jnp.mean(o * o, axis=-1, keepdims=True) + EPS)
    return o * (g * jax.nn.sigmoid(g)), s


def _ret_prompt_kernel(q_ref, k_ref, v_ref, g_ref, dmat_ref, dq_ref, dk_ref, dc_ref, o_ref, st_ref):
    c = RET_CHUNK
    hh = pl.program_id(1) % (LANES // DK_R)
    lane = lax.broadcasted_iota(jnp.int32, (1, LANES), 1)
    hm = (lane // DK_R) == hh
    dmat, dq, dk, dc = dmat_ref[...], dq_ref[...], dk_ref[...], dc_ref[...]

    def chunk(i, s):
        rows = pl.ds(pl.multiple_of(i * c, c), c)
        q = jnp.where(hm, q_ref[rows, :], 0.0)
        k = jnp.where(hm, k_ref[rows, :], 0.0)
        o, s = _ret_chunk(q, k, v_ref[rows, :], g_ref[rows, :], s, dmat, dq, dk, dc)
        o_ref[rows, :] = o
        return s

    s = lax.fori_loop(0, q_ref.shape[0] // c, chunk, jnp.zeros((LANES, DV_R), F32))
    st_ref[...] = jnp.where(hh == 0, s[:DK_R, :], s[DK_R:, :])


def _ret_tables_bcast(c, rows):
    dmat, dq, dk, dc = _ret_decay_tables(c)
    pad = rows - c
    dmat = jnp.pad(dmat, ((0, 0), (0, pad), (0, pad)))
    dq = jnp.broadcast_to(jnp.pad(dq, ((0, 0), (0, pad)))[:, :, None], (H_R, rows, LANES))
    dk = jnp.broadcast_to(jnp.pad(dk, ((0, 0), (0, pad)))[:, :, None], (H_R, rows, LANES))
    dc = jnp.broadcast_to(dc[:, None, None], (H_R, 1, LANES))
    return dmat, dq, dk, dc


def _ret_prompt(q_r, k_r, v_r, g_r, n_seq, seq):
    c = RET_CHUNK
    dmat, dq, dk, dc = _ret_tables_bcast(c, c)
    qk = pl.BlockSpec((seq, LANES), lambda s, h: (s, h // (LANES // DK_R)))
    vg = pl.BlockSpec((seq, DV_R), lambda s, h: (s, h))
    tab = lambda r: pl.BlockSpec((None, r, LANES), lambda s, h: (h, 0, 0))
    return pl.pallas_call(
        _ret_prompt_kernel,
        grid=(n_seq, H_R),
        in_specs=[qk, qk, vg, vg, tab(c), tab(c), tab(c), tab(1)],
        out_specs=[vg, pl.BlockSpec((None, None, DK_R, DV_R), lambda s, h: (s, h, 0, 0))],
        out_shape=[jax.ShapeDtypeStruct((n_seq * seq, R_V), F32),
                   jax.ShapeDtypeStruct((n_seq, H_R, DK_R, DV_R), F32)],
        compiler_params=_params(("arbitrary", "arbitrary")),
        name="ret_prompt",
    )(q_r, k_r, v_r, g_r, dmat, dq, dk, dc)


def _pad_rows(x, rows):
    return jnp.concatenate([x, jnp.zeros((rows - x.shape[0], x.shape[1]), x.dtype)], axis=0)


def _moba_sample_kernel(pt_ref, q_ref, kn_ref, vn_ref, *rest, n_pages):
    k_refs, v_refs = rest[:n_pages], rest[n_pages:2 * n_pages]
    o_ref, s_ref = rest[2 * n_pages], rest[2 * n_pages + 1]
    n_h, t, hd = q_ref.shape
    rows = n_h * t
    page = k_refs[0].shape[0]
    cols = page * n_h
    per_blk = MOBA_BLOCK // page
    n_blk = n_pages // per_blk
    qs = (q_ref[...].reshape(rows, hd) * (HD_A ** -0.5)).astype(BF16)
    head_ok = (lax.broadcasted_iota(jnp.int32, (rows, cols), 0) // t
               == lax.broadcasted_iota(jnp.int32, (rows, cols), 1) % n_h)

    bsum = [jnp.zeros((rows, 1), F32) for _ in range(n_blk)]
    for p in range(n_pages):
        sp = _nt(qs, k_refs[p][...].reshape(cols, hd).astype(BF16))
        s_ref[:, p * cols:(p + 1) * cols] = jnp.where(head_ok, sp, NEG_INF)
        bsum[p // per_blk] = bsum[p // per_blk] + jnp.sum(jnp.where(head_ok, sp, 0.0), axis=1, keepdims=True)
    sel = []
    for j in range(n_blk):
        rank = jnp.zeros((rows, 1), F32)
        for jp in range(n_blk):
            if jp != j:
                beats = (bsum[jp] > bsum[j]) | (bsum[jp] == bsum[j]) if jp < j else bsum[jp] > bsum[j]
                rank = rank + jnp.where(beats, 1.0, 0.0)
        sel.append(rank < MOBA_TOPK)

    own = t * n_h
    r_i = lax.broadcasted_iota(jnp.int32, (rows, own), 0)
    c_i = lax.broadcasted_iota(jnp.int32, (rows, own), 1)
    own_ok = ((r_i // t) == (c_i % n_h)) & ((c_i // n_h) <= (r_i % t))
    s0 = jnp.where(own_ok, _nt(qs, kn_ref[...].reshape(own, hd).astype(BF16)), NEG_INF)
    init = (jnp.full((rows, 1), NEG_INF, F32), jnp.zeros((rows, 1), F32), jnp.zeros((rows, hd), F32))
    carry = _softmax_step(init, s0, vn_ref[...].reshape(own, hd).astype(BF16))
    for p in range(n_pages):
        s = jnp.where(sel[p // per_blk], s_ref[:, p * cols:(p + 1) * cols], NEG_INF)
        carry = _softmax_step(carry, s, v_refs[p][...].reshape(cols, hd).astype(BF16))
    _, l, acc = carry
    o_ref[...] = (acc / l).reshape(n_h, t, hd)


def _moba_sample(q_s, k_s, v_s, cache_k, cache_v, page_table):
    n_s, n_pages = page_table.shape
    page = cache_k.shape[1]
    t = k_s.shape[1]
    one = lambda a: pl.BlockSpec((None,) + a.shape[1:], lambda n, pt: (n, 0, 0, 0))
    pages = [pl.BlockSpec((None,) + cache_k.shape[1:], lambda n, pt, p=p: (pt[n * n_pages + p], 0, 0, 0))
             for p in range(n_pages)]
    grid_spec = pltpu.PrefetchScalarGridSpec(
        num_scalar_prefetch=1,
        grid=(n_s,),
        in_specs=[one(q_s), one(k_s), one(v_s)] + pages + pages,
        out_specs=one(q_s),
        scratch_shapes=[pltpu.VMEM((H_A * t, n_pages * page * H_A), F32)],
    )
    return pl.pallas_call(
        functools.partial(_moba_sample_kernel, n_pages=n_pages),
        grid_spec=grid_spec,
        out_shape=jax.ShapeDtypeStruct(q_s.shape, F32),
        compiler_params=_params(("arbitrary",)),
        name="moba_sample",
    )(page_table.reshape(-1), q_s, k_s, v_s, *([cache_k] * n_pages), *([cache_v] * n_pages))


def _ret_sample_kernel(q_ref, k_ref, v_ref, g_ref, s0_ref, dmat_ref, dq_ref, dk_ref, dc_ref, o_ref, st_ref):
    t = q_ref.shape[0]
    lane = lax.broadcasted_iota(jnp.int32, (1, LANES), 1)
    per = LANES // DK_R
    zero_half = jnp.zeros((DK_R, DV_R), F32)
    for h in range(H_R):
        hh = h % per
        hm = (lane // DK_R) == hh
        qk_l = slice((h // per) * LANES, (h // per + 1) * LANES)
        v_l = slice(h * DV_R, (h + 1) * DV_R)
        q = _pad_rows(jnp.where(hm, q_ref[:, qk_l], 0.0), LANES)
        k = _pad_rows(jnp.where(hm, k_ref[:, qk_l], 0.0), LANES)
        v = _pad_rows(v_ref[:, v_l], LANES)
        g = _pad_rows(g_ref[:, v_l], LANES)
        halves = [zero_half] * per
        halves[hh] = s0_ref[h]
        o, s = _ret_chunk(q, k, v, g, jnp.concatenate(halves, axis=0),
                          dmat_ref[h], dq_ref[h], dk_ref[h], dc_ref[h])
        o_ref[:, v_l] = o[:t, :]
        st_ref[h] = s[hh * DK_R:(hh + 1) * DK_R, :]


def _ret_sample(q_r, k_r, v_r, g_r, s0, n_prompt_rows, dec_seq):
    n_s = s0.shape[0]
    base = n_prompt_rows // dec_seq
    dmat, dq, dk, dc = _ret_tables_bcast(dec_seq, LANES)
    qk = pl.BlockSpec((dec_seq, R_QK), lambda n: (base + n, 0))
    vg = pl.BlockSpec((dec_seq, R_V), lambda n: (base + n, 0))
    st = pl.BlockSpec((None, H_R, DK_R, DV_R), lambda n: (n, 0, 0, 0))
    full = lambda a: pl.BlockSpec(a.shape, lambda n: (0,) * a.ndim)
    return pl.pallas_call(
        _ret_sample_kernel,
        grid=(n_s,),
        in_specs=[qk, qk, vg, vg, st, full(dmat), full(dq), full(dk), full(dc)],
        out_specs=[pl.BlockSpec((dec_seq, R_V), lambda n: (n, 0)), st],
        out_shape=[jax.ShapeDtypeStruct((n_s * dec_seq, R_V), F32), jax.ShapeDtypeStruct(s0.shape, F32)],
        compiler_params=_params(("arbitrary",)),
        name="ret_sample",
    )(q_r, k_r, v_r, g_r, s0, dmat, dq, dk, dc)


ROUTER_ROWS = SUBLANES + N_EXPERTS


def _route(xn, wr_ref, br_ref, eid_ref, we_ref):
    xh, xl = _split_bf16(xn)
    lt = _nt(wr_ref[...], jnp.concatenate([xh, xh, xl], axis=1)) + br_ref[...]
    tm = lt.shape[1]
    r8 = lax.broadcasted_iota(jnp.int32, (SUBLANES, tm), 0)
    lg = jnp.where(r8 < MOE_GROUPS, lt[:SUBLANES, :], NEG_INF)
    mg = jnp.max(lg, axis=0, keepdims=True)
    wg = 1.0 / jnp.sum(jnp.exp(lg - mg), axis=0, keepdims=True)
    gidx = jnp.min(jnp.where(lg == mg, r8, SUBLANES), axis=0, keepdims=True)
    le = jnp.zeros((MOE_PER_GROUP, tm), F32)
    for gi in range(MOE_GROUPS):
        lo = SUBLANES + gi * MOE_PER_GROUP
        le = le + jnp.where(gidx == gi, lt[lo:lo + MOE_PER_GROUP, :], 0.0)
    v1 = jnp.max(le, axis=0, keepdims=True)
    i1 = jnp.min(jnp.where(le == v1, r8, MOE_PER_GROUP), axis=0, keepdims=True)
    le2 = jnp.where(r8 == i1, -jnp.inf, le)
    v2 = jnp.max(le2, axis=0, keepdims=True)
    i2 = jnp.min(jnp.where(le2 == v2, r8, MOE_PER_GROUP), axis=0, keepdims=True)
    e21 = jnp.exp(v2 - v1)
    w1 = wg / (1.0 + e21)
    eid_ref[0:1, :] = gidx * MOE_PER_GROUP + i1
    eid_ref[1:2, :] = gidx * MOE_PER_GROUP + i2
    we_ref[0:1, :] = w1
    we_ref[1:2, :] = w1 * e21


def _proj_router_kernel(*refs, n_in, glu):
    a_refs, w_refs = refs[:n_in], refs[n_in:2 * n_in]
    k = 2 * n_in
    b_ref = refs[k] if glu else None
    k += int(glu)
    h_ref, g_ref, wr_ref, br_ref, ho_ref, xn_ref, eid_ref, we_ref = refs[k:k + 8]
    acc = None
    for a_ref, w_ref in zip(a_refs, w_refs):
        part = _mm(a_ref[...].astype(BF16), w_ref[...])
        acc = part if acc is None else acc + part
    if glu:
        acc = acc + b_ref[...]
        half = acc.shape[1] // 2
        acc = acc[:, :half] * jax.nn.sigmoid(acc[:, half:])
    hn = h_ref[...] + acc
    ho_ref[...] = hn
    xn = _rms_rows(hn, g_ref[...])
    xn_ref[...] = xn
    _route(xn, wr_ref, br_ref, eid_ref, we_ref)


def _router_weights(w_group, b_group, w_expert, b_expert):
    d = w_group.shape[0]
    wt = jnp.concatenate([w_group.T, jnp.zeros((SUBLANES - MOE_GROUPS, d), F32), w_expert.T], axis=0)
    hi = wt.astype(BF16)
    lo = (wt - hi.astype(F32)).astype(BF16)
    bias = jnp.concatenate([b_group, jnp.zeros((SUBLANES - MOE_GROUPS,), F32), b_expert]).reshape(-1, 1)
    return jnp.concatenate([hi, lo, hi], axis=1), bias


def _proj_router(acts, weights, bias, h, g, router, glu):
    n, d = h.shape
    tm = ROW_TILE
    wr, br = router
    row = lambda w: pl.BlockSpec((tm, w), lambda i: (i, 0))
    full = lambda a: pl.BlockSpec(a.shape, lambda i: (0,) * a.ndim)
    tok = pl.BlockSpec((MOE_TOPK, tm), lambda i: (0, i))
    operands = list(acts) + list(weights) + ([bias.reshape(1, -1)] if glu else []) + [h, g.reshape(1, d), wr, br]
    in_specs = ([row(a.shape[1]) for a in acts] + [full(w) for w in weights]
                + ([pl.BlockSpec((1, bias.shape[0]), lambda i: (0, 0))] if glu else [])
                + [row(d), pl.BlockSpec((1, d), lambda i: (0, 0)), full(wr), full(br)])
    return pl.pallas_call(
        functools.partial(_proj_router_kernel, n_in=len(acts), glu=glu),
        grid=(n // tm,),
        in_specs=in_specs,
        out_specs=[row(d), row(d), tok, tok],
        out_shape=[jax.ShapeDtypeStruct((n, d), F32), jax.ShapeDtypeStruct((n, d), F32),
                   jax.ShapeDtypeStruct((MOE_TOPK, n), jnp.int32), jax.ShapeDtypeStruct((MOE_TOPK, n), F32)],
        compiler_params=_params(("arbitrary",)),
        name="glu_router" if glu else "out_proj_router",
    )(*operands)


def _moe_plan(eid, we):
    n_tok = eid.shape[1]
    n_assign = MOE_TOPK * n_tok
    blk = MOE_ROWS
    n_blocks = -(-n_assign // blk) + N_EXPERTS
    e_flat = eid.reshape(-1)
    experts = jnp.arange(N_EXPERTS, dtype=jnp.int32)
    counts = jnp.sum((e_flat[:, None] == experts[None, :]).astype(jnp.int32), axis=0)
    order = jnp.argsort(e_flat).astype(jnp.int32)
    pc = (counts + blk - 1) // blk * blk
    pend = jnp.cumsum(pc)
    pstart = pend - pc
    start = jnp.cumsum(counts) - counts
    first_row = jnp.arange(n_blocks, dtype=jnp.int32) * blk
    blk_e = jnp.minimum(jnp.sum((pend[None, :] <= first_row[:, None]).astype(jnp.int32), axis=1), N_EXPERTS - 1)
    n_valid = jnp.clip(counts[blk_e] - (first_row - pstart[blk_e]), 0, blk).astype(jnp.int32)
    row = first_row[:, None] + jnp.arange(blk, dtype=jnp.int32)[None, :]
    off = row - pstart[blk_e][:, None]
    valid = off < counts[blk_e][:, None]
    a_row = order[jnp.clip(start[blk_e][:, None] + off, 0, n_assign - 1)]
    a_row = jnp.where(valid, a_row, 0).reshape(-1)
    w_row = jnp.where(valid.reshape(-1), we.reshape(-1)[a_row], 0.0)
    n_used = (pend[-1] // blk).astype(jnp.int32).reshape(1)
    return blk_e, n_used, n_valid, a_row % n_tok, a_row, w_row.reshape(-1, 1), n_blocks


def _moe_kernel(blk_e_ref, n_used_ref, n_valid_ref, src_ref, dst_ref, x_hbm, w13_ref, w2_ref, wrow_ref, y_hbm,
                xbuf, ybuf, gsem, ssem):
    b = pl.program_id(0)
    n_used = n_used_ref[0]
    blk = MOE_ROWS
    slot = b % 2

    def gather_row(block, slot, r):
        return pltpu.make_async_copy(x_hbm.at[pl.ds(src_ref[block * blk + r], 1), :],
                                     xbuf.at[slot, pl.ds(r, 1), :], gsem.at[slot])

    def scatter_row(block, slot, r):
        return pltpu.make_async_copy(ybuf.at[slot, pl.ds(r, 1), :],
                                     y_hbm.at[pl.ds(dst_ref[block * blk + r], 1), :], ssem.at[slot])

    def gather_all(slot):
        return pltpu.make_async_copy(x_hbm.at[pl.ds(0, blk), :], xbuf.at[slot], gsem.at[slot])

    def scatter_all(slot):
        return pltpu.make_async_copy(ybuf.at[slot], y_hbm.at[pl.ds(0, blk), :], ssem.at[slot])

    def start_rows(row_copy, block, slot):
        nv = n_valid_ref[block]

        def body(r, c):
            row_copy(block, slot, r).start()
            return c

        @pl.when(nv == blk)
        def _():
            lax.fori_loop(0, blk, body, 0, unroll=8)

        @pl.when(nv < blk)
        def _():
            lax.fori_loop(0, nv, body, 0)

    def wait_rows(row_copy, all_copy, block, slot):
        nv = n_valid_ref[block]

        @pl.when(nv == blk)
        def _():
            all_copy(slot).wait()

        @pl.when(nv < blk)
        def _():
            def body(r, c):
                row_copy(block, slot, r).wait()
                return c
            lax.fori_loop(0, nv, body, 0)

    @pl.when((b == 0) & (n_used > 0))
    def _():
        xbuf[...] = jnp.zeros(xbuf.shape, xbuf.dtype)
        start_rows(gather_row, 0, 0)

    @pl.when(b + 1 < n_used)
    def _():
        start_rows(gather_row, b + 1, 1 - slot)

    @pl.when(b < n_used)
    def _():
        wait_rows(gather_row, gather_all, b, slot)

        @pl.when(b >= 2)
        def _():
            wait_rows(scatter_row, scatter_all, b - 2, slot)

        hb = _mm(xbuf[slot].astype(BF16), w13_ref[...])
        half = hb.shape[1] // 2
        gate = hb[:, :half]
        act = gate * jax.nn.sigmoid(gate) * hb[:, half:]
        ybuf[slot] = _mm(act.astype(BF16), w2_ref[...]) * wrow_ref[...]
        start_rows(scatter_row, b, slot)

    @pl.when(b == pl.num_programs(0) - 1)
    def _():
        for back in (2, 1):
            last = n_used - back

            @pl.when(last >= 0)
            def _():
                wait_rows(scatter_row, scatter_all, last, last % 2)


def _moe_experts(xn, eid, we, w13_bf16, w2_bf16):
    n_tok, d = xn.shape
    blk = MOE_ROWS
    blk_e, n_used, n_valid, src_tok, dst_row, w_row, n_blocks = _moe_plan(eid, we)
    grid_spec = pltpu.PrefetchScalarGridSpec(
        num_scalar_prefetch=5,
        grid=(n_blocks,),
        in_specs=[pl.BlockSpec(memory_space=pl.ANY),
                  pl.BlockSpec((None,) + w13_bf16.shape[1:], lambda b, be, *_: (be[b], 0, 0)),
                  pl.BlockSpec((None,) + w2_bf16.shape[1:], lambda b, be, *_: (be[b], 0, 0)),
                  pl.BlockSpec((blk, 1), lambda b, *_: (b, 0))],
        out_specs=pl.BlockSpec(memory_space=pl.ANY),
        scratch_shapes=[pltpu.VMEM((2, blk, d), F32), pltpu.VMEM((2, blk, d), F32),
                        pltpu.SemaphoreType.DMA((2,)), pltpu.SemaphoreType.DMA((2,))],
    )
    return pl.pallas_call(
        _moe_kernel,
        grid_spec=grid_spec,
        out_shape=jax.ShapeDtypeStruct((MOE_TOPK * n_tok, d), F32),
        compiler_params=_params(("arbitrary",)),
        name="moe_experts",
    )(blk_e, n_used, n_valid, src_tok, dst_row, xn, w13_bf16, w2_bf16, w_row)


def _combine_kernel(h_ref, y0_ref, y1_ref, g_ref, *out_refs, norm):
    h = h_ref[...] + (y0_ref[...] + y1_ref[...])
    out_refs[0][...] = _rms_rows(h, g_ref[...]) if norm else h


def _moe_combine(h, y, g, norm):
    n, d = h.shape
    tm = ROW_TILE
    row = pl.BlockSpec((tm, d), lambda i: (i, 0))
    return pl.pallas_call(
        functools.partial(_combine_kernel, norm=norm),
        grid=(n // tm,),
        in_specs=[row, row, pl.BlockSpec((tm, d), lambda i: (n // tm + i, 0)),
                  pl.BlockSpec((1, d), lambda i: (0, 0))],
        out_specs=row,
        out_shape=jax.ShapeDtypeStruct((n, d), F32),
        compiler_params=_params(("arbitrary",)),
        name="moe_combine_norm" if norm else "moe_combine",
    )(h, y, y, g.reshape(1, d))


S5_CH = LANES
S5_ST = (LANES // S5_GROUP) * S5_STATE
S5_SCAN_LANES = 512


def _s5_kernel(h_ref, g_ref, bre_ref, bim_ref, are_ref, aim_ref, cre_ref, cim_ref, d_ref, s0r_ref, s0i_ref,
               zg_ref, fr_ref, fi_ref, xr, xi, sr, si, *, s, steps):
    c = pl.program_id(1)

    @pl.when(c == 0)
    def _():
        sr[...] = s0r_ref[...]
        si[...] = s0i_ref[...]

    u = _rms_rows(h_ref[...], g_ref[...])
    ub = u.astype(BF16)
    n_ch = u.shape[1] // S5_CH
    for j in range(n_ch):
        uj = ub[:, j * S5_CH:(j + 1) * S5_CH]
        xr[:, j * S5_ST:(j + 1) * S5_ST] = _mm(uj, bre_ref[j])
        xi[:, j * S5_ST:(j + 1) * S5_ST] = _mm(uj, bim_ref[j])

    for lo in range(0, xr.shape[1], S5_SCAN_LANES):
        ls = slice(lo, lo + S5_SCAN_LANES)
        ar = jnp.broadcast_to(are_ref[:, ls], (s, S5_SCAN_LANES))
        ai = jnp.broadcast_to(aim_ref[:, ls], (s, S5_SCAN_LANES))

        def step(t, carry):
            pr, pi = carry
            rows = pl.ds(pl.multiple_of(t * s, s), s)
            nr = ar * pr - ai * pi + xr[rows, ls]
            ni = ar * pi + ai * pr + xi[rows, ls]
            xr[rows, ls] = nr
            xi[rows, ls] = ni
            return nr, ni

        pr, pi = lax.fori_loop(0, steps, step, (sr[:, ls], si[:, ls]))
        sr[:, ls] = pr
        si[:, ls] = pi

    for j in range(n_ch):
        cs = slice(j * S5_CH, (j + 1) * S5_CH)
        ss = slice(j * S5_ST, (j + 1) * S5_ST)
        y = _mm(xr[:, ss].astype(BF16), cre_ref[j]) - _mm(xi[:, ss].astype(BF16), cim_ref[j])
        zg_ref[:, cs] = jax.nn.gelu(y + d_ref[:, cs] * u[:, cs])

    @pl.when(c == pl.num_programs(1) - 1)
    def _():
        fr_ref[...] = sr[...]
        fi_ref[...] = si[...]


def _s5_weights(lam_re, lam_im, log_dt, b_re, b_im, c_re, c_im):
    dt = jnp.exp(log_dt)[:, None]
    mag = jnp.exp(lam_re * dt)
    ang = lam_im * dt
    ab_re = mag * jnp.cos(ang)
    ab_im = mag * jnp.sin(ang)
    den = lam_re * lam_re + lam_im * lam_im
    nr = ab_re - 1.0
    co_re = (nr * lam_re + ab_im * lam_im) / den
    co_im = (ab_im * lam_re - nr * lam_im) / den
    bb_re = co_re[..., None] * b_re - co_im[..., None] * b_im
    bb_im = co_re[..., None] * b_im + co_im[..., None] * b_re
    per = S5_CH // S5_GROUP
    n_ch = lam_re.shape[0] // per
    eye = jnp.eye(per, dtype=F32)

    def in_blocks(bb):
        w = bb.transpose(0, 2, 1).reshape(n_ch, per, S5_GROUP, S5_STATE)
        return jnp.einsum('jgcp,gh->jgchp', w, eye).reshape(n_ch, S5_CH, S5_ST).astype(BF16)

    def out_blocks(cc):
        w = cc.transpose(0, 2, 1).reshape(n_ch, per, S5_STATE, S5_GROUP)
        return jnp.einsum('jgpc,gh->jgphc', w, eye).reshape(n_ch, S5_ST, S5_CH).astype(BF16)

    return (in_blocks(bb_re), in_blocks(bb_im), ab_re.reshape(1, -1), ab_im.reshape(1, -1),
            out_blocks(c_re), out_blocks(c_im))


def _s5_scan(h, g, weights, d_skip, s0_re, s0_im, s, steps):
    n, d = h.shape
    n_groups = s0_re.shape[0] // s
    rows = steps * s
    chunks = n // (n_groups * rows)
    bre, bim, are, aim, cre, cim = weights
    n_state = are.shape[1]
    full = lambda a: pl.BlockSpec(a.shape, lambda gi, c: (0,) * a.ndim)
    row = pl.BlockSpec((rows, d), lambda gi, c: (gi * chunks + c, 0))
    state = pl.BlockSpec((s, n_state), lambda gi, c: (gi, 0))
    return pl.pallas_call(
        functools.partial(_s5_kernel, s=s, steps=steps),
        grid=(n_groups, chunks),
        in_specs=[row, pl.BlockSpec((1, d), lambda gi, c: (0, 0)), full(bre), full(bim), full(are), full(aim),
                  full(cre), full(cim), pl.BlockSpec((1, d), lambda gi, c: (0, 0)), state, state],
        out_specs=[row, state, state],
        out_shape=[jax.ShapeDtypeStruct((n, d), F32), jax.ShapeDtypeStruct(s0_re.shape, F32),
                   jax.ShapeDtypeStruct(s0_re.shape, F32)],
        scratch_shapes=[pltpu.VMEM((rows, n_state), F32), pltpu.VMEM((rows, n_state), F32),
                        pltpu.VMEM((s, n_state), F32), pltpu.VMEM((s, n_state), F32)],
        compiler_params=_params(("arbitrary", "arbitrary")),
        name="s5_scan",
    )(h, g.reshape(1, d), bre, bim, are, aim, cre, cim, d_skip.reshape(1, d), s0_re, s0_im)


S5_PROMPT_STEPS = 32
S5_SAMPLE_SEQS = 32


def kernel(x_prompt, x_sample, cache_k, cache_v, state_ret, state_s5_re, state_s5_im, page_table, norm1, norm2, norm_f, w_in_even, w_out_even, s5_lam_re, s5_lam_im, s5_log_dt, s5_b_re, s5_b_im, s5_c_re, s5_c_im, s5_d, s5_w_glu, s5_b_glu, moe_w_group, moe_b_group, moe_w_expert, moe_b_expert, moe_w13, moe_w2):
    n_p, seq, d = x_prompt.shape
    n_s, dec_seq, _ = x_sample.shape
    pool, page = cache_k.shape[1], cache_k.shape[2]
    past_len = page_table.shape[1] * page
    np_rows, ns_rows = n_p * seq, n_s * dec_seq
    routers = [_router_weights(moe_w_group[li], moe_b_group[li], moe_w_expert[li], moe_b_expert[li])
               for li in range(2)]

    def moe(li, xn, eid, we):
        return _moe_experts(xn, eid, we, moe_w13[li].astype(BF16), moe_w2[li].astype(BF16))

    h = jnp.concatenate([x_prompt.reshape(np_rows, d), x_sample.reshape(ns_rows, d)], axis=0)
    qa, ka, va, qr, kr, vr, gr = _in_proj(h, norm1[0], w_in_even[0].astype(BF16), seq, np_rows, dec_seq, past_len)
    oa_p = _moba_prompt(qa, ka, va, n_p, seq)
    or_p, ret_p = _ret_prompt(qr, kr, vr, gr, n_p, seq)
    heads = lambda a: a[np_rows:].reshape(n_s, dec_seq, H_A, HD_A)
    k_s, v_s = heads(ka), heads(va)
    oa_s = _moba_sample(heads(qa).transpose(0, 2, 1, 3), k_s, v_s, cache_k[0], cache_v[0], page_table)
    oa_s = oa_s.transpose(0, 2, 1, 3).reshape(ns_rows, A_W)
    or_s, ret_s = _ret_sample(qr, kr, vr, gr, state_ret[0], np_rows, dec_seq)
    o_a = jnp.concatenate([oa_p, oa_s], axis=0)
    o_r = jnp.concatenate([or_p, or_s], axis=0)
    w_out = w_out_even[0].astype(BF16)
    h, xn, eid, we = _proj_router([o_a, o_r], [w_out[:A_W], w_out[A_W:]], None, h, norm2[0], routers[0], False)
    h = _moe_combine(h, moe(0, xn, eid, we), norm2[0], False)

    sg = S5_SAMPLE_SEQS
    hp = h[:np_rows].reshape(n_p, seq, d).transpose(1, 0, 2).reshape(np_rows, d)
    hs = h[np_rows:].reshape(n_s // sg, sg, dec_seq, d).transpose(0, 2, 1, 3).reshape(ns_rows, d)
    s5w = _s5_weights(s5_lam_re[0], s5_lam_im[0], s5_log_dt[0], s5_b_re[0], s5_b_im[0], s5_c_re[0], s5_c_im[0])
    n_state = s5_lam_re.shape[1] * s5_lam_re.shape[2]
    zeros = jnp.zeros((n_p, n_state), F32)
    zg_p, s5r_p, s5i_p = _s5_scan(hp, norm1[1], s5w, s5_d[0], zeros, zeros, n_p, S5_PROMPT_STEPS)
    zg_s, s5r_s, s5i_s = _s5_scan(hs, norm1[1], s5w, s5_d[0], state_s5_re[0].reshape(n_s, n_state),
                                  state_s5_im[0].reshape(n_s, n_state), sg, dec_seq)
    h = jnp.concatenate([hp, hs], axis=0)
    zg = jnp.concatenate([zg_p, zg_s], axis=0)
    h, xn, eid, we = _proj_router([zg], [s5_w_glu[0].astype(BF16)], s5_b_glu[0], h, norm2[1], routers[1], True)
    y = _moe_combine(h, moe(1, xn, eid, we), norm_f, True)
    y_prompt = y[:np_rows].reshape(seq, n_p, d).transpose(1, 0, 2)
    y_sample = y[np_rows:].reshape(n_s // sg, dec_seq, sg, d).transpose(0, 2, 1, 3).reshape(n_s, dec_seq, d)

    kv_p = lambda a: a[:np_rows].reshape(1, n_p, seq // page, page, H_A, HD_A)
    st = lambda a, n: a.reshape((1, n) + s5_lam_re.shape[1:])
    return (y_prompt, y_sample, kv_p(ka), kv_p(va), k_s[None], v_s[None], ret_p[None], ret_s[None],
            st(s5r_p, n_p), st(s5i_p, n_p), st(s5r_s, n_s), st(s5i_s, n_s))
```

```python
import functools
import math

import jax
import jax.numpy as jnp
from jax import lax
from jax.experimental import pallas as pl
from jax.experimental.pallas import tpu as pltpu

F32 = jnp.float32
BF16 = jnp.bfloat16

H_A, HD_A = 8, 64
ROT_DIM = HD_A // 4
ROPE_THETA = 500000.0
MOBA_BLOCK = 256
MOBA_TOPK = 3
H_R, DK_R, DV_R = 8, 64, 128
RET_CHUNK = 128
S5_GROUP, S5_STATE = 16, 64
MOE_GROUPS, MOE_PER_GROUP, MOE_TOPK = 4, 8, 2
N_EXPERTS = MOE_GROUPS * MOE_PER_GROUP
A_W = H_A * HD_A
R_QK = H_R * DK_R
R_V = H_R * DV_R
NEG_INF = -1e30
EPS = 1e-6

LANES = 128
SUBLANES = 8
VMEM_LIMIT = 56 * 1024 * 1024

ROW_TILE = 256
MOE_ROWS = 256
MOE_DMA_UNROLL = 8


def _nt(a, b):
    return lax.dot_general(a, b, (((1,), (1,)), ((), ())), preferred_element_type=F32)


def _tn(a, b):
    return lax.dot_general(a, b, (((0,), (0,)), ((), ())), preferred_element_type=F32)


def _mm(a, b):
    return jnp.dot(a, b, preferred_element_type=F32)


def _rms_rows(x, g):
    return x * lax.rsqrt(jnp.mean(x * x, axis=-1, keepdims=True) + EPS) * g


def _split_bf16(x):
    hi = x.astype(BF16)
    lo = (x - hi.astype(F32)).astype(BF16)
    return hi, lo


def _params(sem):
    return pltpu.CompilerParams(dimension_semantics=sem, vmem_limit_bytes=VMEM_LIMIT)


def _rotate_into(out_ref, z, c_ref, s_ref, shift, first, scale):
    for j in range(z.shape[1] // LANES):
        sl = slice(j * LANES, (j + 1) * LANES)
        zc = z[:, sl]
        up = pltpu.roll(zc, LANES - shift, axis=1)
        dn = pltpu.roll(zc, shift, axis=1)
        r = zc * c_ref[:, sl] + jnp.where(first, up, dn) * s_ref[:, sl]
        out_ref[:, sl] = r if scale is None else r * scale


def _in_proj_kernel(x_ref, g_ref, w_ref, ca_ref, sa_ref, cr_ref, sr_ref,
                    qa_ref, ka_ref, va_ref, qr_ref, kr_ref, vr_ref, gr_ref):
    xn = _rms_rows(x_ref[...], g_ref[...]).astype(BF16)
    lane = lax.broadcasted_iota(jnp.int32, (1, LANES), 1)
    first_a = (lane % HD_A) < (ROT_DIM // 2)
    first_r = (lane % 2) == 0

    def sec(lo, width):
        return _mm(xn, w_ref[:, lo:lo + width])

    _rotate_into(qa_ref, sec(0, A_W), ca_ref, sa_ref, ROT_DIM // 2, first_a, None)
    _rotate_into(ka_ref, sec(A_W, A_W), ca_ref, sa_ref, ROT_DIM // 2, first_a, None)
    va_ref[...] = sec(2 * A_W, A_W)
    _rotate_into(qr_ref, sec(3 * A_W, R_QK), cr_ref, sr_ref, 1, first_r, None)
    _rotate_into(kr_ref, sec(3 * A_W + R_QK, R_QK), cr_ref, sr_ref, 1, first_r, DK_R ** -0.5)
    vr_ref[...] = sec(3 * A_W + 2 * R_QK, R_V)
    gr_ref[...] = sec(3 * A_W + 2 * R_QK + R_V, R_V)


def _rope_tables(pos):
    half = ROT_DIM // 2
    inv = ROPE_THETA ** (-jnp.arange(half, dtype=F32) / half)
    ang = pos.astype(F32)[:, None] * inv[None, :]
    cos, sin = jnp.cos(ang), jnp.sin(ang)
    rest = HD_A - ROT_DIM
    c = jnp.concatenate([cos, cos, jnp.ones((pos.shape[0], rest), F32)], axis=-1)
    s = jnp.concatenate([-sin, sin, jnp.zeros((pos.shape[0], rest), F32)], axis=-1)
    return jnp.tile(c, (1, H_A)), jnp.tile(s, (1, H_A))


def _retnet_tables(pos):
    n = DK_R // 2
    inv = 1.0 / (10000.0 ** jnp.linspace(0.0, 1.0, n, dtype=F32))
    ang = pos.astype(F32)[:, None] * inv[None, :]
    cos, sin = jnp.cos(ang), jnp.sin(ang)
    c = jnp.repeat(cos, 2, axis=-1)
    s = jnp.stack([-sin, sin], axis=-1).reshape(pos.shape[0], DK_R)
    return jnp.tile(c, (1, H_R)), jnp.tile(s, (1, H_R))


def _in_proj(h, g, w_bf16, seq, n_prompt_rows, dec_seq, past_len):
    n, d = h.shape
    tm = ROW_TILE
    tiles_per_seq = seq // tm
    n_ptiles = n_prompt_rows // tm
    pos = jnp.concatenate([jnp.arange(seq), past_len + (jnp.arange(tm) % dec_seq)])
    ca, sa = _rope_tables(pos)
    cr, sr = _retnet_tables(pos)

    def tab_map(i):
        return (jnp.where(i < n_ptiles, i % tiles_per_seq, tiles_per_seq), 0)

    row = lambda w: pl.BlockSpec((tm, w), lambda i: (i, 0))
    tab = pl.BlockSpec((tm, A_W), tab_map)
    widths = (A_W, A_W, A_W, R_QK, R_QK, R_V, R_V)
    return pl.pallas_call(
        _in_proj_kernel,
        grid=(n // tm,),
        in_specs=[row(d), pl.BlockSpec((1, d), lambda i: (0, 0)),
                  pl.BlockSpec(w_bf16.shape, lambda i: (0, 0)), tab, tab, tab, tab],
        out_specs=[row(w) for w in widths],
        out_shape=[jax.ShapeDtypeStruct((n, w), F32) for w in widths],
        compiler_params=_params(("arbitrary",)),
        name="in_proj",
    )(h, g.reshape(1, d), w_bf16, ca, sa, cr, sr)


def _moba_select(q_f32, kmean, n_valid):
    n_blk = kmean.shape[0]
    qh, ql = _split_bf16(q_f32)
    kh, kl = _split_bf16(kmean)
    st = _nt(jnp.concatenate([kh, kl, kh], axis=1), jnp.concatenate([qh, qh, ql], axis=1))
    jrow = lax.broadcasted_iota(jnp.int32, st.shape, 0)
    rank = jnp.zeros(st.shape, F32)
    for jp in range(n_blk):
        sj = st[jp:jp + 1, :]
        beats = (sj > st) | ((sj == st) & (jp < jrow))
        rank = rank + jnp.where(beats & (jp < n_valid), 1.0, 0.0)
    sel_t = jnp.where((jrow < n_valid) & (rank < MOBA_TOPK), 1.0, 0.0).astype(BF16)
    rows = q_f32.shape[0]
    eye = (lax.broadcasted_iota(jnp.int32, (rows, rows), 0)
           == lax.broadcasted_iota(jnp.int32, (rows, rows), 1))
    return _nt(jnp.where(eye, 1.0, 0.0).astype(BF16), sel_t)


def _softmax_step(carry, s, v_bf16):
    m, l, acc = carry
    m_new = jnp.maximum(m, jnp.max(s, axis=1, keepdims=True))
    alpha = jnp.exp(m - m_new)
    p = jnp.exp(s - m_new)
    l = alpha * l + jnp.sum(p, axis=1, keepdims=True)
    acc = alpha * acc + _mm(p.astype(BF16), v_bf16)
    return m_new, l, acc


def _moba_prompt_kernel(q_ref, k_ref, v_ref, o_ref, kb_ref, vb_ref, km_ref, sel_ref):
    b = pl.program_id(2)
    blk = MOBA_BLOCK
    n_blk = k_ref.shape[0] // blk

    @pl.when(b == 0)
    def _():
        kb_ref[...] = k_ref[...].astype(BF16)
        vb_ref[...] = v_ref[...].astype(BF16)
        for j in range(n_blk):
            km_ref[j:j + 1, :] = jnp.mean(k_ref[j * blk:(j + 1) * blk, :], axis=0, keepdims=True)

    q = q_ref[...]
    lane = lax.broadcasted_iota(jnp.int32, (1, LANES), 1)
    causal = (lax.broadcasted_iota(jnp.int32, (blk, blk), 1)
              <= lax.broadcasted_iota(jnp.int32, (blk, blk), 0))
    own = pl.ds(pl.multiple_of(b * blk, blk), blk)
    heads = range(LANES // HD_A)
    hms, qss, carries = [], [], []
    for hh in heads:
        hm = (lane // HD_A) == hh
        qh = jnp.where(hm, q, 0.0)
        sel = _moba_select(qh, km_ref[...], b)
        for j in range(n_blk):
            sel_ref[hh, j] = sel[:, j:j + 1]
        qs = (qh * (HD_A ** -0.5)).astype(BF16)
        s0 = jnp.where(causal, _nt(qs, kb_ref[own, :]), NEG_INF)
        init = (jnp.full((blk, 1), NEG_INF, F32), jnp.zeros((blk, 1), F32), jnp.zeros(q.shape, F32))
        hms.append(hm)
        qss.append(qs)
        carries.append(_softmax_step(init, s0, vb_ref[own, :]))

    def past(j, carries):
        rows = pl.ds(pl.multiple_of(j * blk, blk), blk)
        kj, vj = kb_ref[rows, :], vb_ref[rows, :]
        return tuple(
            _softmax_step(carries[hh], jnp.where(sel_ref[hh, j] > 0.5, _nt(qss[hh], kj), NEG_INF), vj)
            for hh in heads)

    carries = lax.fori_loop(0, b, past, tuple(carries))
    out = jnp.zeros(q.shape, F32)
    for hh in heads:
        _, l, acc = carries[hh]
        out = out + jnp.where(hms[hh], acc / l, 0.0)
    o_ref[...] = out


def _moba_prompt(q_a, k_a, v_a, n_seq, seq):
    blk = MOBA_BLOCK
    n_blk = seq // blk
    qspec = pl.BlockSpec((blk, LANES), lambda s, h, b: (s * n_blk + b, h))
    kspec = pl.BlockSpec((seq, LANES), lambda s, h, b: (s, h))
    return pl.pallas_call(
        _moba_prompt_kernel,
        grid=(n_seq, A_W // LANES, n_blk),
        in_specs=[qspec, kspec, kspec],
        out_specs=qspec,
        out_shape=jax.ShapeDtypeStruct((n_seq * seq, A_W), F32),
        scratch_shapes=[pltpu.VMEM((seq, LANES), BF16), pltpu.VMEM((seq, LANES), BF16),
                        pltpu.VMEM((n_blk, LANES), F32), pltpu.VMEM((LANES // HD_A, n_blk, blk, 1), F32)],
        compiler_params=_params(("arbitrary", "arbitrary", "arbitrary")),
        name="moba_prompt",
    )(q_a, k_a, v_a)


def _ret_decay_tables(c):
    log_g = jnp.log(1.0 - 2.0 ** (-5.0 - jnp.arange(H_R, dtype=F32)))
    i = jnp.arange(c, dtype=F32)
    diff = i[:, None] - i[None, :]
    dmat = jnp.where(diff >= 0, jnp.exp(jnp.maximum(diff, 0.0)[None] * log_g[:, None, None]), 0.0)
    dq = jnp.exp((i + 1.0)[None, :] * log_g[:, None])
    dk = jnp.exp((c - 1.0 - i)[None, :] * log_g[:, None])
    dc = jnp.exp(c * log_g)
    return dmat, dq, dk, dc


def _ret_chunk(q, k, v, g, s, dmat, dq, dk, dc):
    att = _nt(q.astype(BF16), k.astype(BF16)) * dmat
    o = _mm(att.astype(BF16), v.astype(BF16)) + _mm((q * dq).astype(BF16), s.astype(BF16))
    s = s * dc + _tn((k * dk).astype(BF16), v.astype(BF16))
    o = o * lax.rsqrt(jnp.mean(o * o, axis=-1, keepdims=True) + EPS)
    return o * (g * jax.nn.sigmoid(g)), s


def _ret_prompt_kernel(q_ref, k_ref, v_ref, g_ref, dmat_ref, dq_ref, dk_ref, dc_ref, o_ref, st_ref):
    c = RET_CHUNK
    per = LANES // DK_R
    lane = lax.broadcasted_iota(jnp.int32, (1, LANES), 1)

    def chunk(i, states):
        rows = pl.ds(pl.multiple_of(i * c, c), c)
        q_all, k_all = q_ref[rows, :], k_ref[rows, :]
        out = []
        for hh in range(per):
            hm = (lane // DK_R) == hh
            vl = slice(hh * DV_R, (hh + 1) * DV_R)
            o, s = _ret_chunk(jnp.where(hm, q_all, 0.0), jnp.where(hm, k_all, 0.0), v_ref[rows, vl], g_ref[rows, vl],
                              states[hh], dmat_ref[hh], dq_ref[hh], dk_ref[hh], dc_ref[hh])
            o_ref[rows, vl] = o
            out.append(s)
        return tuple(out)

    states = lax.fori_loop(0, q_ref.shape[0] // c, chunk, tuple(jnp.zeros((LANES, DV_R), F32) for _ in range(per)))
    for hh in range(per):
        st_ref[hh] = states[hh][hh * DK_R:(hh + 1) * DK_R, :]


def _ret_tables_bcast(c, rows):
    dmat, dq, dk, dc = _ret_decay_tables(c)
    pad = rows - c
    dmat = jnp.pad(dmat, ((0, 0), (0, pad), (0, pad)))
    dq = jnp.broadcast_to(jnp.pad(dq, ((0, 0), (0, pad)))[:, :, None], (H_R, rows, LANES))
    dk = jnp.broadcast_to(jnp.pad(dk, ((0, 0), (0, pad)))[:, :, None], (H_R, rows, LANES))
    dc = jnp.broadcast_to(dc[:, None, None], (H_R, 1, LANES))
    return dmat, dq, dk, dc


def _ret_prompt(q_r, k_r, v_r, g_r, n_seq, seq):
    c = RET_CHUNK
    dmat, dq, dk, dc = _ret_tables_bcast(c, c)
    per = LANES // DK_R
    qk = pl.BlockSpec((seq, LANES), lambda s, h: (s, h))
    vg = pl.BlockSpec((seq, per * DV_R), lambda s, h: (s, h))
    tab = lambda r: pl.BlockSpec((per, r, LANES), lambda s, h: (h, 0, 0))
    return pl.pallas_call(
        _ret_prompt_kernel,
        grid=(n_seq, H_R // per),
        in_specs=[qk, qk, vg, vg, tab(c), tab(c), tab(c), tab(1)],
        out_specs=[vg, pl.BlockSpec((None, per, DK_R, DV_R), lambda s, h: (s, h, 0, 0))],
        out_shape=[jax.ShapeDtypeStruct((n_seq * seq, R_V), F32),
                   jax.ShapeDtypeStruct((n_seq, H_R, DK_R, DV_R), F32)],
        compiler_params=_params(("arbitrary", "arbitrary")),
        name="ret_prompt",
    )(q_r, k_r, v_r, g_r, dmat, dq, dk, dc)


def _pad_rows(x, rows):
    return jnp.concatenate([x, jnp.zeros((rows - x.shape[0], x.shape[1]), x.dtype)], axis=0)


def _moba_sample_kernel(pt_ref, q_ref, kn_ref, vn_ref, *rest, n_pages):
    k_refs, v_refs = rest[:n_pages], rest[n_pages:2 * n_pages]
    o_ref, s_ref = rest[2 * n_pages], rest[2 * n_pages + 1]
    q = q_ref[...]
    t = q.shape[0]
    page = k_refs[0].shape[-1]
    per_blk = MOBA_BLOCK // page
    n_blk = n_pages // per_blk
    lane = lax.broadcasted_iota(jnp.int32, (1, A_W), 1)
    qbd = jnp.concatenate([jnp.where((lane // HD_A) == h, q, 0.0) for h in range(H_A)], axis=0)
    qs = (qbd * (HD_A ** -0.5)).astype(BF16)
    rows = qbd.shape[0]

    bsum = [jnp.zeros((rows, 1), F32) for _ in range(n_blk)]
    for p in range(n_pages):
        sp = _mm(qs, k_refs[p][...].reshape(A_W, page).astype(BF16))
        s_ref[:, p * page:(p + 1) * page] = sp
        bsum[p // per_blk] = bsum[p // per_blk] + jnp.sum(sp, axis=1, keepdims=True)
    sel = []
    for j in range(n_blk):
        rank = jnp.zeros((rows, 1), F32)
        for jp in range(n_blk):
            if jp != j:
                beats = (bsum[jp] > bsum[j]) | (bsum[jp] == bsum[j]) if jp < j else bsum[jp] > bsum[j]
                rank = rank + jnp.where(beats, 1.0, 0.0)
        sel.append(rank < MOBA_TOPK)

    qi = lax.broadcasted_iota(jnp.int32, (rows, LANES), 0) % t
    causal = lax.broadcasted_iota(jnp.int32, (rows, LANES), 1) <= qi
    s0 = jnp.where(causal, _nt(qs, _pad_rows(kn_ref[...], LANES).astype(BF16)), NEG_INF)
    m = jnp.max(s0, axis=1, keepdims=True)
    p0 = jnp.exp(s0 - m)
    carry = (m, jnp.sum(p0, axis=1, keepdims=True), _mm(p0.astype(BF16), _pad_rows(vn_ref[...], LANES).astype(BF16)))
    for p in range(n_pages):
        m, l, acc = carry
        s = jnp.where(sel[p // per_blk], s_ref[:, p * page:(p + 1) * page], NEG_INF)
        m_new = jnp.maximum(m, jnp.max(s, axis=1, keepdims=True))
        alpha = jnp.exp(m - m_new)
        pr = jnp.exp(s - m_new)
        carry = (m_new, alpha * l + jnp.sum(pr, axis=1, keepdims=True),
                 alpha * acc + _nt(pr.astype(BF16), v_refs[p][...].reshape(A_W, page).astype(BF16)))
    _, l, acc = carry
    o = acc / l
    out = jnp.zeros((t, A_W), F32)
    for h in range(H_A):
        out = out + jnp.where((lane // HD_A) == h, o[h * t:(h + 1) * t, :], 0.0)
    o_ref[...] = out


def _moba_sample(q_a, k_a, v_a, cache_kt, cache_vt, page_table, n_prompt_rows, dec_seq):
    n_s, n_pages = page_table.shape
    base = n_prompt_rows // dec_seq
    new = pl.BlockSpec((dec_seq, A_W), lambda n, pt: (base + n, 0))
    pages = [pl.BlockSpec((None,) + cache_kt.shape[1:], lambda n, pt, p=p: (pt[n * n_pages + p], 0, 0, 0))
             for p in range(n_pages)]
    grid_spec = pltpu.PrefetchScalarGridSpec(
        num_scalar_prefetch=1,
        grid=(n_s,),
        in_specs=[new, new, new] + pages + pages,
        out_specs=pl.BlockSpec((dec_seq, A_W), lambda n, pt: (n, 0)),
        scratch_shapes=[pltpu.VMEM((H_A * dec_seq, n_pages * cache_kt.shape[-1]), F32)],
    )
    return pl.pallas_call(
        functools.partial(_moba_sample_kernel, n_pages=n_pages),
        grid_spec=grid_spec,
        out_shape=jax.ShapeDtypeStruct((n_s * dec_seq, A_W), F32),
        compiler_params=_params(("arbitrary",)),
        name="moba_sample",
    )(page_table.reshape(-1), q_a, k_a, v_a, *([cache_kt] * n_pages), *([cache_vt] * n_pages))


def _ret_sample_kernel(q_ref, k_ref, v_ref, g_ref, s0_ref, dmat_ref, dq_ref, dk_ref, dc_ref, o_ref, st_ref):
    t = q_ref.shape[0]
    lane = lax.broadcasted_iota(jnp.int32, (1, LANES), 1)
    per = LANES // DK_R
    zero_half = jnp.zeros((DK_R, DV_R), F32)
    for h in range(H_R):
        hh = h % per
        hm = (lane // DK_R) == hh
        qk_l = slice((h // per) * LANES, (h // per + 1) * LANES)
        v_l = slice(h * DV_R, (h + 1) * DV_R)
        q = _pad_rows(jnp.where(hm, q_ref[:, qk_l], 0.0), LANES)
        k = _pad_rows(jnp.where(hm, k_ref[:, qk_l], 0.0), LANES)
        v = _pad_rows(v_ref[:, v_l], LANES)
        g = _pad_rows(g_ref[:, v_l], LANES)
        halves = [zero_half] * per
        halves[hh] = s0_ref[h]
        o, s = _ret_chunk(q, k, v, g, jnp.concatenate(halves, axis=0),
                          dmat_ref[h], dq_ref[h], dk_ref[h], dc_ref[h])
        o_ref[:, v_l] = o[:t, :]
        st_ref[h] = s[hh * DK_R:(hh + 1) * DK_R, :]


def _ret_sample(q_r, k_r, v_r, g_r, s0, n_prompt_rows, dec_seq):
    n_s = s0.shape[0]
    base = n_prompt_rows // dec_seq
    dmat, dq, dk, dc = _ret_tables_bcast(dec_seq, LANES)
    qk = pl.BlockSpec((dec_seq, R_QK), lambda n: (base + n, 0))
    vg = pl.BlockSpec((dec_seq, R_V), lambda n: (base + n, 0))
    st = pl.BlockSpec((None, H_R, DK_R, DV_R), lambda n: (n, 0, 0, 0))
    full = lambda a: pl.BlockSpec(a.shape, lambda n: (0,) * a.ndim)
    return pl.pallas_call(
        _ret_sample_kernel,
        grid=(n_s,),
        in_specs=[qk, qk, vg, vg, st, full(dmat), full(dq), full(dk), full(dc)],
        out_specs=[pl.BlockSpec((dec_seq, R_V), lambda n: (n, 0)), st],
        out_shape=[jax.ShapeDtypeStruct((n_s * dec_seq, R_V), F32), jax.ShapeDtypeStruct(s0.shape, F32)],
        compiler_params=_params(("arbitrary",)),
        name="ret_sample",
    )(q_r, k_r, v_r, g_r, s0, dmat, dq, dk, dc)


ROUTER_ROWS = SUBLANES + N_EXPERTS


def _route(xn, wr_ref, br_ref, eid_ref, we_ref):
    xh, xl = _split_bf16(xn)
    lt = _nt(wr_ref[...], jnp.concatenate([xh, xh, xl], axis=1)) + br_ref[...]
    tm = lt.shape[1]
    r8 = lax.broadcasted_iota(jnp.int32, (SUBLANES, tm), 0)
    lg = jnp.where(r8 < MOE_GROUPS, lt[:SUBLANES, :], NEG_INF)
    mg = jnp.max(lg, axis=0, keepdims=True)
    wg = 1.0 / jnp.sum(jnp.exp(lg - mg), axis=0, keepdims=True)
    gidx = jnp.min(jnp.where(lg == mg, r8, SUBLANES), axis=0, keepdims=True)
    le = jnp.zeros((MOE_PER_GROUP, tm), F32)
    for gi in range(MOE_GROUPS):
        lo = SUBLANES + gi * MOE_PER_GROUP
        le = le + jnp.where(gidx == gi, lt[lo:lo + MOE_PER_GROUP, :], 0.0)
    v1 = jnp.max(le, axis=0, keepdims=True)
    i1 = jnp.min(jnp.where(le == v1, r8, MOE_PER_GROUP), axis=0, keepdims=True)
    le2 = jnp.where(r8 == i1, -jnp.inf, le)
    v2 = jnp.max(le2, axis=0, keepdims=True)
    i2 = jnp.min(jnp.where(le2 == v2, r8, MOE_PER_GROUP), axis=0, keepdims=True)
    e21 = jnp.exp(v2 - v1)
    w1 = wg / (1.0 + e21)
    eid_ref[0:1, :] = gidx * MOE_PER_GROUP + i1
    eid_ref[1:2, :] = gidx * MOE_PER_GROUP + i2
    we_ref[0:1, :] = w1
    we_ref[1:2, :] = w1 * e21


def _proj_router_kernel(*refs, n_in, glu):
    a_refs, w_refs = refs[:n_in], refs[n_in:2 * n_in]
    k = 2 * n_in
    b_ref = refs[k] if glu else None
    k += int(glu)
    h_ref, g_ref, wr_ref, br_ref, ho_ref, xn_ref, eid_ref, we_ref = refs[k:k + 8]
    acc = None
    for a_ref, w_ref in zip(a_refs, w_refs):
        part = _mm(a_ref[...].astype(BF16), w_ref[...])
        acc = part if acc is None else acc + part
    if glu:
        acc = acc + b_ref[...]
        half = acc.shape[1] // 2
        acc = acc[:, :half] * jax.nn.sigmoid(acc[:, half:])
    hn = h_ref[...] + acc
    ho_ref[...] = hn
    xn = _rms_rows(hn, g_ref[...])
    xn_ref[...] = xn
    _route(xn, wr_ref, br_ref, eid_ref, we_ref)


def _router_weights(w_group, b_group, w_expert, b_expert):
    d = w_group.shape[0]
    wt = jnp.concatenate([w_group.T, jnp.zeros((SUBLANES - MOE_GROUPS, d), F32), w_expert.T], axis=0)
    hi = wt.astype(BF16)
    lo = (wt - hi.astype(F32)).astype(BF16)
    bias = jnp.concatenate([b_group, jnp.zeros((SUBLANES - MOE_GROUPS,), F32), b_expert]).reshape(-1, 1)
    return jnp.concatenate([hi, lo, hi], axis=1), bias


def _proj_router(acts, weights, bias, h, g, router, glu):
    n, d = h.shape
    tm = ROW_TILE
    wr, br = router
    row = lambda w: pl.BlockSpec((tm, w), lambda i: (i, 0))
    full = lambda a: pl.BlockSpec(a.shape, lambda i: (0,) * a.ndim)
    tok = pl.BlockSpec((MOE_TOPK, tm), lambda i: (0, i))
    operands = list(acts) + list(weights) + ([bias.reshape(1, -1)] if glu else []) + [h, g.reshape(1, d), wr, br]
    in_specs = ([row(a.shape[1]) for a in acts] + [full(w) for w in weights]
                + ([pl.BlockSpec((1, bias.shape[0]), lambda i: (0, 0))] if glu else [])
                + [row(d), pl.BlockSpec((1, d), lambda i: (0, 0)), full(wr), full(br)])
    return pl.pallas_call(
        functools.partial(_proj_router_kernel, n_in=len(acts), glu=glu),
        grid=(n // tm,),
        in_specs=in_specs,
        out_specs=[row(d), row(d), tok, tok],
        out_shape=[jax.ShapeDtypeStruct((n, d), F32), jax.ShapeDtypeStruct((n, d), F32),
                   jax.ShapeDtypeStruct((MOE_TOPK, n), jnp.int32), jax.ShapeDtypeStruct((MOE_TOPK, n), F32)],
        compiler_params=_params(("arbitrary",)),
        name="glu_router" if glu else "out_proj_router",
    )(*operands)


def _moe_plan(eid, we):
    n_tok = eid.shape[1]
    n_assign = MOE_TOPK * n_tok
    blk = MOE_ROWS
    n_blocks = -(-n_assign // blk) + N_EXPERTS
    e_flat = eid.reshape(-1)
    experts = jnp.arange(N_EXPERTS, dtype=jnp.int32)
    counts = jnp.sum((e_flat[:, None] == experts[None, :]).astype(jnp.int32), axis=0)
    order = jnp.argsort(e_flat).astype(jnp.int32)
    pc = (counts + blk - 1) // blk * blk
    pend = jnp.cumsum(pc)
    pstart = pend - pc
    start = jnp.cumsum(counts) - counts
    first_row = jnp.arange(n_blocks, dtype=jnp.int32) * blk
    blk_e = jnp.minimum(jnp.sum((pend[None, :] <= first_row[:, None]).astype(jnp.int32), axis=1), N_EXPERTS - 1)
    n_valid = jnp.clip(counts[blk_e] - (first_row - pstart[blk_e]), 0, blk).astype(jnp.int32)
    row = first_row[:, None] + jnp.arange(blk, dtype=jnp.int32)[None, :]
    off = row - pstart[blk_e][:, None]
    valid = off < counts[blk_e][:, None]
    a_row = order[jnp.clip(start[blk_e][:, None] + off, 0, n_assign - 1)]
    a_row = jnp.where(valid, a_row, 0).reshape(-1)
    w_row = jnp.where(valid.reshape(-1), we.reshape(-1)[a_row], 0.0)
    n_used = (pend[-1] // blk).astype(jnp.int32).reshape(1)
    return blk_e, n_used, n_valid, a_row % n_tok, a_row, w_row.reshape(-1, 1), n_blocks


def _moe_kernel(blk_e_ref, n_used_ref, n_valid_ref, src_ref, dst_ref, x_hbm, w13_ref, w2_ref, wrow_ref, y_hbm,
                xbuf, ybuf, gsem, ssem):
    b = pl.program_id(0)
    n_used = n_used_ref[0]
    blk = MOE_ROWS
    slot = b % 2

    def gather_row(block, slot, r):
        return pltpu.make_async_copy(x_hbm.at[pl.ds(src_ref[block * blk + r], 1), :],
                                     xbuf.at[slot, pl.ds(r, 1), :], gsem.at[slot])

    def scatter_row(block, slot, r):
        return pltpu.make_async_copy(ybuf.at[slot, pl.ds(r, 1), :],
                                     y_hbm.at[pl.ds(dst_ref[block * blk + r], 1), :], ssem.at[slot])

    def gather_all(slot):
        return pltpu.make_async_copy(x_hbm.at[pl.ds(0, blk), :], xbuf.at[slot], gsem.at[slot])

    def scatter_all(slot):
        return pltpu.make_async_copy(ybuf.at[slot], y_hbm.at[pl.ds(0, blk), :], ssem.at[slot])

    def start_rows(row_copy, block, slot):
        nv = n_valid_ref[block]

        def body(r, c):
            row_copy(block, slot, r).start()
            return c

        def body8(i, c):
            for k in range(MOE_DMA_UNROLL):
                row_copy(block, slot, i * MOE_DMA_UNROLL + k).start(priority=k % 2)
            return c

        @pl.when(nv == blk)
        def _():
            lax.fori_loop(0, blk // MOE_DMA_UNROLL, body8, 0)

        @pl.when(nv < blk)
        def _():
            lax.fori_loop(0, nv, body, 0)

    def wait_rows(row_copy, all_copy, block, slot):
        nv = n_valid_ref[block]

        @pl.when(nv == blk)
        def _():
            all_copy(slot).wait()

        @pl.when(nv < blk)
        def _():
            def body(r, c):
                row_copy(block, slot, r).wait()
                return c
            lax.fori_loop(0, nv, body, 0)

    @pl.when((b == 0) & (n_used > 0))
    def _():
        xbuf[...] = jnp.zeros(xbuf.shape, xbuf.dtype)
        start_rows(gather_row, 0, 0)

    @pl.when(b + 1 < n_used)
    def _():
        start_rows(gather_row, b + 1, 1 - slot)

    @pl.when(b < n_used)
    def _():
        wait_rows(gather_row, gather_all, b, slot)

        @pl.when(b >= 2)
        def _():
            wait_rows(scatter_row, scatter_all, b - 2, slot)

        hb = _mm(xbuf[slot].astype(BF16), w13_ref[...])
        half = hb.shape[1] // 2
        gate = hb[:, :half]
        act = gate * jax.nn.sigmoid(gate) * hb[:, half:]
        ybuf[slot] = _mm(act.astype(BF16), w2_ref[...]) * wrow_ref[...]
        start_rows(scatter_row, b, slot)

    @pl.when(b == pl.num_programs(0) - 1)
    def _():
        for back in (2, 1):
            last = n_used - back

            @pl.when(last >= 0)
            def _():
                wait_rows(scatter_row, scatter_all, last, last % 2)


def _moe_experts(xn, eid, we, w13_bf16, w2_bf16):
    n_tok, d = xn.shape
    blk = MOE_ROWS
    blk_e, n_used, n_valid, src_tok, dst_row, w_row, n_blocks = _moe_plan(eid, we)
    grid_spec = pltpu.PrefetchScalarGridSpec(
        num_scalar_prefetch=5,
        grid=(n_blocks,),
        in_specs=[pl.BlockSpec(memory_space=pl.ANY),
                  pl.BlockSpec((None,) + w13_bf16.shape[1:], lambda b, be, *_: (be[b], 0, 0)),
                  pl.BlockSpec((None,) + w2_bf16.shape[1:], lambda b, be, *_: (be[b], 0, 0)),
                  pl.BlockSpec((blk, 1), lambda b, *_: (b, 0))],
        out_specs=pl.BlockSpec(memory_space=pl.ANY),
        scratch_shapes=[pltpu.VMEM((2, blk, d), F32), pltpu.VMEM((2, blk, d), F32),
                        pltpu.SemaphoreType.DMA((2,)), pltpu.SemaphoreType.DMA((2,))],
    )
    return pl.pallas_call(
        _moe_kernel,
        grid_spec=grid_spec,
        out_shape=jax.ShapeDtypeStruct((MOE_TOPK * n_tok, d), F32),
        compiler_params=_params(("arbitrary",)),
        name="moe_experts",
    )(blk_e, n_used, n_valid, src_tok, dst_row, xn, w13_bf16, w2_bf16, w_row)


def _combine_kernel(h_ref, y0_ref, y1_ref, g_ref, *out_refs, norm):
    h = h_ref[...] + (y0_ref[...] + y1_ref[...])
    out_refs[0][...] = _rms_rows(h, g_ref[...]) if norm else h


def _moe_combine(h, y, g, norm):
    n, d = h.shape
    tm = ROW_TILE
    row = pl.BlockSpec((tm, d), lambda i: (i, 0))
    return pl.pallas_call(
        functools.partial(_combine_kernel, norm=norm),
        grid=(n // tm,),
        in_specs=[row, row, pl.BlockSpec((tm, d), lambda i: (n // tm + i, 0)),
                  pl.BlockSpec((1, d), lambda i: (0, 0))],
        out_specs=row,
        out_shape=jax.ShapeDtypeStruct((n, d), F32),
        compiler_params=_params(("arbitrary",)),
        name="moe_combine_norm" if norm else "moe_combine",
    )(h, y, y, g.reshape(1, d))


S5_CH = LANES
S5_ST = (LANES // S5_GROUP) * S5_STATE
S5_SCAN_LANES = 512


def _s5_kernel(h_ref, g_ref, bre_ref, bim_ref, are_ref, aim_ref, cre_ref, cim_ref, d_ref, s0r_ref, s0i_ref,
               zg_ref, fr_ref, fi_ref, xr, xi, sr, si, *, s, steps):
    c = pl.program_id(1)

    @pl.when(c == 0)
    def _():
        sr[...] = s0r_ref[...]
        si[...] = s0i_ref[...]

    u = _rms_rows(h_ref[...], g_ref[...])
    ub = u.astype(BF16)
    n_ch = u.shape[1] // S5_CH
    for j in range(n_ch):
        uj = ub[:, j * S5_CH:(j + 1) * S5_CH]
        xr[:, j * S5_ST:(j + 1) * S5_ST] = _mm(uj, bre_ref[j])
        xi[:, j * S5_ST:(j + 1) * S5_ST] = _mm(uj, bim_ref[j])

    for lo in range(0, xr.shape[1], S5_SCAN_LANES):
        ls = slice(lo, lo + S5_SCAN_LANES)
        ar = jnp.broadcast_to(are_ref[:, ls], (s, S5_SCAN_LANES))
        ai = jnp.broadcast_to(aim_ref[:, ls], (s, S5_SCAN_LANES))

        def step(t, carry):
            pr, pi = carry
            rows = pl.ds(pl.multiple_of(t * s, s), s)
            nr = ar * pr - ai * pi + xr[rows, ls]
            ni = ar * pi + ai * pr + xi[rows, ls]
            xr[rows, ls] = nr
            xi[rows, ls] = ni
            return nr, ni

        pr, pi = lax.fori_loop(0, steps, step, (sr[:, ls], si[:, ls]))
        sr[:, ls] = pr
        si[:, ls] = pi

    for j in range(n_ch):
        cs = slice(j * S5_CH, (j + 1) * S5_CH)
        ss = slice(j * S5_ST, (j + 1) * S5_ST)
        y = _mm(xr[:, ss].astype(BF16), cre_ref[j]) - _mm(xi[:, ss].astype(BF16), cim_ref[j])
        zg_ref[:, cs] = jax.nn.gelu(y + d_ref[:, cs] * u[:, cs])

    @pl.when(c == pl.num_programs(1) - 1)
    def _():
        fr_ref[...] = sr[...]
        fi_ref[...] = si[...]


def _s5_weights(lam_re, lam_im, log_dt, b_re, b_im, c_re, c_im):
    dt = jnp.exp(log_dt)[:, None]
    mag = jnp.exp(lam_re * dt)
    ang = lam_im * dt
    ab_re = mag * jnp.cos(ang)
    ab_im = mag * jnp.sin(ang)
    den = lam_re * lam_re + lam_im * lam_im
    nr = ab_re - 1.0
    co_re = (nr * lam_re + ab_im * lam_im) / den
    co_im = (ab_im * lam_re - nr * lam_im) / den
    bb_re = co_re[..., None] * b_re - co_im[..., None] * b_im
    bb_im = co_re[..., None] * b_im + co_im[..., None] * b_re
    per = S5_CH // S5_GROUP
    n_ch = lam_re.shape[0] // per
    eye = jnp.eye(per, dtype=F32)

    def in_blocks(bb):
        w = bb.transpose(0, 2, 1).reshape(n_ch, per, S5_GROUP, S5_STATE)
        return jnp.einsum('jgcp,gh->jgchp', w, eye).reshape(n_ch, S5_CH, S5_ST).astype(BF16)

    def out_blocks(cc):
        w = cc.transpose(0, 2, 1).reshape(n_ch, per, S5_STATE, S5_GROUP)
        return jnp.einsum('jgpc,gh->jgphc', w, eye).reshape(n_ch, S5_ST, S5_CH).astype(BF16)

    return (in_blocks(bb_re), in_blocks(bb_im), ab_re.reshape(1, -1), ab_im.reshape(1, -1),
            out_blocks(c_re), out_blocks(c_im))


def _s5_scan(h, g, weights, d_skip, s0_re, s0_im, s, steps):
    n, d = h.shape
    n_groups = s0_re.shape[0] // s
    rows = steps * s
    chunks = n // (n_groups * rows)
    bre, bim, are, aim, cre, cim = weights
    n_state = are.shape[1]
    full = lambda a: pl.BlockSpec(a.shape, lambda gi, c: (0,) * a.ndim)
    row = pl.BlockSpec((rows, d), lambda gi, c: (gi * chunks + c, 0))
    state = pl.BlockSpec((s, n_state), lambda gi, c: (gi, 0))
    return pl.pallas_call(
        functools.partial(_s5_kernel, s=s, steps=steps),
        grid=(n_groups, chunks),
        in_specs=[row, pl.BlockSpec((1, d), lambda gi, c: (0, 0)), full(bre), full(bim), full(are), full(aim),
                  full(cre), full(cim), pl.BlockSpec((1, d), lambda gi, c: (0, 0)), state, state],
        out_specs=[row, state, state],
        out_shape=[jax.ShapeDtypeStruct((n, d), F32), jax.ShapeDtypeStruct(s0_re.shape, F32),
                   jax.ShapeDtypeStruct(s0_re.shape, F32)],
        scratch_shapes=[pltpu.VMEM((rows, n_state), F32), pltpu.VMEM((rows, n_state), F32),
                        pltpu.VMEM((s, n_state), F32), pltpu.VMEM((s, n_state), F32)],
        compiler_params=_params(("arbitrary", "arbitrary")),
        name="s5_scan",
    )(h, g.reshape(1, d), bre, bim, are, aim, cre, cim, d_skip.reshape(1, d), s0_re, s0_im)


S5_PROMPT_STEPS = 32
S5_SAMPLE_SEQS = 32


def kernel(x_prompt, x_sample, cache_k, cache_v, state_ret, state_s5_re, state_s5_im, page_table, norm1, norm2, norm_f, w_in_even, w_out_even, s5_lam_re, s5_lam_im, s5_log_dt, s5_b_re, s5_b_im, s5_c_re, s5_c_im, s5_d, s5_w_glu, s5_b_glu, moe_w_group, moe_b_group, moe_w_expert, moe_b_expert, moe_w13, moe_w2):
    n_p, seq, d = x_prompt.shape
    n_s, dec_seq, _ = x_sample.shape
    pool, page = cache_k.shape[1], cache_k.shape[2]
    past_len = page_table.shape[1] * page
    np_rows, ns_rows = n_p * seq, n_s * dec_seq
    routers = [_router_weights(moe_w_group[li], moe_b_group[li], moe_w_expert[li], moe_b_expert[li])
               for li in range(2)]

    def moe(li, xn, eid, we):
        return _moe_experts(xn, eid, we, moe_w13[li].astype(BF16), moe_w2[li].astype(BF16))

    h = jnp.concatenate([x_prompt.reshape(np_rows, d), x_sample.reshape(ns_rows, d)], axis=0)
    qa, ka, va, qr, kr, vr, gr = _in_proj(h, norm1[0], w_in_even[0].astype(BF16), seq, np_rows, dec_seq, past_len)
    oa_p = _moba_prompt(qa, ka, va, n_p, seq)
    or_p, ret_p = _ret_prompt(qr, kr, vr, gr, n_p, seq)
    heads = lambda a: a[np_rows:].reshape(n_s, dec_seq, H_A, HD_A)
    k_s, v_s = heads(ka), heads(va)
    oa_s = _moba_sample(qa, ka, va, cache_k[0].transpose(0, 2, 3, 1), cache_v[0].transpose(0, 2, 3, 1),
                        page_table, np_rows, dec_seq)
    or_s, ret_s = _ret_sample(qr, kr, vr, gr, state_ret[0], np_rows, dec_seq)
    o_a = jnp.concatenate([oa_p, oa_s], axis=0)
    o_r = jnp.concatenate([or_p, or_s], axis=0)
    w_out = w_out_even[0].astype(BF16)
    h, xn, eid, we = _proj_router([o_a, o_r], [w_out[:A_W], w_out[A_W:]], None, h, norm2[0], routers[0], False)
    h = _moe_combine(h, moe(0, xn, eid, we), norm2[0], False)

    sg = S5_SAMPLE_SEQS
    hp = h[:np_rows].reshape(n_p, seq, d).transpose(1, 0, 2).reshape(np_rows, d)
    hs = h[np_rows:].reshape(n_s // sg, sg, dec_seq, d).transpose(0, 2, 1, 3).reshape(ns_rows, d)
    s5w = _s5_weights(s5_lam_re[0], s5_lam_im[0], s5_log_dt[0], s5_b_re[0], s5_b_im[0], s5_c_re[0], s5_c_im[0])
    n_state = s5_lam_re.shape[1] * s5_lam_re.shape[2]
    zeros = jnp.zeros((n_p, n_state), F32)
    zg_p, s5r_p, s5i_p = _s5_scan(hp, norm1[1], s5w, s5_d[0], zeros, zeros, n_p, S5_PROMPT_STEPS)
    zg_s, s5r_s, s5i_s = _s5_scan(hs, norm1[1], s5w, s5_d[0], state_s5_re[0].reshape(n_s, n_state),
                                  state_s5_im[0].reshape(n_s, n_state), sg, dec_seq)
    h = jnp.concatenate([hp, hs], axis=0)
    zg = jnp.concatenate([zg_p, zg_s], axis=0)
    h, xn, eid, we = _proj_router([zg], [s5_w_glu[0].astype(BF16)], s5_b_glu[0], h, norm2[1], routers[1], True)
    y = _moe_combine(h, moe(1, xn, eid, we), norm_f, True)
    y_prompt = y[:np_rows].reshape(seq, n_p, d).transpose(1, 0, 2)
    y_sample = y[np_rows:].reshape(n_s // sg, dec_seq, sg, d).transpose(0, 2, 1, 3).reshape(n_s, dec_seq, d)

    kv_p = lambda a: a[:np_rows].reshape(1, n_p, seq // page, page, H_A, HD_A)
    st = lambda a, n: a.reshape((1, n) + s5_lam_re.shape[1:])
    return (y_prompt, y_sample, kv_p(ka), kv_p(va), k_s[None], v_s[None], ret_p[None], ret_s[None],
            st(s5r_p, n_p), st(s5i_p, n_p), st(s5r_s, n_s), st(s5i_s, n_s))
```

```python
import functools
import math

import jax
import jax.numpy as jnp
from jax import lax
from jax.experimental import pallas as pl
from jax.experimental.pallas import tpu as pltpu

F32 = jnp.float32
BF16 = jnp.bfloat16

H_A, HD_A = 8, 64
ROT_DIM = HD_A // 4
ROPE_THETA = 500000.0
MOBA_BLOCK = 256
MOBA_TOPK = 3
H_R, DK_R, DV_R = 8, 64, 128
RET_CHUNK = 128
S5_GROUP, S5_STATE = 16, 64
MOE_GROUPS, MOE_PER_GROUP, MOE_TOPK = 4, 8, 2
N_EXPERTS = MOE_GROUPS * MOE_PER_GROUP
A_W = H_A * HD_A
R_QK = H_R * DK_R
R_V = H_R * DV_R
NEG_INF = -1e30
EPS = 1e-6

LANES = 128
SUBLANES = 8
VMEM_LIMIT = 56 * 1024 * 1024

ROW_TILE = 256
MOE_ROWS = 256
MOE_DMA_UNROLL = 8


def _nt(a, b):
    return lax.dot_general(a, b, (((1,), (1,)), ((), ())), preferred_element_type=F32)


def _tn(a, b):
    return lax.dot_general(a, b, (((0,), (0,)), ((), ())), preferred_element_type=F32)


def _mm(a, b):
    return jnp.dot(a, b, preferred_element_type=F32)


def _rms_rows(x, g):
    return x * lax.rsqrt(jnp.mean(x * x, axis=-1, keepdims=True) + EPS) * g


def _split_bf16(x):
    hi = x.astype(BF16)
    lo = (x - hi.astype(F32)).astype(BF16)
    return hi, lo


def _params(sem):
    return pltpu.CompilerParams(dimension_semantics=sem, vmem_limit_bytes=VMEM_LIMIT)


def _rotate_into(out_ref, z, c_ref, s_ref, shift, first, scale):
    for j in range(z.shape[1] // LANES):
        sl = slice(j * LANES, (j + 1) * LANES)
        zc = z[:, sl]
        up = pltpu.roll(zc, LANES - shift, axis=1)
        dn = pltpu.roll(zc, shift, axis=1)
        r = zc * c_ref[:, sl] + jnp.where(first, up, dn) * s_ref[:, sl]
        out_ref[:, sl] = r if scale is None else r * scale


def _pick_rows(n_ptiles, p_ref, s_ref):
    return jnp.where(pl.program_id(0) < n_ptiles, p_ref[...], s_ref[...])


def _split_rows(tm, width, n_ptiles):
    return (pl.BlockSpec((tm, width), lambda i: (jnp.minimum(i, n_ptiles - 1), 0)),
            pl.BlockSpec((tm, width), lambda i: (jnp.maximum(i - n_ptiles, 0), 0)))


def _in_proj_kernel(xp_ref, xs_ref, g_ref, w_ref, ca_ref, sa_ref, cr_ref, sr_ref,
                    qa_ref, ka_ref, va_ref, qr_ref, kr_ref, vr_ref, gr_ref, *, n_ptiles):
    xn = _rms_rows(_pick_rows(n_ptiles, xp_ref, xs_ref), g_ref[...]).astype(BF16)
    lane = lax.broadcasted_iota(jnp.int32, (1, LANES), 1)
    first_a = (lane % HD_A) < (ROT_DIM // 2)
    first_r = (lane % 2) == 0

    def sec(lo, width):
        return _mm(xn, w_ref[:, lo:lo + width])

    _rotate_into(qa_ref, sec(0, A_W), ca_ref, sa_ref, ROT_DIM // 2, first_a, None)
    _rotate_into(ka_ref, sec(A_W, A_W), ca_ref, sa_ref, ROT_DIM // 2, first_a, None)
    va_ref[...] = sec(2 * A_W, A_W)
    _rotate_into(qr_ref, sec(3 * A_W, R_QK), cr_ref, sr_ref, 1, first_r, None)
    _rotate_into(kr_ref, sec(3 * A_W + R_QK, R_QK), cr_ref, sr_ref, 1, first_r, DK_R ** -0.5)
    vr_ref[...] = sec(3 * A_W + 2 * R_QK, R_V)
    gr_ref[...] = sec(3 * A_W + 2 * R_QK + R_V, R_V)


def _rope_tables(pos):
    half = ROT_DIM // 2
    inv = ROPE_THETA ** (-jnp.arange(half, dtype=F32) / half)
    ang = pos.astype(F32)[:, None] * inv[None, :]
    cos, sin = jnp.cos(ang), jnp.sin(ang)
    rest = HD_A - ROT_DIM
    c = jnp.concatenate([cos, cos, jnp.ones((pos.shape[0], rest), F32)], axis=-1)
    s = jnp.concatenate([-sin, sin, jnp.zeros((pos.shape[0], rest), F32)], axis=-1)
    return jnp.tile(c, (1, H_A)), jnp.tile(s, (1, H_A))


def _retnet_tables(pos):
    n = DK_R // 2
    inv = 1.0 / (10000.0 ** jnp.linspace(0.0, 1.0, n, dtype=F32))
    ang = pos.astype(F32)[:, None] * inv[None, :]
    cos, sin = jnp.cos(ang), jnp.sin(ang)
    c = jnp.repeat(cos, 2, axis=-1)
    s = jnp.stack([-sin, sin], axis=-1).reshape(pos.shape[0], DK_R)
    return jnp.tile(c, (1, H_R)), jnp.tile(s, (1, H_R))


def _in_proj(x_p, x_s, g, w_bf16, seq, dec_seq, past_len):
    d = x_p.shape[1]
    n = x_p.shape[0] + x_s.shape[0]
    tm = ROW_TILE
    tiles_per_seq = seq // tm
    n_ptiles = x_p.shape[0] // tm
    pos = jnp.concatenate([jnp.arange(seq), past_len + (jnp.arange(tm) % dec_seq)])
    ca, sa = _rope_tables(pos)
    cr, sr = _retnet_tables(pos)

    def tab_map(i):
        return (jnp.where(i < n_ptiles, i % tiles_per_seq, tiles_per_seq), 0)

    row = lambda w: pl.BlockSpec((tm, w), lambda i: (i, 0))
    tab = pl.BlockSpec((tm, A_W), tab_map)
    widths = (A_W, A_W, A_W, R_QK, R_QK, R_V, R_V)
    return pl.pallas_call(
        functools.partial(_in_proj_kernel, n_ptiles=n_ptiles),
        grid=(n // tm,),
        in_specs=[*_split_rows(tm, d, n_ptiles), pl.BlockSpec((1, d), lambda i: (0, 0)),
                  pl.BlockSpec(w_bf16.shape, lambda i: (0, 0)), tab, tab, tab, tab],
        out_specs=[row(w) for w in widths],
        out_shape=[jax.ShapeDtypeStruct((n, w), F32) for w in widths],
        compiler_params=_params(("arbitrary",)),
        name="in_proj",
    )(x_p, x_s, g.reshape(1, d), w_bf16, ca, sa, cr, sr)


def _moba_select(q_f32, kmean, n_valid, eye):
    n_blk = kmean.shape[0]
    qh, ql = _split_bf16(q_f32)
    kh, kl = _split_bf16(kmean)
    st = _nt(jnp.concatenate([kh, kl, kh], axis=1), jnp.concatenate([qh, qh, ql], axis=1))
    jrow = lax.broadcasted_iota(jnp.int32, st.shape, 0)
    rank = jnp.zeros(st.shape, F32)
    for jp in range(n_blk):
        sj = st[jp:jp + 1, :]
        beats = (sj > st) | ((sj == st) & (jp < jrow))
        rank = rank + jnp.where(beats & (jp < n_valid), 1.0, 0.0)
    sel_t = jnp.where((jrow < n_valid) & (rank < MOBA_TOPK), 1.0, 0.0)
    sel_t = jnp.concatenate([sel_t, jnp.zeros((LANES - n_blk, st.shape[1]), F32)], axis=0).astype(BF16)
    return _nt(eye, sel_t)


def _moba_prompt_kernel(q_ref, k_ref, v_ref, o_ref, kb_ref, vb_ref, km_ref, s_ref):
    b = pl.program_id(2)
    blk = MOBA_BLOCK
    n_blk = k_ref.shape[0] // blk
    half = blk // 2

    @pl.when(b == 0)
    def _():
        kb_ref[...] = k_ref[...].astype(BF16)
        vb_ref[...] = v_ref[...].astype(BF16)
        for j in range(n_blk):
            km_ref[j:j + 1, :] = jnp.mean(k_ref[j * blk:(j + 1) * blk, :], axis=0, keepdims=True)

    q = q_ref[...]
    lane = lax.broadcasted_iota(jnp.int32, (1, LANES), 1)
    r_i = lax.broadcasted_iota(jnp.int32, (blk, blk), 0)
    c_i = lax.broadcasted_iota(jnp.int32, (blk, blk), 1)
    eye = jnp.where(r_i == c_i, 1.0, 0.0).astype(BF16)
    own = pl.ds(pl.multiple_of(b * blk, blk), blk)
    heads = range(LANES // HD_A)
    fold = lambda x: (x[:, :half], x[:, half:])
    hms, q_aug, s_own, mx = [], [], [], []
    for hh in heads:
        hm = (lane // HD_A) == hh
        qh = jnp.where(hm, q, 0.0)
        selq = _moba_select(qh, km_ref[...], b, eye)
        bias = jnp.where(selq > 0.5, 0.0, NEG_INF).astype(BF16)
        qs = (qh * (HD_A ** -0.5)).astype(BF16)
        so = jnp.where(c_i <= r_i, _nt(qs, kb_ref[own, :]), NEG_INF)
        hms.append(hm)
        q_aug.append(jnp.concatenate([qs, bias], axis=1))
        s_own.append(so)
        mx.append(jnp.maximum(*fold(so)))

    def rows_of(j):
        return pl.ds(pl.multiple_of(j * blk, blk), blk)

    def scores(j, mx):
        onehot = jnp.broadcast_to(jnp.where(lane == j, 1.0, 0.0).astype(BF16), (blk, LANES))
        k_aug = jnp.concatenate([kb_ref[rows_of(j), :], onehot], axis=1)
        out = []
        for hh in heads:
            s = _nt(q_aug[hh], k_aug)
            s_ref[hh, j] = s
            out.append(jnp.maximum(mx[hh], jnp.maximum(*fold(s))))
        return tuple(out)

    mx = lax.fori_loop(0, b, scores, tuple(mx))
    ms = [jnp.max(mx[hh], axis=1, keepdims=True) for hh in heads]

    def weights(s, hh, vj):
        p = jnp.exp(s - ms[hh])
        lo, hi = fold(p)
        return lo + hi, _mm(p.astype(BF16), vj)

    def accumulate(j, carry):
        vj = vb_ref[rows_of(j), :]
        out = []
        for hh in heads:
            l_part, pv = weights(s_ref[hh, j], hh, vj)
            out.append((carry[hh][0] + l_part, carry[hh][1] + pv))
        return tuple(out)

    carry = lax.fori_loop(0, b, accumulate, tuple(weights(s_own[hh], hh, vb_ref[own, :]) for hh in heads))
    out = jnp.zeros(q.shape, F32)
    for hh in heads:
        l_part, acc = carry[hh]
        out = out + jnp.where(hms[hh], acc / jnp.sum(l_part, axis=1, keepdims=True), 0.0)
    o_ref[...] = out


def _moba_prompt(q_a, k_a, v_a, n_seq, seq):
    blk = MOBA_BLOCK
    n_blk = seq // blk
    qspec = pl.BlockSpec((blk, LANES), lambda s, h, b: (s * n_blk + b, h))
    kspec = pl.BlockSpec((seq, LANES), lambda s, h, b: (s, h))
    return pl.pallas_call(
        _moba_prompt_kernel,
        grid=(n_seq, A_W // LANES, n_blk),
        in_specs=[qspec, kspec, kspec],
        out_specs=qspec,
        out_shape=jax.ShapeDtypeStruct((n_seq * seq, A_W), F32),
        scratch_shapes=[pltpu.VMEM((seq, LANES), BF16), pltpu.VMEM((seq, LANES), BF16),
                        pltpu.VMEM((n_blk, LANES), F32), pltpu.VMEM((LANES // HD_A, n_blk, blk, blk), F32)],
        compiler_params=_params(("arbitrary", "arbitrary", "arbitrary")),
        name="moba_prompt",
    )(q_a, k_a, v_a)


def _ret_decay_tables(c):
    log_g = jnp.log(1.0 - 2.0 ** (-5.0 - jnp.arange(H_R, dtype=F32)))
    i = jnp.arange(c, dtype=F32)
    diff = i[:, None] - i[None, :]
    dmat = jnp.where(diff >= 0, jnp.exp(jnp.maximum(diff, 0.0)[None] * log_g[:, None, None]), 0.0)
    dq = jnp.exp((i + 1.0)[None, :] * log_g[:, None])
    dk = jnp.exp((c - 1.0 - i)[None, :] * log_g[:, None])
    dc = jnp.exp(c * log_g)
    return dmat, dq, dk, dc


def _ret_chunk(q, k, v, g, s, dmat, dq, dk, dc):
    att = _nt(q.astype(BF16), k.astype(BF16)) * dmat
    o = _mm(att.astype(BF16), v.astype(BF16)) + _mm((q * dq).astype(BF16), s.astype(BF16))
    s = s * dc + _tn((k * dk).astype(BF16), v.astype(BF16))
    o = o * lax.rsqrt(jnp.mean(o * o, axis=-1, keepdims=True) + EPS)
    return o * (g * jax.nn.sigmoid(g)), s


def _ret_prompt_kernel(q_ref, k_ref, v_ref, g_ref, dmat_ref, dq_ref, dk_ref, dc_ref, o_ref, st_ref):
    c = RET_CHUNK
    per = LANES // DK_R
    lane = lax.broadcasted_iota(jnp.int32, (1, LANES), 1)

    def chunk(i, states):
        rows = pl.ds(pl.multiple_of(i * c, c), c)
        q_all, k_all = q_ref[rows, :], k_ref[rows, :]
        out = []
        for hh in range(per):
            hm = (lane // DK_R) == hh
            vl = slice(hh * DV_R, (hh + 1) * DV_R)
            o, s = _ret_chunk(jnp.where(hm, q_all, 0.0), jnp.where(hm, k_all, 0.0), v_ref[rows, vl], g_ref[rows, vl],
                              states[hh], dmat_ref[hh], dq_ref[hh], dk_ref[hh], dc_ref[hh])
            o_ref[rows, vl] = o
            out.append(s)
        return tuple(out)

    states = lax.fori_loop(0, q_ref.shape[0] // c, chunk, tuple(jnp.zeros((LANES, DV_R), F32) for _ in range(per)))
    for hh in range(per):
        st_ref[hh] = states[hh][hh * DK_R:(hh + 1) * DK_R, :]


def _ret_tables_bcast(c, rows):
    dmat, dq, dk, dc = _ret_decay_tables(c)
    pad = rows - c
    dmat = jnp.pad(dmat, ((0, 0), (0, pad), (0, pad)))
    dq = jnp.broadcast_to(jnp.pad(dq, ((0, 0), (0, pad)))[:, :, None], (H_R, rows, LANES))
    dk = jnp.broadcast_to(jnp.pad(dk, ((0, 0), (0, pad)))[:, :, None], (H_R, rows, LANES))
    dc = jnp.broadcast_to(dc[:, None, None], (H_R, 1, LANES))
    return dmat, dq, dk, dc


def _ret_prompt(q_r, k_r, v_r, g_r, n_seq, seq):
    c = RET_CHUNK
    dmat, dq, dk, dc = _ret_tables_bcast(c, c)
    per = LANES // DK_R
    qk = pl.BlockSpec((seq, LANES), lambda s, h: (s, h))
    vg = pl.BlockSpec((seq, per * DV_R), lambda s, h: (s, h))
    tab = lambda r: pl.BlockSpec((per, r, LANES), lambda s, h: (h, 0, 0))
    return pl.pallas_call(
        _ret_prompt_kernel,
        grid=(n_seq, H_R // per),
        in_specs=[qk, qk, vg, vg, tab(c), tab(c), tab(c), tab(1)],
        out_specs=[vg, pl.BlockSpec((None, per, DK_R, DV_R), lambda s, h: (s, h, 0, 0))],
        out_shape=[jax.ShapeDtypeStruct((n_seq * seq, R_V), F32),
                   jax.ShapeDtypeStruct((n_seq, H_R, DK_R, DV_R), F32)],
        compiler_params=_params(("arbitrary", "arbitrary")),
        name="ret_prompt",
    )(q_r, k_r, v_r, g_r, dmat, dq, dk, dc)


def _pad_rows(x, rows):
    return jnp.concatenate([x, jnp.zeros((rows - x.shape[0], x.shape[1]), x.dtype)], axis=0)


def _moba_sample_kernel(pt_ref, q_ref, kn_ref, vn_ref, *rest, n_pages):
    k_refs, v_refs = rest[:n_pages], rest[n_pages:2 * n_pages]
    o_ref, s_ref = rest[2 * n_pages], rest[2 * n_pages + 1]
    q = q_ref[...]
    t = q.shape[0]
    page = k_refs[0].shape[-1]
    per_blk = MOBA_BLOCK // page
    n_blk = n_pages // per_blk
    lane = lax.broadcasted_iota(jnp.int32, (1, A_W), 1)
    qbd = jnp.concatenate([jnp.where((lane // HD_A) == h, q, 0.0) for h in range(H_A)], axis=0)
    qs = (qbd * (HD_A ** -0.5)).astype(BF16)
    rows = qbd.shape[0]

    bsum = [jnp.zeros((rows, 1), F32) for _ in range(n_blk)]
    for p in range(n_pages):
        sp = _mm(qs, k_refs[p][...].reshape(A_W, page).astype(BF16))
        s_ref[:, p * page:(p + 1) * page] = sp
        bsum[p // per_blk] = bsum[p // per_blk] + jnp.sum(sp, axis=1, keepdims=True)
    sel = []
    for j in range(n_blk):
        rank = jnp.zeros((rows, 1), F32)
        for jp in range(n_blk):
            if jp != j:
                beats = (bsum[jp] > bsum[j]) | (bsum[jp] == bsum[j]) if jp < j else bsum[jp] > bsum[j]
                rank = rank + jnp.where(beats, 1.0, 0.0)
        sel.append(rank < MOBA_TOPK)

    qi = lax.broadcasted_iota(jnp.int32, (rows, LANES), 0) % t
    causal = lax.broadcasted_iota(jnp.int32, (rows, LANES), 1) <= qi
    s0 = jnp.where(causal, _nt(qs, _pad_rows(kn_ref[...], LANES).astype(BF16)), NEG_INF)
    m = jnp.max(s0, axis=1, keepdims=True)
    p0 = jnp.exp(s0 - m)
    carry = (m, jnp.sum(p0, axis=1, keepdims=True), _mm(p0.astype(BF16), _pad_rows(vn_ref[...], LANES).astype(BF16)))
    for p in range(n_pages):
        m, l, acc = carry
        s = jnp.where(sel[p // per_blk], s_ref[:, p * page:(p + 1) * page], NEG_INF)
        m_new = jnp.maximum(m, jnp.max(s, axis=1, keepdims=True))
        alpha = jnp.exp(m - m_new)
        pr = jnp.exp(s - m_new)
        carry = (m_new, alpha * l + jnp.sum(pr, axis=1, keepdims=True),
                 alpha * acc + _nt(pr.astype(BF16), v_refs[p][...].reshape(A_W, page).astype(BF16)))
    _, l, acc = carry
    o = acc / l
    out = jnp.zeros((t, A_W), F32)
    for h in range(H_A):
        out = out + jnp.where((lane // HD_A) == h, o[h * t:(h + 1) * t, :], 0.0)
    o_ref[...] = out


def _moba_sample(q_a, k_a, v_a, cache_kt, cache_vt, page_table, n_prompt_rows, dec_seq):
    n_s, n_pages = page_table.shape
    base = n_prompt_rows // dec_seq
    new = pl.BlockSpec((dec_seq, A_W), lambda n, pt: (base + n, 0))
    pages = [pl.BlockSpec((None,) + cache_kt.shape[1:], lambda n, pt, p=p: (pt[n * n_pages + p], 0, 0, 0))
             for p in range(n_pages)]
    grid_spec = pltpu.PrefetchScalarGridSpec(
        num_scalar_prefetch=1,
        grid=(n_s,),
        in_specs=[new, new, new] + pages + pages,
        out_specs=pl.BlockSpec((dec_seq, A_W), lambda n, pt: (n, 0)),
        scratch_shapes=[pltpu.VMEM((H_A * dec_seq, n_pages * cache_kt.shape[-1]), F32)],
    )
    return pl.pallas_call(
        functools.partial(_moba_sample_kernel, n_pages=n_pages),
        grid_spec=grid_spec,
        out_shape=jax.ShapeDtypeStruct((n_s * dec_seq, A_W), F32),
        compiler_params=_params(("arbitrary",)),
        name="moba_sample",
    )(page_table.reshape(-1), q_a, k_a, v_a, *([cache_kt] * n_pages), *([cache_vt] * n_pages))


def _ret_sample_kernel(q_ref, k_ref, v_ref, g_ref, s0_ref, dmat_ref, dq_ref, dk_ref, dc_ref, o_ref, st_ref):
    t = q_ref.shape[0]
    lane = lax.broadcasted_iota(jnp.int32, (1, LANES), 1)
    per = LANES // DK_R
    zero_half = jnp.zeros((DK_R, DV_R), F32)
    for h in range(H_R):
        hh = h % per
        hm = (lane // DK_R) == hh
        qk_l = slice((h // per) * LANES, (h // per + 1) * LANES)
        v_l = slice(h * DV_R, (h + 1) * DV_R)
        q = _pad_rows(jnp.where(hm, q_ref[:, qk_l], 0.0), LANES)
        k = _pad_rows(jnp.where(hm, k_ref[:, qk_l], 0.0), LANES)
        v = _pad_rows(v_ref[:, v_l], LANES)
        g = _pad_rows(g_ref[:, v_l], LANES)
        halves = [zero_half] * per
        halves[hh] = s0_ref[h]
        o, s = _ret_chunk(q, k, v, g, jnp.concatenate(halves, axis=0),
                          dmat_ref[h], dq_ref[h], dk_ref[h], dc_ref[h])
        o_ref[:, v_l] = o[:t, :]
        st_ref[h] = s[hh * DK_R:(hh + 1) * DK_R, :]


def _ret_sample(q_r, k_r, v_r, g_r, s0, n_prompt_rows, dec_seq):
    n_s = s0.shape[0]
    base = n_prompt_rows // dec_seq
    dmat, dq, dk, dc = _ret_tables_bcast(dec_seq, LANES)
    qk = pl.BlockSpec((dec_seq, R_QK), lambda n: (base + n, 0))
    vg = pl.BlockSpec((dec_seq, R_V), lambda n: (base + n, 0))
    st = pl.BlockSpec((None, H_R, DK_R, DV_R), lambda n: (n, 0, 0, 0))
    full = lambda a: pl.BlockSpec(a.shape, lambda n: (0,) * a.ndim)
    return pl.pallas_call(
        _ret_sample_kernel,
        grid=(n_s,),
        in_specs=[qk, qk, vg, vg, st, full(dmat), full(dq), full(dk), full(dc)],
        out_specs=[pl.BlockSpec((dec_seq, R_V), lambda n: (n, 0)), st],
        out_shape=[jax.ShapeDtypeStruct((n_s * dec_seq, R_V), F32), jax.ShapeDtypeStruct(s0.shape, F32)],
        compiler_params=_params(("arbitrary",)),
        name="ret_sample",
    )(q_r, k_r, v_r, g_r, s0, dmat, dq, dk, dc)


ROUTER_ROWS = SUBLANES + N_EXPERTS


def _route(xn, wr_ref, br_ref, eid_ref, we_ref):
    xh, xl = _split_bf16(xn)
    lt = _nt(wr_ref[...], jnp.concatenate([xh, xh, xl], axis=1)) + br_ref[...]
    tm = lt.shape[1]
    r8 = lax.broadcasted_iota(jnp.int32, (SUBLANES, tm), 0)
    lg = jnp.where(r8 < MOE_GROUPS, lt[:SUBLANES, :], NEG_INF)
    mg = jnp.max(lg, axis=0, keepdims=True)
    wg = 1.0 / jnp.sum(jnp.exp(lg - mg), axis=0, keepdims=True)
    gidx = jnp.min(jnp.where(lg == mg, r8, SUBLANES), axis=0, keepdims=True)
    le = jnp.zeros((MOE_PER_GROUP, tm), F32)
    for gi in range(MOE_GROUPS):
        lo = SUBLANES + gi * MOE_PER_GROUP
        le = le + jnp.where(gidx == gi, lt[lo:lo + MOE_PER_GROUP, :], 0.0)
    v1 = jnp.max(le, axis=0, keepdims=True)
    i1 = jnp.min(jnp.where(le == v1, r8, MOE_PER_GROUP), axis=0, keepdims=True)
    le2 = jnp.where(r8 == i1, -jnp.inf, le)
    v2 = jnp.max(le2, axis=0, keepdims=True)
    i2 = jnp.min(jnp.where(le2 == v2, r8, MOE_PER_GROUP), axis=0, keepdims=True)
    e21 = jnp.exp(v2 - v1)
    w1 = wg / (1.0 + e21)
    eid_ref[0:1, :] = gidx * MOE_PER_GROUP + i1
    eid_ref[1:2, :] = gidx * MOE_PER_GROUP + i2
    we_ref[0:1, :] = w1
    we_ref[1:2, :] = w1 * e21


def _proj_router_kernel(*refs, n_in, glu, n_ptiles):
    a_refs, w_refs = refs[:2 * n_in], refs[2 * n_in:3 * n_in]
    k = 3 * n_in
    b_ref = refs[k] if glu else None
    k += int(glu)
    hp_ref, hs_ref, g_ref, wr_ref, br_ref, ho_ref, xn_ref, eid_ref, we_ref = refs[k:k + 9]
    acc = None
    for j, w_ref in enumerate(w_refs):
        a = _pick_rows(n_ptiles, a_refs[2 * j], a_refs[2 * j + 1])
        part = _mm(a.astype(BF16), w_ref[...])
        acc = part if acc is None else acc + part
    if glu:
        acc = acc + b_ref[...]
        half = acc.shape[1] // 2
        acc = acc[:, :half] * jax.nn.sigmoid(acc[:, half:])
    hn = _pick_rows(n_ptiles, hp_ref, hs_ref) + acc
    ho_ref[...] = hn
    xn = _rms_rows(hn, g_ref[...])
    xn_ref[...] = xn
    _route(xn, wr_ref, br_ref, eid_ref, we_ref)


def _router_weights(w_group, b_group, w_expert, b_expert):
    d = w_group.shape[0]
    wt = jnp.concatenate([w_group.T, jnp.zeros((SUBLANES - MOE_GROUPS, d), F32), w_expert.T], axis=0)
    hi = wt.astype(BF16)
    lo = (wt - hi.astype(F32)).astype(BF16)
    bias = jnp.concatenate([b_group, jnp.zeros((SUBLANES - MOE_GROUPS,), F32), b_expert]).reshape(-1, 1)
    return jnp.concatenate([hi, lo, hi], axis=1), bias


def _proj_router(acts, weights, bias, h, g, router, glu):
    d = h[0].shape[1]
    n = h[0].shape[0] + h[1].shape[0]
    tm = ROW_TILE
    n_ptiles = h[0].shape[0] // tm
    wr, br = router
    row = lambda w: pl.BlockSpec((tm, w), lambda i: (i, 0))
    full = lambda a: pl.BlockSpec(a.shape, lambda i: (0,) * a.ndim)
    tok = pl.BlockSpec((MOE_TOPK, tm), lambda i: (0, i))
    operands = ([a for pair in acts for a in pair] + list(weights) + ([bias.reshape(1, -1)] if glu else [])
                + [h[0], h[1], g.reshape(1, d), wr, br])
    in_specs = ([spec for pair in acts for spec in _split_rows(tm, pair[0].shape[1], n_ptiles)]
                + [full(w) for w in weights]
                + ([pl.BlockSpec((1, bias.shape[0]), lambda i: (0, 0))] if glu else [])
                + [*_split_rows(tm, d, n_ptiles), pl.BlockSpec((1, d), lambda i: (0, 0)), full(wr), full(br)])
    return pl.pallas_call(
        functools.partial(_proj_router_kernel, n_in=len(acts), glu=glu, n_ptiles=n_ptiles),
        grid=(n // tm,),
        in_specs=in_specs,
        out_specs=[row(d), row(d), tok, tok],
        out_shape=[jax.ShapeDtypeStruct((n, d), F32), jax.ShapeDtypeStruct((n, d), F32),
                   jax.ShapeDtypeStruct((MOE_TOPK, n), jnp.int32), jax.ShapeDtypeStruct((MOE_TOPK, n), F32)],
        compiler_params=_params(("arbitrary",)),
        name="glu_router" if glu else "out_proj_router",
    )(*operands)


def _moe_plan(eid, we):
    n_tok = eid.shape[1]
    n_assign = MOE_TOPK * n_tok
    blk = MOE_ROWS
    n_blocks = -(-n_assign // blk) + N_EXPERTS
    e_flat = eid.reshape(-1)
    experts = jnp.arange(N_EXPERTS, dtype=jnp.int32)
    counts = jnp.sum((e_flat[:, None] == experts[None, :]).astype(jnp.int32), axis=0)
    order = jnp.argsort(e_flat).astype(jnp.int32)
    pc = (counts + blk - 1) // blk * blk
    pend = jnp.cumsum(pc)
    pstart = pend - pc
    start = jnp.cumsum(counts) - counts
    first_row = jnp.arange(n_blocks, dtype=jnp.int32) * blk
    blk_e = jnp.minimum(jnp.sum((pend[None, :] <= first_row[:, None]).astype(jnp.int32), axis=1), N_EXPERTS - 1)
    n_valid = jnp.clip(counts[blk_e] - (first_row - pstart[blk_e]), 0, blk).astype(jnp.int32)
    row = first_row[:, None] + jnp.arange(blk, dtype=jnp.int32)[None, :]
    off = row - pstart[blk_e][:, None]
    valid = off < counts[blk_e][:, None]
    a_row = order[jnp.clip(start[blk_e][:, None] + off, 0, n_assign - 1)]
    a_row = jnp.where(valid, a_row, 0).reshape(-1)
    w_row = jnp.where(valid.reshape(-1), we.reshape(-1)[a_row], 0.0)
    n_used = (pend[-1] // blk).astype(jnp.int32).reshape(1)
    return blk_e, n_used, n_valid, a_row % n_tok, a_row, w_row.reshape(-1, 1), n_blocks


def _moe_kernel(blk_e_ref, n_used_ref, n_valid_ref, src_ref, dst_ref, x_hbm, w13_ref, w2_ref, wrow_ref, y_hbm,
                xbuf, ybuf, w13b, w2b, gsem, ssem):
    b = pl.program_id(0)
    n_used = n_used_ref[0]
    blk = MOE_ROWS
    slot = b % 2

    def gather_row(block, slot, r):
        return pltpu.make_async_copy(x_hbm.at[pl.ds(src_ref[block * blk + r], 1), :],
                                     xbuf.at[slot, pl.ds(r, 1), :], gsem.at[slot])

    def scatter_row(block, slot, r):
        return pltpu.make_async_copy(ybuf.at[slot, pl.ds(r, 1), :],
                                     y_hbm.at[pl.ds(dst_ref[block * blk + r], 1), :], ssem.at[slot])

    def gather_all(slot):
        return pltpu.make_async_copy(x_hbm.at[pl.ds(0, blk), :], xbuf.at[slot], gsem.at[slot])

    def scatter_all(slot):
        return pltpu.make_async_copy(ybuf.at[slot], y_hbm.at[pl.ds(0, blk), :], ssem.at[slot])

    def start_rows(row_copy, block, slot):
        nv = n_valid_ref[block]

        def body(r, c):
            row_copy(block, slot, r).start()
            return c

        def body8(i, c):
            for k in range(MOE_DMA_UNROLL):
                row_copy(block, slot, i * MOE_DMA_UNROLL + k).start(priority=k % 2)
            return c

        @pl.when(nv == blk)
        def _():
            lax.fori_loop(0, blk // MOE_DMA_UNROLL, body8, 0)

        @pl.when(nv < blk)
        def _():
            lax.fori_loop(0, nv, body, 0)

    def wait_rows(row_copy, all_copy, block, slot):
        nv = n_valid_ref[block]

        @pl.when(nv == blk)
        def _():
            all_copy(slot).wait()

        @pl.when(nv < blk)
        def _():
            def body(r, c):
                row_copy(block, slot, r).wait()
                return c
            lax.fori_loop(0, nv, body, 0)

    @pl.when((b == 0) & (n_used > 0))
    def _():
        xbuf[...] = jnp.zeros(xbuf.shape, xbuf.dtype)
        start_rows(gather_row, 0, 0)

    @pl.when(b + 1 < n_used)
    def _():
        start_rows(gather_row, b + 1, 1 - slot)

    @pl.when(b < n_used)
    def _():
        wait_rows(gather_row, gather_all, b, slot)

        @pl.when(b >= 2)
        def _():
            wait_rows(scatter_row, scatter_all, b - 2, slot)

        @pl.when((b == 0) | (blk_e_ref[b] != blk_e_ref[jnp.maximum(b - 1, 0)]))
        def _():
            w13b[...] = w13_ref[...].astype(BF16)
            w2b[...] = w2_ref[...].astype(BF16)

        hb = _mm(xbuf[slot].astype(BF16), w13b[...])
        half = hb.shape[1] // 2
        gate = hb[:, :half]
        act = gate * jax.nn.sigmoid(gate) * hb[:, half:]
        ybuf[slot] = _mm(act.astype(BF16), w2b[...]) * wrow_ref[...]
        start_rows(scatter_row, b, slot)

    @pl.when(b == pl.num_programs(0) - 1)
    def _():
        for back in (2, 1):
            last = n_used - back

            @pl.when(last >= 0)
            def _():
                wait_rows(scatter_row, scatter_all, last, last % 2)


def _moe_experts(xn, eid, we, w13, w2, layer):
    n_tok, d = xn.shape
    blk = MOE_ROWS
    blk_e, n_used, n_valid, src_tok, dst_row, w_row, n_blocks = _moe_plan(eid, we)
    grid_spec = pltpu.PrefetchScalarGridSpec(
        num_scalar_prefetch=5,
        grid=(n_blocks,),
        in_specs=[pl.BlockSpec(memory_space=pl.ANY),
                  pl.BlockSpec((None, None) + w13.shape[2:], lambda b, be, *_: (layer, be[b], 0, 0)),
                  pl.BlockSpec((None, None) + w2.shape[2:], lambda b, be, *_: (layer, be[b], 0, 0)),
                  pl.BlockSpec((blk, 1), lambda b, *_: (b, 0))],
        out_specs=pl.BlockSpec(memory_space=pl.ANY),
        scratch_shapes=[pltpu.VMEM((2, blk, d), F32), pltpu.VMEM((2, blk, d), F32),
                        pltpu.VMEM(w13.shape[2:], BF16), pltpu.VMEM(w2.shape[2:], BF16),
                        pltpu.SemaphoreType.DMA((2,)), pltpu.SemaphoreType.DMA((2,))],
    )
    return pl.pallas_call(
        _moe_kernel,
        grid_spec=grid_spec,
        out_shape=jax.ShapeDtypeStruct((MOE_TOPK * n_tok, d), F32),
        compiler_params=_params(("arbitrary",)),
        name="moe_experts",
    )(blk_e, n_used, n_valid, src_tok, dst_row, xn, w13, w2, w_row)


def _combine_kernel(h_ref, y0_ref, y1_ref, g_ref, *out_refs, norm):
    h = h_ref[...] + (y0_ref[...] + y1_ref[...])
    out_refs[0][...] = _rms_rows(h, g_ref[...]) if norm else h


def _moe_combine(h, y, g, norm, row_lo, n_rows):
    n, d = h.shape
    tm = ROW_TILE
    lo = row_lo // tm
    return pl.pallas_call(
        functools.partial(_combine_kernel, norm=norm),
        grid=(n_rows // tm,),
        in_specs=[pl.BlockSpec((tm, d), lambda i: (lo + i, 0)), pl.BlockSpec((tm, d), lambda i: (lo + i, 0)),
                  pl.BlockSpec((tm, d), lambda i: (n // tm + lo + i, 0)), pl.BlockSpec((1, d), lambda i: (0, 0))],
        out_specs=pl.BlockSpec((tm, d), lambda i: (i, 0)),
        out_shape=jax.ShapeDtypeStruct((n_rows, d), F32),
        compiler_params=_params(("arbitrary",)),
        name="moe_combine_norm" if norm else "moe_combine",
    )(h, y, y, g.reshape(1, d))


S5_CH = LANES
S5_ST = (LANES // S5_GROUP) * S5_STATE
S5_SCAN_LANES = 512


def _s5_kernel(h_ref, g_ref, bre_ref, bim_ref, are_ref, aim_ref, cre_ref, cim_ref, d_ref, s0r_ref, s0i_ref,
               zg_ref, fr_ref, fi_ref, xr, xi, sr, si, *, s, steps):
    c = pl.program_id(1)

    @pl.when(c == 0)
    def _():
        sr[...] = s0r_ref[...]
        si[...] = s0i_ref[...]

    u = _rms_rows(h_ref[...], g_ref[...])
    ub = u.astype(BF16)
    n_ch = u.shape[1] // S5_CH
    for j in range(n_ch):
        uj = ub[:, j * S5_CH:(j + 1) * S5_CH]
        xr[:, j * S5_ST:(j + 1) * S5_ST] = _mm(uj, bre_ref[j])
        xi[:, j * S5_ST:(j + 1) * S5_ST] = _mm(uj, bim_ref[j])

    for lo in range(0, xr.shape[1], S5_SCAN_LANES):
        ls = slice(lo, lo + S5_SCAN_LANES)
        ar = jnp.broadcast_to(are_ref[:, ls], (s, S5_SCAN_LANES))
        ai = jnp.broadcast_to(aim_ref[:, ls], (s, S5_SCAN_LANES))

        def step(t, carry):
            pr, pi = carry
            rows = pl.ds(pl.multiple_of(t * s, s), s)
            nr = ar * pr - ai * pi + xr[rows, ls]
            ni = ar * pi + ai * pr + xi[rows, ls]
            xr[rows, ls] = nr
            xi[rows, ls] = ni
            return nr, ni

        pr, pi = lax.fori_loop(0, steps, step, (sr[:, ls], si[:, ls]))
        sr[:, ls] = pr
        si[:, ls] = pi

    for j in range(n_ch):
        cs = slice(j * S5_CH, (j + 1) * S5_CH)
        ss = slice(j * S5_ST, (j + 1) * S5_ST)
        y = _mm(xr[:, ss].astype(BF16), cre_ref[j]) - _mm(xi[:, ss].astype(BF16), cim_ref[j])
        zg_ref[:, cs] = jax.nn.gelu(y + d_ref[:, cs] * u[:, cs])

    @pl.when(c == pl.num_programs(1) - 1)
    def _():
        fr_ref[...] = sr[...]
        fi_ref[...] = si[...]


def _s5_weights(lam_re, lam_im, log_dt, b_re, b_im, c_re, c_im):
    dt = jnp.exp(log_dt)[:, None]
    mag = jnp.exp(lam_re * dt)
    ang = lam_im * dt
    ab_re = mag * jnp.cos(ang)
    ab_im = mag * jnp.sin(ang)
    den = lam_re * lam_re + lam_im * lam_im
    nr = ab_re - 1.0
    co_re = (nr * lam_re + ab_im * lam_im) / den
    co_im = (ab_im * lam_re - nr * lam_im) / den
    bb_re = co_re[..., None] * b_re - co_im[..., None] * b_im
    bb_im = co_re[..., None] * b_im + co_im[..., None] * b_re
    per = S5_CH // S5_GROUP
    n_ch = lam_re.shape[0] // per
    eye = jnp.eye(per, dtype=F32)

    def in_blocks(bb):
        w = bb.transpose(0, 2, 1).reshape(n_ch, per, S5_GROUP, S5_STATE)
        return jnp.einsum('jgcp,gh->jgchp', w, eye).reshape(n_ch, S5_CH, S5_ST).astype(BF16)

    def out_blocks(cc):
        w = cc.transpose(0, 2, 1).reshape(n_ch, per, S5_STATE, S5_GROUP)
        return jnp.einsum('jgpc,gh->jgphc', w, eye).reshape(n_ch, S5_ST, S5_CH).astype(BF16)

    return (in_blocks(bb_re), in_blocks(bb_im), ab_re.reshape(1, -1), ab_im.reshape(1, -1),
            out_blocks(c_re), out_blocks(c_im))


def _s5_scan(h, g, weights, d_skip, s0_re, s0_im, s, steps):
    n, d = h.shape
    n_groups = s0_re.shape[0] // s
    rows = steps * s
    chunks = n // (n_groups * rows)
    bre, bim, are, aim, cre, cim = weights
    n_state = are.shape[1]
    full = lambda a: pl.BlockSpec(a.shape, lambda gi, c: (0,) * a.ndim)
    row = pl.BlockSpec((rows, d), lambda gi, c: (gi * chunks + c, 0))
    state = pl.BlockSpec((s, n_state), lambda gi, c: (gi, 0))
    return pl.pallas_call(
        functools.partial(_s5_kernel, s=s, steps=steps),
        grid=(n_groups, chunks),
        in_specs=[row, pl.BlockSpec((1, d), lambda gi, c: (0, 0)), full(bre), full(bim), full(are), full(aim),
                  full(cre), full(cim), pl.BlockSpec((1, d), lambda gi, c: (0, 0)), state, state],
        out_specs=[row, state, state],
        out_shape=[jax.ShapeDtypeStruct((n, d), F32), jax.ShapeDtypeStruct(s0_re.shape, F32),
                   jax.ShapeDtypeStruct(s0_re.shape, F32)],
        scratch_shapes=[pltpu.VMEM((rows, n_state), F32), pltpu.VMEM((rows, n_state), F32),
                        pltpu.VMEM((s, n_state), F32), pltpu.VMEM((s, n_state), F32)],
        compiler_params=_params(("arbitrary", "arbitrary")),
        name="s5_scan",
    )(h, g.reshape(1, d), bre, bim, are, aim, cre, cim, d_skip.reshape(1, d), s0_re, s0_im)


S5_PROMPT_STEPS = 32
S5_SAMPLE_SEQS = 32


def kernel(x_prompt, x_sample, cache_k, cache_v, state_ret, state_s5_re, state_s5_im, page_table, norm1, norm2, norm_f, w_in_even, w_out_even, s5_lam_re, s5_lam_im, s5_log_dt, s5_b_re, s5_b_im, s5_c_re, s5_c_im, s5_d, s5_w_glu, s5_b_glu, moe_w_group, moe_b_group, moe_w_expert, moe_b_expert, moe_w13, moe_w2):
    n_p, seq, d = x_prompt.shape
    n_s, dec_seq, _ = x_sample.shape
    pool, page = cache_k.shape[1], cache_k.shape[2]
    past_len = page_table.shape[1] * page
    np_rows, ns_rows = n_p * seq, n_s * dec_seq
    routers = [_router_weights(moe_w_group[li], moe_b_group[li], moe_w_expert[li], moe_b_expert[li])
               for li in range(2)]

    def moe(li, xn, eid, we):
        return _moe_experts(xn, eid, we, moe_w13, moe_w2, li)

    xp, xs = x_prompt.reshape(np_rows, d), x_sample.reshape(ns_rows, d)
    qa, ka, va, qr, kr, vr, gr = _in_proj(xp, xs, norm1[0], w_in_even[0].astype(BF16), seq, dec_seq, past_len)
    oa_p = _moba_prompt(qa, ka, va, n_p, seq)
    or_p, ret_p = _ret_prompt(qr, kr, vr, gr, n_p, seq)
    heads = lambda a: a[np_rows:].reshape(n_s, dec_seq, H_A, HD_A)
    k_s, v_s = heads(ka), heads(va)
    oa_s = _moba_sample(qa, ka, va, cache_k[0].transpose(0, 2, 3, 1), cache_v[0].transpose(0, 2, 3, 1),
                        page_table, np_rows, dec_seq)
    or_s, ret_s = _ret_sample(qr, kr, vr, gr, state_ret[0], np_rows, dec_seq)
    w_out = w_out_even[0].astype(BF16)
    h, xn, eid, we = _proj_router([(oa_p, oa_s), (or_p, or_s)], [w_out[:A_W], w_out[A_W:]], None, (xp, xs),
                                  norm2[0], routers[0], False)
    y = moe(0, xn, eid, we)
    hp = _moe_combine(h, y, norm2[0], False, 0, np_rows)
    hs = _moe_combine(h, y, norm2[0], False, np_rows, ns_rows)

    sg = S5_SAMPLE_SEQS
    hp = hp.reshape(n_p, seq, d).transpose(1, 0, 2).reshape(np_rows, d)
    hs = hs.reshape(n_s // sg, sg, dec_seq, d).transpose(0, 2, 1, 3).reshape(ns_rows, d)
    s5w = _s5_weights(s5_lam_re[0], s5_lam_im[0], s5_log_dt[0], s5_b_re[0], s5_b_im[0], s5_c_re[0], s5_c_im[0])
    n_state = s5_lam_re.shape[1] * s5_lam_re.shape[2]
    zeros = jnp.zeros((n_p, n_state), F32)
    zg_p, s5r_p, s5i_p = _s5_scan(hp, norm1[1], s5w, s5_d[0], zeros, zeros, n_p, S5_PROMPT_STEPS)
    zg_s, s5r_s, s5i_s = _s5_scan(hs, norm1[1], s5w, s5_d[0], state_s5_re[0].reshape(n_s, n_state),
                                  state_s5_im[0].reshape(n_s, n_state), sg, dec_seq)
    h, xn, eid, we = _proj_router([(zg_p, zg_s)], [s5_w_glu[0].astype(BF16)], s5_b_glu[0], (hp, hs),
                                  norm2[1], routers[1], True)
    y = moe(1, xn, eid, we)
    y_prompt = _moe_combine(h, y, norm_f, True, 0, np_rows).reshape(seq, n_p, d).transpose(1, 0, 2)
    y_sample = _moe_combine(h, y, norm_f, True, np_rows, ns_rows)
    y_sample = y_sample.reshape(n_s // sg, dec_seq, sg, d).transpose(0, 2, 1, 3).reshape(n_s, dec_seq, d)

    kv_p = lambda a: a[:np_rows].reshape(1, n_p, seq // page, page, H_A, HD_A)
    st = lambda a, n: a.reshape((1, n) + s5_lam_re.shape[1:])
    return (y_prompt, y_sample, kv_p(ka), kv_p(va), k_s[None], v_s[None], ret_p[None], ret_s[None],
            st(s5r_p, n_p), st(s5i_p, n_p), st(s5r_s, n_s), st(s5i_s, n_s))
```

```python
import functools
import math

import jax
import jax.numpy as jnp
from jax import lax
from jax.experimental import pallas as pl
from jax.experimental.pallas import tpu as pltpu

F32 = jnp.float32
BF16 = jnp.bfloat16

H_A, HD_A = 8, 64
ROT_DIM = HD_A // 4
ROPE_THETA = 500000.0
MOBA_BLOCK = 256
MOBA_TOPK = 3
H_R, DK_R, DV_R = 8, 64, 128
RET_CHUNK = 128
S5_GROUP, S5_STATE = 16, 64
MOE_GROUPS, MOE_PER_GROUP, MOE_TOPK = 4, 8, 2
N_EXPERTS = MOE_GROUPS * MOE_PER_GROUP
A_W = H_A * HD_A
R_QK = H_R * DK_R
R_V = H_R * DV_R
NEG_INF = -1e30
EPS = 1e-6

LANES = 128
SUBLANES = 8
VMEM_LIMIT = 56 * 1024 * 1024

ROW_TILE = 256
MOE_ROWS = 256


def _nt(a, b):
    return lax.dot_general(a, b, (((1,), (1,)), ((), ())), preferred_element_type=F32)


def _tn(a, b):
    return lax.dot_general(a, b, (((0,), (0,)), ((), ())), preferred_element_type=F32)


def _mm(a, b):
    return jnp.dot(a, b, preferred_element_type=F32)


def _rms_rows(x, g):
    return x * lax.rsqrt(jnp.mean(x * x, axis=-1, keepdims=True) + EPS) * g


def _split_bf16(x):
    hi = x.astype(BF16)
    lo = (x - hi.astype(F32)).astype(BF16)
    return hi, lo


def _params(sem):
    return pltpu.CompilerParams(dimension_semantics=sem, vmem_limit_bytes=VMEM_LIMIT)


def _rotate_into(out_ref, z, c_ref, s_ref, shift, first, scale):
    for j in range(z.shape[1] // LANES):
        sl = slice(j * LANES, (j + 1) * LANES)
        zc = z[:, sl]
        up = pltpu.roll(zc, LANES - shift, axis=1)
        dn = pltpu.roll(zc, shift, axis=1)
        r = zc * c_ref[:, sl] + jnp.where(first, up, dn) * s_ref[:, sl]
        out_ref[:, sl] = r if scale is None else r * scale


def _pick_rows(n_ptiles, p_ref, s_ref):
    return jnp.where(pl.program_id(0) < n_ptiles, p_ref[...], s_ref[...])


def _split_rows(tm, width, n_ptiles):
    return (pl.BlockSpec((tm, width), lambda i: (jnp.minimum(i, n_ptiles - 1), 0)),
            pl.BlockSpec((tm, width), lambda i: (jnp.maximum(i - n_ptiles, 0), 0)))


def _in_proj_kernel(xp_ref, xs_ref, g_ref, w_ref, ca_ref, sa_ref, cr_ref, sr_ref,
                    qa_ref, ka_ref, va_ref, qr_ref, kr_ref, vr_ref, gr_ref, *, n_ptiles):
    xn = _rms_rows(_pick_rows(n_ptiles, xp_ref, xs_ref), g_ref[...]).astype(BF16)
    lane = lax.broadcasted_iota(jnp.int32, (1, LANES), 1)
    first_a = (lane % HD_A) < (ROT_DIM // 2)
    first_r = (lane % 2) == 0

    def sec(lo, width):
        return _mm(xn, w_ref[:, lo:lo + width])

    _rotate_into(qa_ref, sec(0, A_W), ca_ref, sa_ref, ROT_DIM // 2, first_a, None)
    _rotate_into(ka_ref, sec(A_W, A_W), ca_ref, sa_ref, ROT_DIM // 2, first_a, None)
    va_ref[...] = sec(2 * A_W, A_W)
    _rotate_into(qr_ref, sec(3 * A_W, R_QK), cr_ref, sr_ref, 1, first_r, None)
    _rotate_into(kr_ref, sec(3 * A_W + R_QK, R_QK), cr_ref, sr_ref, 1, first_r, DK_R ** -0.5)
    vr_ref[...] = sec(3 * A_W + 2 * R_QK, R_V)
    gr_ref[...] = sec(3 * A_W + 2 * R_QK + R_V, R_V)


def _rope_tables(pos):
    half = ROT_DIM // 2
    inv = ROPE_THETA ** (-jnp.arange(half, dtype=F32) / half)
    ang = pos.astype(F32)[:, None] * inv[None, :]
    cos, sin = jnp.cos(ang), jnp.sin(ang)
    rest = HD_A - ROT_DIM
    c = jnp.concatenate([cos, cos, jnp.ones((pos.shape[0], rest), F32)], axis=-1)
    s = jnp.concatenate([-sin, sin, jnp.zeros((pos.shape[0], rest), F32)], axis=-1)
    return jnp.tile(c, (1, H_A)), jnp.tile(s, (1, H_A))


def _retnet_tables(pos):
    n = DK_R // 2
    inv = 1.0 / (10000.0 ** jnp.linspace(0.0, 1.0, n, dtype=F32))
    ang = pos.astype(F32)[:, None] * inv[None, :]
    cos, sin = jnp.cos(ang), jnp.sin(ang)
    c = jnp.repeat(cos, 2, axis=-1)
    s = jnp.stack([-sin, sin], axis=-1).reshape(pos.shape[0], DK_R)
    return jnp.tile(c, (1, H_R)), jnp.tile(s, (1, H_R))


def _in_proj(x_p, x_s, g, w_bf16, seq, dec_seq, past_len):
    d = x_p.shape[1]
    n = x_p.shape[0] + x_s.shape[0]
    tm = ROW_TILE
    tiles_per_seq = seq // tm
    n_ptiles = x_p.shape[0] // tm
    pos = jnp.concatenate([jnp.arange(seq), past_len + (jnp.arange(tm) % dec_seq)])
    ca, sa = _rope_tables(pos)
    cr, sr = _retnet_tables(pos)

    def tab_map(i):
        return (jnp.where(i < n_ptiles, i % tiles_per_seq, tiles_per_seq), 0)

    row = lambda w: pl.BlockSpec((tm, w), lambda i: (i, 0))
    tab = pl.BlockSpec((tm, A_W), tab_map)
    widths = (A_W, A_W, A_W, R_QK, R_QK, R_V, R_V)
    return pl.pallas_call(
        functools.partial(_in_proj_kernel, n_ptiles=n_ptiles),
        grid=(n // tm,),
        in_specs=[*_split_rows(tm, d, n_ptiles), pl.BlockSpec((1, d), lambda i: (0, 0)),
                  pl.BlockSpec(w_bf16.shape, lambda i: (0, 0)), tab, tab, tab, tab],
        out_specs=[row(w) for w in widths],
        out_shape=[jax.ShapeDtypeStruct((n, w), F32) for w in widths],
        compiler_params=_params(("arbitrary",)),
        name="in_proj",
    )(x_p, x_s, g.reshape(1, d), w_bf16, ca, sa, cr, sr)


def _moba_select(q_f32, kmean, n_valid, eye):
    n_blk = kmean.shape[0]
    qh, ql = _split_bf16(q_f32)
    kh, kl = _split_bf16(kmean)
    st = _nt(jnp.concatenate([kh, kl, kh], axis=1), jnp.concatenate([qh, qh, ql], axis=1))
    jrow = lax.broadcasted_iota(jnp.int32, st.shape, 0)
    rank = jnp.zeros(st.shape, F32)
    for jp in range(n_blk):
        sj = st[jp:jp + 1, :]
        beats = (sj > st) | ((sj == st) & (jp < jrow))
        rank = rank + jnp.where(beats & (jp < n_valid), 1.0, 0.0)
    sel_t = jnp.where((jrow < n_valid) & (rank < MOBA_TOPK), 1.0, 0.0)
    sel_t = jnp.concatenate([sel_t, jnp.zeros((LANES - n_blk, st.shape[1]), F32)], axis=0).astype(BF16)
    return _nt(eye, sel_t)


def _moba_prompt_kernel(q_ref, k_ref, v_ref, o_ref, kb_ref, vb_ref, km_ref, s_ref):
    b = pl.program_id(2)
    blk = MOBA_BLOCK
    n_blk = k_ref.shape[0] // blk
    half = blk // 2

    @pl.when(b == 0)
    def _():
        kb_ref[...] = k_ref[...].astype(BF16)
        vb_ref[...] = v_ref[...].astype(BF16)
        for j in range(n_blk):
            km_ref[j:j + 1, :] = jnp.mean(k_ref[j * blk:(j + 1) * blk, :], axis=0, keepdims=True)

    q = q_ref[...]
    lane = lax.broadcasted_iota(jnp.int32, (1, LANES), 1)
    r_i = lax.broadcasted_iota(jnp.int32, (blk, blk), 0)
    c_i = lax.broadcasted_iota(jnp.int32, (blk, blk), 1)
    eye = jnp.where(r_i == c_i, 1.0, 0.0).astype(BF16)
    own = pl.ds(pl.multiple_of(b * blk, blk), blk)
    heads = range(LANES // HD_A)
    fold = lambda x: (x[:, :half], x[:, half:])
    hms, q_aug, s_own, mx = [], [], [], []
    for hh in heads:
        hm = (lane // HD_A) == hh
        qh = jnp.where(hm, q, 0.0)
        selq = _moba_select(qh, km_ref[...], b, eye)
        bias = jnp.where(selq > 0.5, 0.0, NEG_INF).astype(BF16)
        qs = (qh * (HD_A ** -0.5)).astype(BF16)
        so = jnp.where(c_i <= r_i, _nt(qs, kb_ref[own, :]), NEG_INF)
        hms.append(hm)
        q_aug.append(jnp.concatenate([qs, bias], axis=1))
        s_own.append(so)
        mx.append(jnp.maximum(*fold(so)))

    def rows_of(j):
        return pl.ds(pl.multiple_of(j * blk, blk), blk)

    def scores(j, mx):
        onehot = jnp.broadcast_to(jnp.where(lane == j, 1.0, 0.0).astype(BF16), (blk, LANES))
        k_aug = jnp.concatenate([kb_ref[rows_of(j), :], onehot], axis=1)
        out = []
        for hh in heads:
            s = _nt(q_aug[hh], k_aug)
            s_ref[hh, j] = s
            out.append(jnp.maximum(mx[hh], jnp.maximum(*fold(s))))
        return tuple(out)

    mx = lax.fori_loop(0, b, scores, tuple(mx))
    ms = [jnp.max(mx[hh], axis=1, keepdims=True) for hh in heads]

    def weights(s, hh, vj):
        p = jnp.exp(s - ms[hh])
        lo, hi = fold(p)
        return lo + hi, _mm(p.astype(BF16), vj)

    def accumulate(j, carry):
        vj = vb_ref[rows_of(j), :]
        out = []
        for hh in heads:
            l_part, pv = weights(s_ref[hh, j], hh, vj)
            out.append((carry[hh][0] + l_part, carry[hh][1] + pv))
        return tuple(out)

    carry = lax.fori_loop(0, b, accumulate, tuple(weights(s_own[hh], hh, vb_ref[own, :]) for hh in heads))
    out = jnp.zeros(q.shape, F32)
    for hh in heads:
        l_part, acc = carry[hh]
        out = out + jnp.where(hms[hh], acc / jnp.sum(l_part, axis=1, keepdims=True), 0.0)
    o_ref[...] = out


def _moba_prompt(q_a, k_a, v_a, n_seq, seq):
    blk = MOBA_BLOCK
    n_blk = seq // blk
    qspec = pl.BlockSpec((blk, LANES), lambda s, h, b: (s * n_blk + b, h))
    kspec = pl.BlockSpec((seq, LANES), lambda s, h, b: (s, h))
    return pl.pallas_call(
        _moba_prompt_kernel,
        grid=(n_seq, A_W // LANES, n_blk),
        in_specs=[qspec, kspec, kspec],
        out_specs=qspec,
        out_shape=jax.ShapeDtypeStruct((n_seq * seq, A_W), F32),
        scratch_shapes=[pltpu.VMEM((seq, LANES), BF16), pltpu.VMEM((seq, LANES), BF16),
                        pltpu.VMEM((n_blk, LANES), F32), pltpu.VMEM((LANES // HD_A, n_blk, blk, blk), F32)],
        compiler_params=_params(("arbitrary", "arbitrary", "arbitrary")),
        name="moba_prompt",
    )(q_a, k_a, v_a)


def _ret_decay_tables(c):
    log_g = jnp.log(1.0 - 2.0 ** (-5.0 - jnp.arange(H_R, dtype=F32)))
    i = jnp.arange(c, dtype=F32)
    diff = i[:, None] - i[None, :]
    dmat = jnp.where(diff >= 0, jnp.exp(jnp.maximum(diff, 0.0)[None] * log_g[:, None, None]), 0.0)
    dq = jnp.exp((i + 1.0)[None, :] * log_g[:, None])
    dk = jnp.exp((c - 1.0 - i)[None, :] * log_g[:, None])
    dc = jnp.exp(c * log_g)
    return dmat, dq, dk, dc


def _ret_chunk(q, k, v, g, s, dmat, dq, dk, dc):
    att = _nt(q.astype(BF16), k.astype(BF16)) * dmat
    o = _mm(att.astype(BF16), v.astype(BF16)) + _mm((q * dq).astype(BF16), s.astype(BF16))
    s = s * dc + _tn((k * dk).astype(BF16), v.astype(BF16))
    o = o * lax.rsqrt(jnp.mean(o * o, axis=-1, keepdims=True) + EPS)
    return o * (g * jax.nn.sigmoid(g)), s


def _ret_prompt_kernel(q_ref, k_ref, v_ref, g_ref, dmat_ref, dq_ref, dk_ref, dc_ref, o_ref, st_ref):
    c = RET_CHUNK
    per = LANES // DK_R
    lane = lax.broadcasted_iota(jnp.int32, (1, LANES), 1)

    def chunk(i, states):
        rows = pl.ds(pl.multiple_of(i * c, c), c)
        q_all, k_all = q_ref[rows, :], k_ref[rows, :]
        out = []
        for hh in range(per):
            hm = (lane // DK_R) == hh
            vl = slice(hh * DV_R, (hh + 1) * DV_R)
            o, s = _ret_chunk(jnp.where(hm, q_all, 0.0), jnp.where(hm, k_all, 0.0), v_ref[rows, vl], g_ref[rows, vl],
                              states[hh], dmat_ref[hh], dq_ref[hh], dk_ref[hh], dc_ref[hh])
            o_ref[rows, vl] = o
            out.append(s)
        return tuple(out)

    states = lax.fori_loop(0, q_ref.shape[0] // c, chunk, tuple(jnp.zeros((LANES, DV_R), F32) for _ in range(per)))
    for hh in range(per):
        st_ref[hh] = states[hh][hh * DK_R:(hh + 1) * DK_R, :]


def _ret_tables_bcast(c, rows):
    dmat, dq, dk, dc = _ret_decay_tables(c)
    pad = rows - c
    dmat = jnp.pad(dmat, ((0, 0), (0, pad), (0, pad)))
    dq = jnp.broadcast_to(jnp.pad(dq, ((0, 0), (0, pad)))[:, :, None], (H_R, rows, LANES))
    dk = jnp.broadcast_to(jnp.pad(dk, ((0, 0), (0, pad)))[:, :, None], (H_R, rows, LANES))
    dc = jnp.broadcast_to(dc[:, None, None], (H_R, 1, LANES))
    return dmat, dq, dk, dc


def _ret_prompt(q_r, k_r, v_r, g_r, n_seq, seq):
    c = RET_CHUNK
    dmat, dq, dk, dc = _ret_tables_bcast(c, c)
    per = LANES // DK_R
    qk = pl.BlockSpec((seq, LANES), lambda s, h: (s, h))
    vg = pl.BlockSpec((seq, per * DV_R), lambda s, h: (s, h))
    tab = lambda r: pl.BlockSpec((per, r, LANES), lambda s, h: (h, 0, 0))
    return pl.pallas_call(
        _ret_prompt_kernel,
        grid=(n_seq, H_R // per),
        in_specs=[qk, qk, vg, vg, tab(c), tab(c), tab(c), tab(1)],
        out_specs=[vg, pl.BlockSpec((None, per, DK_R, DV_R), lambda s, h: (s, h, 0, 0))],
        out_shape=[jax.ShapeDtypeStruct((n_seq * seq, R_V), F32),
                   jax.ShapeDtypeStruct((n_seq, H_R, DK_R, DV_R), F32)],
        compiler_params=_params(("arbitrary", "arbitrary")),
        name="ret_prompt",
    )(q_r, k_r, v_r, g_r, dmat, dq, dk, dc)


def _pad_rows(x, rows):
    return jnp.concatenate([x, jnp.zeros((rows - x.shape[0], x.shape[1]), x.dtype)], axis=0)


def _moba_sample_kernel(pt_ref, q_ref, kn_ref, vn_ref, *rest, n_pages):
    k_refs, v_refs = rest[:n_pages], rest[n_pages:2 * n_pages]
    o_ref, s_ref = rest[2 * n_pages], rest[2 * n_pages + 1]
    q = q_ref[...]
    t = q.shape[0]
    page = k_refs[0].shape[-1]
    per_blk = MOBA_BLOCK // page
    n_blk = n_pages // per_blk
    lane = lax.broadcasted_iota(jnp.int32, (1, A_W), 1)
    qbd = jnp.concatenate([jnp.where((lane // HD_A) == h, q, 0.0) for h in range(H_A)], axis=0)
    qs = (qbd * (HD_A ** -0.5)).astype(BF16)
    rows = qbd.shape[0]

    bsum = [jnp.zeros((rows, 1), F32) for _ in range(n_blk)]
    for p in range(n_pages):
        sp = _mm(qs, k_refs[p][...].reshape(A_W, page).astype(BF16))
        s_ref[:, p * page:(p + 1) * page] = sp
        bsum[p // per_blk] = bsum[p // per_blk] + jnp.sum(sp, axis=1, keepdims=True)
    sel = []
    for j in range(n_blk):
        rank = jnp.zeros((rows, 1), F32)
        for jp in range(n_blk):
            if jp != j:
                beats = (bsum[jp] > bsum[j]) | (bsum[jp] == bsum[j]) if jp < j else bsum[jp] > bsum[j]
                rank = rank + jnp.where(beats, 1.0, 0.0)
        sel.append(rank < MOBA_TOPK)

    qi = lax.broadcasted_iota(jnp.int32, (rows, LANES), 0) % t
    causal = lax.broadcasted_iota(jnp.int32, (rows, LANES), 1) <= qi
    s0 = jnp.where(causal, _nt(qs, _pad_rows(kn_ref[...], LANES).astype(BF16)), NEG_INF)
    m = jnp.max(s0, axis=1, keepdims=True)
    p0 = jnp.exp(s0 - m)
    carry = (m, jnp.sum(p0, axis=1, keepdims=True), _mm(p0.astype(BF16), _pad_rows(vn_ref[...], LANES).astype(BF16)))
    for p in range(n_pages):
        m, l, acc = carry
        s = jnp.where(sel[p // per_blk], s_ref[:, p * page:(p + 1) * page], NEG_INF)
        m_new = jnp.maximum(m, jnp.max(s, axis=1, keepdims=True))
        alpha = jnp.exp(m - m_new)
        pr = jnp.exp(s - m_new)
        carry = (m_new, alpha * l + jnp.sum(pr, axis=1, keepdims=True),
                 alpha * acc + _nt(pr.astype(BF16), v_refs[p][...].reshape(A_W, page).astype(BF16)))
    _, l, acc = carry
    o = acc / l
    out = jnp.zeros((t, A_W), F32)
    for h in range(H_A):
        out = out + jnp.where((lane // HD_A) == h, o[h * t:(h + 1) * t, :], 0.0)
    o_ref[...] = out


def _moba_sample(q_a, k_a, v_a, cache_kt, cache_vt, page_table, n_prompt_rows, dec_seq):
    n_s, n_pages = page_table.shape
    base = n_prompt_rows // dec_seq
    new = pl.BlockSpec((dec_seq, A_W), lambda n, pt: (base + n, 0))
    pages = [pl.BlockSpec((None,) + cache_kt.shape[1:], lambda n, pt, p=p: (pt[n * n_pages + p], 0, 0, 0))
             for p in range(n_pages)]
    grid_spec = pltpu.PrefetchScalarGridSpec(
        num_scalar_prefetch=1,
        grid=(n_s,),
        in_specs=[new, new, new] + pages + pages,
        out_specs=pl.BlockSpec((dec_seq, A_W), lambda n, pt: (n, 0)),
        scratch_shapes=[pltpu.VMEM((H_A * dec_seq, n_pages * cache_kt.shape[-1]), F32)],
    )
    return pl.pallas_call(
        functools.partial(_moba_sample_kernel, n_pages=n_pages),
        grid_spec=grid_spec,
        out_shape=jax.ShapeDtypeStruct((n_s * dec_seq, A_W), F32),
        compiler_params=_params(("arbitrary",)),
        name="moba_sample",
    )(page_table.reshape(-1), q_a, k_a, v_a, *([cache_kt] * n_pages), *([cache_vt] * n_pages))


def _ret_sample_kernel(q_ref, k_ref, v_ref, g_ref, s0_ref, dmat_ref, dq_ref, dk_ref, dc_ref, o_ref, st_ref):
    t = q_ref.shape[0]
    lane = lax.broadcasted_iota(jnp.int32, (1, LANES), 1)
    per = LANES // DK_R
    zero_half = jnp.zeros((DK_R, DV_R), F32)
    for h in range(H_R):
        hh = h % per
        hm = (lane // DK_R) == hh
        qk_l = slice((h // per) * LANES, (h // per + 1) * LANES)
        v_l = slice(h * DV_R, (h + 1) * DV_R)
        q = _pad_rows(jnp.where(hm, q_ref[:, qk_l], 0.0), LANES)
        k = _pad_rows(jnp.where(hm, k_ref[:, qk_l], 0.0), LANES)
        v = _pad_rows(v_ref[:, v_l], LANES)
        g = _pad_rows(g_ref[:, v_l], LANES)
        halves = [zero_half] * per
        halves[hh] = s0_ref[h]
        o, s = _ret_chunk(q, k, v, g, jnp.concatenate(halves, axis=0),
                          dmat_ref[h], dq_ref[h], dk_ref[h], dc_ref[h])
        o_ref[:, v_l] = o[:t, :]
        st_ref[h] = s[hh * DK_R:(hh + 1) * DK_R, :]


def _ret_sample(q_r, k_r, v_r, g_r, s0, n_prompt_rows, dec_seq):
    n_s = s0.shape[0]
    base = n_prompt_rows // dec_seq
    dmat, dq, dk, dc = _ret_tables_bcast(dec_seq, LANES)
    qk = pl.BlockSpec((dec_seq, R_QK), lambda n: (base + n, 0))
    vg = pl.BlockSpec((dec_seq, R_V), lambda n: (base + n, 0))
    st = pl.BlockSpec((None, H_R, DK_R, DV_R), lambda n: (n, 0, 0, 0))
    full = lambda a: pl.BlockSpec(a.shape, lambda n: (0,) * a.ndim)
    return pl.pallas_call(
        _ret_sample_kernel,
        grid=(n_s,),
        in_specs=[qk, qk, vg, vg, st, full(dmat), full(dq), full(dk), full(dc)],
        out_specs=[pl.BlockSpec((dec_seq, R_V), lambda n: (n, 0)), st],
        out_shape=[jax.ShapeDtypeStruct((n_s * dec_seq, R_V), F32), jax.ShapeDtypeStruct(s0.shape, F32)],
        compiler_params=_params(("arbitrary",)),
        name="ret_sample",
    )(q_r, k_r, v_r, g_r, s0, dmat, dq, dk, dc)


ROUTER_ROWS = SUBLANES + N_EXPERTS


def _route(xn, wr_ref, br_ref, eid_ref, we_ref):
    xh, xl = _split_bf16(xn)
    lt = _nt(wr_ref[...], jnp.concatenate([xh, xh, xl], axis=1)) + br_ref[...]
    tm = lt.shape[1]
    r8 = lax.broadcasted_iota(jnp.int32, (SUBLANES, tm), 0)
    lg = jnp.where(r8 < MOE_GROUPS, lt[:SUBLANES, :], NEG_INF)
    mg = jnp.max(lg, axis=0, keepdims=True)
    wg = 1.0 / jnp.sum(jnp.exp(lg - mg), axis=0, keepdims=True)
    gidx = jnp.min(jnp.where(lg == mg, r8, SUBLANES), axis=0, keepdims=True)
    le = jnp.zeros((MOE_PER_GROUP, tm), F32)
    for gi in range(MOE_GROUPS):
        lo = SUBLANES + gi * MOE_PER_GROUP
        le = le + jnp.where(gidx == gi, lt[lo:lo + MOE_PER_GROUP, :], 0.0)
    v1 = jnp.max(le, axis=0, keepdims=True)
    i1 = jnp.min(jnp.where(le == v1, r8, MOE_PER_GROUP), axis=0, keepdims=True)
    le2 = jnp.where(r8 == i1, -jnp.inf, le)
    v2 = jnp.max(le2, axis=0, keepdims=True)
    i2 = jnp.min(jnp.where(le2 == v2, r8, MOE_PER_GROUP), axis=0, keepdims=True)
    e21 = jnp.exp(v2 - v1)
    w1 = wg / (1.0 + e21)
    eid_ref[0:1, :] = gidx * MOE_PER_GROUP + i1
    eid_ref[1:2, :] = gidx * MOE_PER_GROUP + i2
    we_ref[0:1, :] = w1
    we_ref[1:2, :] = w1 * e21


def _proj_router_kernel(*refs, n_in, glu, n_ptiles):
    a_refs, w_refs = refs[:2 * n_in], refs[2 * n_in:3 * n_in]
    k = 3 * n_in
    b_ref = refs[k] if glu else None
    k += int(glu)
    hp_ref, hs_ref, g_ref, wr_ref, br_ref, ho_ref, xn_ref, eid_ref, we_ref = refs[k:k + 9]
    acc = None
    for j, w_ref in enumerate(w_refs):
        a = _pick_rows(n_ptiles, a_refs[2 * j], a_refs[2 * j + 1])
        part = _mm(a.astype(BF16), w_ref[...])
        acc = part if acc is None else acc + part
    if glu:
        acc = acc + b_ref[...]
        half = acc.shape[1] // 2
        acc = acc[:, :half] * jax.nn.sigmoid(acc[:, half:])
    hn = _pick_rows(n_ptiles, hp_ref, hs_ref) + acc
    ho_ref[...] = hn
    xn = _rms_rows(hn, g_ref[...])
    xn_ref[...] = xn
    _route(xn, wr_ref, br_ref, eid_ref, we_ref)


def _router_weights(w_group, b_group, w_expert, b_expert):
    d = w_group.shape[0]
    wt = jnp.concatenate([w_group.T, jnp.zeros((SUBLANES - MOE_GROUPS, d), F32), w_expert.T], axis=0)
    hi = wt.astype(BF16)
    lo = (wt - hi.astype(F32)).astype(BF16)
    bias = jnp.concatenate([b_group, jnp.zeros((SUBLANES - MOE_GROUPS,), F32), b_expert]).reshape(-1, 1)
    return jnp.concatenate([hi, lo, hi], axis=1), bias


def _proj_router(acts, weights, bias, h, g, router, glu):
    d = h[0].shape[1]
    n = h[0].shape[0] + h[1].shape[0]
    tm = ROW_TILE
    n_ptiles = h[0].shape[0] // tm
    wr, br = router
    row = lambda w: pl.BlockSpec((tm, w), lambda i: (i, 0))
    full = lambda a: pl.BlockSpec(a.shape, lambda i: (0,) * a.ndim)
    tok = pl.BlockSpec((MOE_TOPK, tm), lambda i: (0, i))
    operands = ([a for pair in acts for a in pair] + list(weights) + ([bias.reshape(1, -1)] if glu else [])
                + [h[0], h[1], g.reshape(1, d), wr, br])
    in_specs = ([spec for pair in acts for spec in _split_rows(tm, pair[0].shape[1], n_ptiles)]
                + [full(w) for w in weights]
                + ([pl.BlockSpec((1, bias.shape[0]), lambda i: (0, 0))] if glu else [])
                + [*_split_rows(tm, d, n_ptiles), pl.BlockSpec((1, d), lambda i: (0, 0)), full(wr), full(br)])
    return pl.pallas_call(
        functools.partial(_proj_router_kernel, n_in=len(acts), glu=glu, n_ptiles=n_ptiles),
        grid=(n // tm,),
        in_specs=in_specs,
        out_specs=[row(d), row(d), tok, tok],
        out_shape=[jax.ShapeDtypeStruct((n, d), F32), jax.ShapeDtypeStruct((n, d), F32),
                   jax.ShapeDtypeStruct((MOE_TOPK, n), jnp.int32), jax.ShapeDtypeStruct((MOE_TOPK, n), F32)],
        compiler_params=_params(("arbitrary",)),
        name="glu_router" if glu else "out_proj_router",
    )(*operands)


def _moe_plan(eid, we):
    n_tok = eid.shape[1]
    n_assign = MOE_TOPK * n_tok
    blk = MOE_ROWS
    n_blocks = -(-n_assign // blk) + N_EXPERTS
    e_flat = eid.reshape(-1)
    experts = jnp.arange(N_EXPERTS, dtype=jnp.int32)
    counts = jnp.sum((e_flat[:, None] == experts[None, :]).astype(jnp.int32), axis=0)
    order = jnp.argsort(e_flat).astype(jnp.int32)
    pc = (counts + blk - 1) // blk * blk
    pend = jnp.cumsum(pc)
    pstart = pend - pc
    start = jnp.cumsum(counts) - counts
    n_steps = n_blocks + 2
    first_row = jnp.arange(n_steps, dtype=jnp.int32) * blk
    blk_e = jnp.minimum(jnp.sum((pend[None, :] <= first_row[:, None]).astype(jnp.int32), axis=1), N_EXPERTS - 1)
    lane = jnp.arange(blk, dtype=jnp.int32)[None, :]
    off = first_row[:, None] + lane - pstart[blk_e][:, None]
    valid = off < counts[blk_e][:, None]
    a_row = order[jnp.clip(start[blk_e][:, None] + off, 0, n_assign - 1)]
    a_row = jnp.where(valid, a_row, 0)
    w_row = jnp.where(valid, we.reshape(-1)[a_row], 0.0)
    src = a_row % n_tok
    dst = jnp.concatenate([n_assign + lane, jnp.where(valid, a_row, n_assign + lane)], axis=0)
    n_used = (pend[-1] // blk).astype(jnp.int32).reshape(1)
    return blk_e, n_used, src.reshape(-1), dst.reshape(-1), w_row.reshape(-1, 1), n_steps


def _moe_kernel(blk_e_ref, n_used_ref, src_ref, dst_ref, x_hbm, w13_ref, w2_ref, wrow_ref, y_hbm,
                xbuf, ybuf, w13b, w2b, gsem, ssem):
    b = pl.program_id(0)
    n_used = n_used_ref[0]
    blk = MOE_ROWS
    slot = b % 2

    def gather_row(block, slot, r):
        return pltpu.make_async_copy(x_hbm.at[pl.ds(src_ref[block * blk + r], 1), :],
                                     xbuf.at[slot, pl.ds(r, 1), :], gsem.at[slot])

    def scatter_row(block, slot, r):
        return pltpu.make_async_copy(ybuf.at[slot, pl.ds(r, 1), :],
                                     y_hbm.at[pl.ds(dst_ref[block * blk + r], 1), :], ssem.at[slot])

    def gather_all(slot):
        return pltpu.make_async_copy(x_hbm.at[pl.ds(0, blk), :], xbuf.at[slot], gsem.at[slot])

    def scatter_all(slot):
        return pltpu.make_async_copy(ybuf.at[slot], y_hbm.at[pl.ds(0, blk), :], ssem.at[slot])

    def issue_row_copies():
        for r in range(blk):
            gather_row(b + 1, 1 - slot, r).start(priority=r % 2)
        for r in range(blk):
            scatter_row(b, 1 - slot, r).start(priority=r % 2)

    @pl.when(b == 0)
    def _():
        xbuf[...] = jnp.zeros(xbuf.shape, xbuf.dtype)
        ybuf[...] = jnp.zeros(ybuf.shape, ybuf.dtype)
        fill = pltpu.make_async_copy(ybuf.at[0], y_hbm.at[pl.ds(y_hbm.shape[0] - blk, blk), :], ssem.at[0])
        fill.start()
        fill.wait()
        for r in range(blk):
            gather_row(0, 0, r).start(priority=r % 2)

    @pl.when(b <= n_used)
    def _():
        gather_all(slot).wait()

    @pl.when((b >= 1) & (b <= n_used + 1))
    def _():
        scatter_all(slot).wait()

    @pl.when(b < n_used)
    def _():
        @pl.when((b == 0) | (blk_e_ref[b] != blk_e_ref[jnp.maximum(b - 1, 0)]))
        def _():
            w13b[...] = w13_ref[...].astype(BF16)
            w2b[...] = w2_ref[...].astype(BF16)

        issue_row_copies()
        hb = _mm(xbuf[slot].astype(BF16), w13b[...])
        half = hb.shape[1] // 2
        gate = hb[:, :half]
        act = gate * jax.nn.sigmoid(gate) * hb[:, half:]
        ybuf[slot] = _mm(act.astype(BF16), w2b[...]) * wrow_ref[...]

    @pl.when(b == n_used)
    def _():
        for r in range(blk):
            scatter_row(b, 1 - slot, r).start(priority=r % 2)


def _moe_experts(xn, eid, we, w13, w2, layer):
    n_tok, d = xn.shape
    blk = MOE_ROWS
    blk_e, n_used, src_tok, dst_row, w_row, n_steps = _moe_plan(eid, we)
    grid_spec = pltpu.PrefetchScalarGridSpec(
        num_scalar_prefetch=4,
        grid=(n_steps,),
        in_specs=[pl.BlockSpec(memory_space=pl.ANY),
                  pl.BlockSpec((None, None) + w13.shape[2:], lambda b, be, *_: (layer, be[b], 0, 0)),
                  pl.BlockSpec((None, None) + w2.shape[2:], lambda b, be, *_: (layer, be[b], 0, 0)),
                  pl.BlockSpec((blk, 1), lambda b, *_: (b, 0))],
        out_specs=pl.BlockSpec(memory_space=pl.ANY),
        scratch_shapes=[pltpu.VMEM((2, blk, d), F32), pltpu.VMEM((2, blk, d), F32),
                        pltpu.VMEM(w13.shape[2:], BF16), pltpu.VMEM(w2.shape[2:], BF16),
                        pltpu.SemaphoreType.DMA((2,)), pltpu.SemaphoreType.DMA((2,))],
    )
    return pl.pallas_call(
        _moe_kernel,
        grid_spec=grid_spec,
        out_shape=jax.ShapeDtypeStruct((MOE_TOPK * n_tok + blk, d), F32),
        compiler_params=_params(("arbitrary",)),
        name="moe_experts",
    )(blk_e, n_used, src_tok, dst_row, xn, w13, w2, w_row)


def _combine_kernel(h_ref, y0_ref, y1_ref, g_ref, *out_refs, norm):
    h = h_ref[...] + (y0_ref[...] + y1_ref[...])
    out_refs[0][...] = _rms_rows(h, g_ref[...]) if norm else h


def _moe_combine(h, y, g, norm, row_lo, n_rows):
    n, d = h.shape
    tm = ROW_TILE
    lo = row_lo // tm
    return pl.pallas_call(
        functools.partial(_combine_kernel, norm=norm),
        grid=(n_rows // tm,),
        in_specs=[pl.BlockSpec((tm, d), lambda i: (lo + i, 0)), pl.BlockSpec((tm, d), lambda i: (lo + i, 0)),
                  pl.BlockSpec((tm, d), lambda i: (n // tm + lo + i, 0)), pl.BlockSpec((1, d), lambda i: (0, 0))],
        out_specs=pl.BlockSpec((tm, d), lambda i: (i, 0)),
        out_shape=jax.ShapeDtypeStruct((n_rows, d), F32),
        compiler_params=_params(("arbitrary",)),
        name="moe_combine_norm" if norm else "moe_combine",
    )(h, y, y, g.reshape(1, d))


S5_CH = LANES
S5_ST = (LANES // S5_GROUP) * S5_STATE
S5_SCAN_LANES = 512


def _s5_kernel(h_ref, g_ref, bre_ref, bim_ref, are_ref, aim_ref, cre_ref, cim_ref, d_ref, s0r_ref, s0i_ref,
               zg_ref, fr_ref, fi_ref, xr, xi, sr, si, *, s, steps):
    c = pl.program_id(1)

    @pl.when(c == 0)
    def _():
        sr[...] = s0r_ref[...]
        si[...] = s0i_ref[...]

    u = _rms_rows(h_ref[...], g_ref[...])
    ub = u.astype(BF16)
    n_ch = u.shape[1] // S5_CH
    for j in range(n_ch):
        uj = ub[:, j * S5_CH:(j + 1) * S5_CH]
        xr[:, j * S5_ST:(j + 1) * S5_ST] = _mm(uj, bre_ref[j])
        xi[:, j * S5_ST:(j + 1) * S5_ST] = _mm(uj, bim_ref[j])

    for lo in range(0, xr.shape[1], S5_SCAN_LANES):
        ls = slice(lo, lo + S5_SCAN_LANES)
        ar = jnp.broadcast_to(are_ref[:, ls], (s, S5_SCAN_LANES))
        ai = jnp.broadcast_to(aim_ref[:, ls], (s, S5_SCAN_LANES))

        def step(t, carry):
            pr, pi = carry
            rows = pl.ds(pl.multiple_of(t * s, s), s)
            nr = ar * pr - ai * pi + xr[rows, ls]
            ni = ar * pi + ai * pr + xi[rows, ls]
            xr[rows, ls] = nr
            xi[rows, ls] = ni
            return nr, ni

        pr, pi = lax.fori_loop(0, steps, step, (sr[:, ls], si[:, ls]))
        sr[:, ls] = pr
        si[:, ls] = pi

    for j in range(n_ch):
        cs = slice(j * S5_CH, (j + 1) * S5_CH)
        ss = slice(j * S5_ST, (j + 1) * S5_ST)
        y = _mm(xr[:, ss].astype(BF16), cre_ref[j]) - _mm(xi[:, ss].astype(BF16), cim_ref[j])
        zg_ref[:, cs] = jax.nn.gelu(y + d_ref[:, cs] * u[:, cs])

    @pl.when(c == pl.num_programs(1) - 1)
    def _():
        fr_ref[...] = sr[...]
        fi_ref[...] = si[...]


def _s5_weights(lam_re, lam_im, log_dt, b_re, b_im, c_re, c_im):
    dt = jnp.exp(log_dt)[:, None]
    mag = jnp.exp(lam_re * dt)
    ang = lam_im * dt
    ab_re = mag * jnp.cos(ang)
    ab_im = mag * jnp.sin(ang)
    den = lam_re * lam_re + lam_im * lam_im
    nr = ab_re - 1.0
    co_re = (nr * lam_re + ab_im * lam_im) / den
    co_im = (ab_im * lam_re - nr * lam_im) / den
    bb_re = co_re[..., None] * b_re - co_im[..., None] * b_im
    bb_im = co_re[..., None] * b_im + co_im[..., None] * b_re
    per = S5_CH // S5_GROUP
    n_ch = lam_re.shape[0] // per
    eye = jnp.eye(per, dtype=F32)

    def in_blocks(bb):
        w = bb.transpose(0, 2, 1).reshape(n_ch, per, S5_GROUP, S5_STATE)
        return jnp.einsum('jgcp,gh->jgchp', w, eye).reshape(n_ch, S5_CH, S5_ST).astype(BF16)

    def out_blocks(cc):
        w = cc.transpose(0, 2, 1).reshape(n_ch, per, S5_STATE, S5_GROUP)
        return jnp.einsum('jgpc,gh->jgphc', w, eye).reshape(n_ch, S5_ST, S5_CH).astype(BF16)

    return (in_blocks(bb_re), in_blocks(bb_im), ab_re.reshape(1, -1), ab_im.reshape(1, -1),
            out_blocks(c_re), out_blocks(c_im))


def _s5_scan(h, g, weights, d_skip, s0_re, s0_im, s, steps):
    n, d = h.shape
    n_groups = s0_re.shape[0] // s
    rows = steps * s
    chunks = n // (n_groups * rows)
    bre, bim, are, aim, cre, cim = weights
    n_state = are.shape[1]
    full = lambda a: pl.BlockSpec(a.shape, lambda gi, c: (0,) * a.ndim)
    row = pl.BlockSpec((rows, d), lambda gi, c: (gi * chunks + c, 0))
    state = pl.BlockSpec((s, n_state), lambda gi, c: (gi, 0))
    return pl.pallas_call(
        functools.partial(_s5_kernel, s=s, steps=steps),
        grid=(n_groups, chunks),
        in_specs=[row, pl.BlockSpec((1, d), lambda gi, c: (0, 0)), full(bre), full(bim), full(are), full(aim),
                  full(cre), full(cim), pl.BlockSpec((1, d), lambda gi, c: (0, 0)), state, state],
        out_specs=[row, state, state],
        out_shape=[jax.ShapeDtypeStruct((n, d), F32), jax.ShapeDtypeStruct(s0_re.shape, F32),
                   jax.ShapeDtypeStruct(s0_re.shape, F32)],
        scratch_shapes=[pltpu.VMEM((rows, n_state), F32), pltpu.VMEM((rows, n_state), F32),
                        pltpu.VMEM((s, n_state), F32), pltpu.VMEM((s, n_state), F32)],
        compiler_params=_params(("arbitrary", "arbitrary")),
        name="s5_scan",
    )(h, g.reshape(1, d), bre, bim, are, aim, cre, cim, d_skip.reshape(1, d), s0_re, s0_im)


S5_PROMPT_STEPS = 32
S5_SAMPLE_SEQS = 32


def kernel(x_prompt, x_sample, cache_k, cache_v, state_ret, state_s5_re, state_s5_im, page_table, norm1, norm2, norm_f, w_in_even, w_out_even, s5_lam_re, s5_lam_im, s5_log_dt, s5_b_re, s5_b_im, s5_c_re, s5_c_im, s5_d, s5_w_glu, s5_b_glu, moe_w_group, moe_b_group, moe_w_expert, moe_b_expert, moe_w13, moe_w2):
    n_p, seq, d = x_prompt.shape
    n_s, dec_seq, _ = x_sample.shape
    pool, page = cache_k.shape[1], cache_k.shape[2]
    past_len = page_table.shape[1] * page
    np_rows, ns_rows = n_p * seq, n_s * dec_seq
    routers = [_router_weights(moe_w_group[li], moe_b_group[li], moe_w_expert[li], moe_b_expert[li])
               for li in range(2)]

    def moe(li, xn, eid, we):
        return _moe_experts(xn, eid, we, moe_w13, moe_w2, li)

    xp, xs = x_prompt.reshape(np_rows, d), x_sample.reshape(ns_rows, d)
    qa, ka, va, qr, kr, vr, gr = _in_proj(xp, xs, norm1[0], w_in_even[0].astype(BF16), seq, dec_seq, past_len)
    oa_p = _moba_prompt(qa, ka, va, n_p, seq)
    or_p, ret_p = _ret_prompt(qr, kr, vr, gr, n_p, seq)
    heads = lambda a: a[np_rows:].reshape(n_s, dec_seq, H_A, HD_A)
    k_s, v_s = heads(ka), heads(va)
    oa_s = _moba_sample(qa, ka, va, cache_k[0].transpose(0, 2, 3, 1), cache_v[0].transpose(0, 2, 3, 1),
                        page_table, np_rows, dec_seq)
    or_s, ret_s = _ret_sample(qr, kr, vr, gr, state_ret[0], np_rows, dec_seq)
    w_out = w_out_even[0].astype(BF16)
    h, xn, eid, we = _proj_router([(oa_p, oa_s), (or_p, or_s)], [w_out[:A_W], w_out[A_W:]], None, (xp, xs),
                                  norm2[0], routers[0], False)
    y = moe(0, xn, eid, we)
    hp = _moe_combine(h, y, norm2[0], False, 0, np_rows)
    hs = _moe_combine(h, y, norm2[0], False, np_rows, ns_rows)

    sg = S5_SAMPLE_SEQS
    hp = hp.reshape(n_p, seq, d).transpose(1, 0, 2).reshape(np_rows, d)
    hs = hs.reshape(n_s // sg, sg, dec_seq, d).transpose(0, 2, 1, 3).reshape(ns_rows, d)
    s5w = _s5_weights(s5_lam_re[0], s5_lam_im[0], s5_log_dt[0], s5_b_re[0], s5_b_im[0], s5_c_re[0], s5_c_im[0])
    n_state = s5_lam_re.shape[1] * s5_lam_re.shape[2]
    zeros = jnp.zeros((n_p, n_state), F32)
    zg_p, s5r_p, s5i_p = _s5_scan(hp, norm1[1], s5w, s5_d[0], zeros, zeros, n_p, S5_PROMPT_STEPS)
    zg_s, s5r_s, s5i_s = _s5_scan(hs, norm1[1], s5w, s5_d[0], state_s5_re[0].reshape(n_s, n_state),
                                  state_s5_im[0].reshape(n_s, n_state), sg, dec_seq)
    h, xn, eid, we = _proj_router([(zg_p, zg_s)], [s5_w_glu[0].astype(BF16)], s5_b_glu[0], (hp, hs),
                                  norm2[1], routers[1], True)
    y = moe(1, xn, eid, we)
    y_prompt = _moe_combine(h, y, norm_f, True, 0, np_rows).reshape(seq, n_p, d).transpose(1, 0, 2)
    y_sample = _moe_combine(h, y, norm_f, True, np_rows, ns_rows)
    y_sample = y_sample.reshape(n_s // sg, dec_seq, sg, d).transpose(0, 2, 1, 3).reshape(n_s, dec_seq, d)

    kv_p = lambda a: a[:np_rows].reshape(1, n_p, seq // page, page, H_A, HD_A)
    st = lambda a, n: a.reshape((1, n) + s5_lam_re.shape[1:])
    return (y_prompt, y_sample, kv_p(ka), kv_p(va), k_s[None], v_s[None], ret_p[None], ret_s[None],
            st(s5r_p, n_p), st(s5i_p, n_p), st(s5r_s, n_s), st(s5i_s, n_s))
```

```python
import functools
import math

import jax
import jax.numpy as jnp
from jax import lax
from jax.experimental import pallas as pl
from jax.experimental.pallas import tpu as pltpu

F32 = jnp.float32
BF16 = jnp.bfloat16

H_A, HD_A = 8, 64
ROT_DIM = HD_A // 4
ROPE_THETA = 500000.0
MOBA_BLOCK = 256
MOBA_TOPK = 3
H_R, DK_R, DV_R = 8, 64, 128
RET_CHUNK = 128
S5_GROUP, S5_STATE = 16, 64
MOE_GROUPS, MOE_PER_GROUP, MOE_TOPK = 4, 8, 2
N_EXPERTS = MOE_GROUPS * MOE_PER_GROUP
A_W = H_A * HD_A
R_QK = H_R * DK_R
R_V = H_R * DV_R
NEG_INF = -1e30
EPS = 1e-6

LANES = 128
SUBLANES = 8
VMEM_LIMIT = 56 * 1024 * 1024

ROW_TILE = 256
MOE_ROWS = 256


def _nt(a, b):
    return lax.dot_general(a, b, (((1,), (1,)), ((), ())), preferred_element_type=F32)


def _tn(a, b):
    return lax.dot_general(a, b, (((0,), (0,)), ((), ())), preferred_element_type=F32)


def _mm(a, b):
    return jnp.dot(a, b, preferred_element_type=F32)


def _rms_rows(x, g):
    return x * lax.rsqrt(jnp.mean(x * x, axis=-1, keepdims=True) + EPS) * g


def _split_bf16(x):
    hi = x.astype(BF16)
    lo = (x - hi.astype(F32)).astype(BF16)
    return hi, lo


def _params(sem):
    return pltpu.CompilerParams(dimension_semantics=sem, vmem_limit_bytes=VMEM_LIMIT)


def _rotate_into(out_ref, z, c_ref, s_ref, shift, first, scale):
    for j in range(z.shape[1] // LANES):
        sl = slice(j * LANES, (j + 1) * LANES)
        zc = z[:, sl]
        up = pltpu.roll(zc, LANES - shift, axis=1)
        dn = pltpu.roll(zc, shift, axis=1)
        r = zc * c_ref[:, sl] + jnp.where(first, up, dn) * s_ref[:, sl]
        out_ref[:, sl] = r if scale is None else r * scale


def _pick_rows(n_ptiles, p_ref, s_ref):
    return jnp.where(pl.program_id(0) < n_ptiles, p_ref[...], s_ref[...])


def _split_rows(tm, width, n_ptiles):
    return (pl.BlockSpec((tm, width), lambda i: (jnp.minimum(i, n_ptiles - 1), 0)),
            pl.BlockSpec((tm, width), lambda i: (jnp.maximum(i - n_ptiles, 0), 0)))


def _in_proj_kernel(x_ref, g_ref, w_ref, ca_ref, sa_ref, cr_ref, sr_ref,
                    qa_ref, ka_ref, va_ref, qr_ref, kr_ref, vr_ref, gr_ref):
    xn = _rms_rows(x_ref[...], g_ref[...]).astype(BF16)
    lane = lax.broadcasted_iota(jnp.int32, (1, LANES), 1)
    first_a = (lane % HD_A) < (ROT_DIM // 2)
    first_r = (lane % 2) == 0

    def sec(lo, width):
        return _mm(xn, w_ref[:, lo:lo + width])

    _rotate_into(qa_ref, sec(0, A_W), ca_ref, sa_ref, ROT_DIM // 2, first_a, None)
    _rotate_into(ka_ref, sec(A_W, A_W), ca_ref, sa_ref, ROT_DIM // 2, first_a, None)
    va_ref[...] = sec(2 * A_W, A_W)
    _rotate_into(qr_ref, sec(3 * A_W, R_QK), cr_ref, sr_ref, 1, first_r, None)
    _rotate_into(kr_ref, sec(3 * A_W + R_QK, R_QK), cr_ref, sr_ref, 1, first_r, DK_R ** -0.5)
    vr_ref[...] = sec(3 * A_W + 2 * R_QK, R_V)
    gr_ref[...] = sec(3 * A_W + 2 * R_QK + R_V, R_V)


def _rope_tables(pos):
    half = ROT_DIM // 2
    inv = ROPE_THETA ** (-jnp.arange(half, dtype=F32) / half)
    ang = pos.astype(F32)[:, None] * inv[None, :]
    cos, sin = jnp.cos(ang), jnp.sin(ang)
    rest = HD_A - ROT_DIM
    c = jnp.concatenate([cos, cos, jnp.ones((pos.shape[0], rest), F32)], axis=-1)
    s = jnp.concatenate([-sin, sin, jnp.zeros((pos.shape[0], rest), F32)], axis=-1)
    return jnp.tile(c, (1, H_A)), jnp.tile(s, (1, H_A))


def _retnet_tables(pos):
    n = DK_R // 2
    inv = 1.0 / (10000.0 ** jnp.linspace(0.0, 1.0, n, dtype=F32))
    ang = pos.astype(F32)[:, None] * inv[None, :]
    cos, sin = jnp.cos(ang), jnp.sin(ang)
    c = jnp.repeat(cos, 2, axis=-1)
    s = jnp.stack([-sin, sin], axis=-1).reshape(pos.shape[0], DK_R)
    return jnp.tile(c, (1, H_R)), jnp.tile(s, (1, H_R))


def _in_proj(x, g, w_bf16, pos):
    n, d = x.shape
    tm = ROW_TILE
    period_tiles = pos.shape[0] // tm
    ca, sa = _rope_tables(pos)
    cr, sr = _retnet_tables(pos)
    row = lambda w: pl.BlockSpec((tm, w), lambda i: (i, 0))
    tab = pl.BlockSpec((tm, A_W), lambda i: (i % period_tiles, 0))
    widths = (A_W, A_W, A_W, R_QK, R_QK, R_V, R_V)
    return pl.pallas_call(
        _in_proj_kernel,
        grid=(n // tm,),
        in_specs=[row(d), pl.BlockSpec((1, d), lambda i: (0, 0)),
                  pl.BlockSpec(w_bf16.shape, lambda i: (0, 0)), tab, tab, tab, tab],
        out_specs=[row(w) for w in widths],
        out_shape=[jax.ShapeDtypeStruct((n, w), F32) for w in widths],
        compiler_params=_params(("arbitrary",)),
        name="in_proj",
    )(x, g.reshape(1, d), w_bf16, ca, sa, cr, sr)


def _moba_select(q_f32, kmean, n_valid, eye):
    n_blk = kmean.shape[0]
    qh, ql = _split_bf16(q_f32)
    kh, kl = _split_bf16(kmean)
    st = _nt(jnp.concatenate([kh, kl, kh], axis=1), jnp.concatenate([qh, qh, ql], axis=1))
    jrow = lax.broadcasted_iota(jnp.int32, st.shape, 0)
    rank = jnp.zeros(st.shape, F32)
    for jp in range(n_blk):
        sj = st[jp:jp + 1, :]
        beats = (sj > st) | ((sj == st) & (jp < jrow))
        rank = rank + jnp.where(beats & (jp < n_valid), 1.0, 0.0)
    sel_t = jnp.where((jrow < n_valid) & (rank < MOBA_TOPK), 1.0, 0.0)
    sel_t = jnp.concatenate([sel_t, jnp.zeros((LANES - n_blk, st.shape[1]), F32)], axis=0).astype(BF16)
    return _nt(eye, sel_t)


def _moba_prompt_kernel(q_ref, k_ref, v_ref, o_ref, kb_ref, vb_ref, km_ref, s_ref):
    b = pl.program_id(2)
    blk = MOBA_BLOCK
    n_blk = k_ref.shape[0] // blk
    half = blk // 2

    @pl.when(b == 0)
    def _():
        kb_ref[...] = k_ref[...].astype(BF16)
        vb_ref[...] = v_ref[...].astype(BF16)
        for j in range(n_blk):
            km_ref[j:j + 1, :] = jnp.mean(k_ref[j * blk:(j + 1) * blk, :], axis=0, keepdims=True)

    q = q_ref[...]
    lane = lax.broadcasted_iota(jnp.int32, (1, LANES), 1)
    r_i = lax.broadcasted_iota(jnp.int32, (blk, blk), 0)
    c_i = lax.broadcasted_iota(jnp.int32, (blk, blk), 1)
    eye = jnp.where(r_i == c_i, 1.0, 0.0).astype(BF16)
    own = pl.ds(pl.multiple_of(b * blk, blk), blk)
    heads = range(LANES // HD_A)
    fold = lambda x: (x[:, :half], x[:, half:])
    hms, q_aug, s_own, mx = [], [], [], []
    for hh in heads:
        hm = (lane // HD_A) == hh
        qh = jnp.where(hm, q, 0.0)
        selq = _moba_select(qh, km_ref[...], b, eye)
        bias = jnp.where(selq > 0.5, 0.0, NEG_INF).astype(BF16)
        qs = (qh * (HD_A ** -0.5)).astype(BF16)
        so = jnp.where(c_i <= r_i, _nt(qs, kb_ref[own, :]), NEG_INF)
        hms.append(hm)
        q_aug.append(jnp.concatenate([qs, bias], axis=1))
        s_own.append(so)
        mx.append(jnp.maximum(*fold(so)))

    def rows_of(j):
        return pl.ds(pl.multiple_of(j * blk, blk), blk)

    def scores(j, mx):
        onehot = jnp.broadcast_to(jnp.where(lane == j, 1.0, 0.0).astype(BF16), (blk, LANES))
        k_aug = jnp.concatenate([kb_ref[rows_of(j), :], onehot], axis=1)
        out = []
        for hh in heads:
            s = _nt(q_aug[hh], k_aug)
            s_ref[hh, j] = s
            out.append(jnp.maximum(mx[hh], jnp.maximum(*fold(s))))
        return tuple(out)

    mx = lax.fori_loop(0, b, scores, tuple(mx))
    ms = [jnp.max(mx[hh], axis=1, keepdims=True) for hh in heads]

    def weights(s, hh, vj):
        p = jnp.exp(s - ms[hh])
        lo, hi = fold(p)
        return lo + hi, _mm(p.astype(BF16), vj)

    def accumulate(j, carry):
        vj = vb_ref[rows_of(j), :]
        out = []
        for hh in heads:
            l_part, pv = weights(s_ref[hh, j], hh, vj)
            out.append((carry[hh][0] + l_part, carry[hh][1] + pv))
        return tuple(out)

    carry = lax.fori_loop(0, b, accumulate, tuple(weights(s_own[hh], hh, vb_ref[own, :]) for hh in heads))
    out = jnp.zeros(q.shape, F32)
    for hh in heads:
        l_part, acc = carry[hh]
        out = out + jnp.where(hms[hh], acc / jnp.sum(l_part, axis=1, keepdims=True), 0.0)
    o_ref[...] = out


def _moba_prompt(q_a, k_a, v_a, n_seq, seq):
    blk = MOBA_BLOCK
    n_blk = seq // blk
    qspec = pl.BlockSpec((blk, LANES), lambda s, h, b: (s * n_blk + b, h))
    kspec = pl.BlockSpec((seq, LANES), lambda s, h, b: (s, h))
    return pl.pallas_call(
        _moba_prompt_kernel,
        grid=(n_seq, A_W // LANES, n_blk),
        in_specs=[qspec, kspec, kspec],
        out_specs=qspec,
        out_shape=jax.ShapeDtypeStruct((n_seq * seq, A_W), F32),
        scratch_shapes=[pltpu.VMEM((seq, LANES), BF16), pltpu.VMEM((seq, LANES), BF16),
                        pltpu.VMEM((n_blk, LANES), F32), pltpu.VMEM((LANES // HD_A, n_blk, blk, blk), F32)],
        compiler_params=_params(("arbitrary", "arbitrary", "arbitrary")),
        name="moba_prompt",
    )(q_a, k_a, v_a)


def _ret_decay_tables(c):
    log_g = jnp.log(1.0 - 2.0 ** (-5.0 - jnp.arange(H_R, dtype=F32)))
    i = jnp.arange(c, dtype=F32)
    diff = i[:, None] - i[None, :]
    dmat = jnp.where(diff >= 0, jnp.exp(jnp.maximum(diff, 0.0)[None] * log_g[:, None, None]), 0.0)
    dq = jnp.exp((i + 1.0)[None, :] * log_g[:, None])
    dk = jnp.exp((c - 1.0 - i)[None, :] * log_g[:, None])
    dc = jnp.exp(c * log_g)
    return dmat, dq, dk, dc


def _ret_chunk(q, k, v, g, s, dmat, dq, dk, dc):
    att = _nt(q.astype(BF16), k.astype(BF16)) * dmat
    o = _mm(att.astype(BF16), v.astype(BF16)) + _mm((q * dq).astype(BF16), s.astype(BF16))
    s = s * dc + _tn((k * dk).astype(BF16), v.astype(BF16))
    o = o * lax.rsqrt(jnp.mean(o * o, axis=-1, keepdims=True) + EPS)
    return o * (g * jax.nn.sigmoid(g)), s


def _ret_prompt_kernel(q_ref, k_ref, v_ref, g_ref, dmat_ref, dq_ref, dk_ref, dc_ref, o_ref, st_ref):
    c = RET_CHUNK
    per = LANES // DK_R
    lane = lax.broadcasted_iota(jnp.int32, (1, LANES), 1)

    def chunk(i, states):
        rows = pl.ds(pl.multiple_of(i * c, c), c)
        q_all, k_all = q_ref[rows, :], k_ref[rows, :]
        out = []
        for hh in range(per):
            hm = (lane // DK_R) == hh
            vl = slice(hh * DV_R, (hh + 1) * DV_R)
            o, s = _ret_chunk(jnp.where(hm, q_all, 0.0), jnp.where(hm, k_all, 0.0), v_ref[rows, vl], g_ref[rows, vl],
                              states[hh], dmat_ref[hh], dq_ref[hh], dk_ref[hh], dc_ref[hh])
            o_ref[rows, vl] = o
            out.append(s)
        return tuple(out)

    states = lax.fori_loop(0, q_ref.shape[0] // c, chunk, tuple(jnp.zeros((LANES, DV_R), F32) for _ in range(per)),
                           unroll=2)
    for hh in range(per):
        st_ref[hh] = states[hh][hh * DK_R:(hh + 1) * DK_R, :]


def _ret_tables_bcast(c, rows):
    dmat, dq, dk, dc = _ret_decay_tables(c)
    pad = rows - c
    dmat = jnp.pad(dmat, ((0, 0), (0, pad), (0, pad)))
    dq = jnp.broadcast_to(jnp.pad(dq, ((0, 0), (0, pad)))[:, :, None], (H_R, rows, LANES))
    dk = jnp.broadcast_to(jnp.pad(dk, ((0, 0), (0, pad)))[:, :, None], (H_R, rows, LANES))
    dc = jnp.broadcast_to(dc[:, None, None], (H_R, 1, LANES))
    return dmat, dq, dk, dc


def _ret_prompt(q_r, k_r, v_r, g_r, n_seq, seq):
    c = RET_CHUNK
    dmat, dq, dk, dc = _ret_tables_bcast(c, c)
    per = LANES // DK_R
    qk = pl.BlockSpec((seq, LANES), lambda s, h: (s, h))
    vg = pl.BlockSpec((seq, per * DV_R), lambda s, h: (s, h))
    tab = lambda r: pl.BlockSpec((per, r, LANES), lambda s, h: (h, 0, 0))
    return pl.pallas_call(
        _ret_prompt_kernel,
        grid=(n_seq, H_R // per),
        in_specs=[qk, qk, vg, vg, tab(c), tab(c), tab(c), tab(1)],
        out_specs=[vg, pl.BlockSpec((None, per, DK_R, DV_R), lambda s, h: (s, h, 0, 0))],
        out_shape=[jax.ShapeDtypeStruct((n_seq * seq, R_V), F32),
                   jax.ShapeDtypeStruct((n_seq, H_R, DK_R, DV_R), F32)],
        compiler_params=_params(("arbitrary", "arbitrary")),
        name="ret_prompt",
    )(q_r, k_r, v_r, g_r, dmat, dq, dk, dc)


def _pad_rows(x, rows):
    return jnp.concatenate([x, jnp.zeros((rows - x.shape[0], x.shape[1]), x.dtype)], axis=0)


def _moba_sample_kernel(pt_ref, q_ref, kn_ref, vn_ref, *rest, n_pages):
    k_refs, v_refs = rest[:n_pages], rest[n_pages:2 * n_pages]
    o_ref, s_ref = rest[2 * n_pages], rest[2 * n_pages + 1]
    q = q_ref[...]
    t = q.shape[0]
    page = k_refs[0].shape[-1]
    per_blk = MOBA_BLOCK // page
    n_blk = n_pages // per_blk
    lane = lax.broadcasted_iota(jnp.int32, (1, A_W), 1)
    qbd = jnp.concatenate([jnp.where((lane // HD_A) == h, q, 0.0) for h in range(H_A)], axis=0)
    qs = (qbd * (HD_A ** -0.5)).astype(BF16)
    rows = qbd.shape[0]

    bsum = [jnp.zeros((rows, 1), F32) for _ in range(n_blk)]
    for p in range(n_pages):
        sp = _mm(qs, k_refs[p][...].reshape(A_W, page).astype(BF16))
        s_ref[:, p * page:(p + 1) * page] = sp
        bsum[p // per_blk] = bsum[p // per_blk] + jnp.sum(sp, axis=1, keepdims=True)
    sel = []
    for j in range(n_blk):
        rank = jnp.zeros((rows, 1), F32)
        for jp in range(n_blk):
            if jp != j:
                beats = (bsum[jp] > bsum[j]) | (bsum[jp] == bsum[j]) if jp < j else bsum[jp] > bsum[j]
                rank = rank + jnp.where(beats, 1.0, 0.0)
        sel.append(rank < MOBA_TOPK)

    qi = lax.broadcasted_iota(jnp.int32, (rows, LANES), 0) % t
    causal = lax.broadcasted_iota(jnp.int32, (rows, LANES), 1) <= qi
    s0 = jnp.where(causal, _nt(qs, _pad_rows(kn_ref[...], LANES).astype(BF16)), NEG_INF)
    m = jnp.max(s0, axis=1, keepdims=True)
    p0 = jnp.exp(s0 - m)
    carry = (m, jnp.sum(p0, axis=1, keepdims=True), _mm(p0.astype(BF16), _pad_rows(vn_ref[...], LANES).astype(BF16)))
    for p in range(n_pages):
        m, l, acc = carry
        s = jnp.where(sel[p // per_blk], s_ref[:, p * page:(p + 1) * page], NEG_INF)
        m_new = jnp.maximum(m, jnp.max(s, axis=1, keepdims=True))
        alpha = jnp.exp(m - m_new)
        pr = jnp.exp(s - m_new)
        carry = (m_new, alpha * l + jnp.sum(pr, axis=1, keepdims=True),
                 alpha * acc + _nt(pr.astype(BF16), v_refs[p][...].reshape(A_W, page).astype(BF16)))
    _, l, acc = carry
    o = acc / l
    out = jnp.zeros((t, A_W), F32)
    for h in range(H_A):
        out = out + jnp.where((lane // HD_A) == h, o[h * t:(h + 1) * t, :], 0.0)
    o_ref[...] = out


def _moba_sample(q_a, k_a, v_a, cache_kt, cache_vt, page_table, dec_seq):
    n_s, n_pages = page_table.shape
    new = pl.BlockSpec((dec_seq, A_W), lambda n, pt: (n, 0))
    pages = [pl.BlockSpec((None,) + cache_kt.shape[1:], lambda n, pt, p=p: (pt[n * n_pages + p], 0, 0, 0))
             for p in range(n_pages)]
    grid_spec = pltpu.PrefetchScalarGridSpec(
        num_scalar_prefetch=1,
        grid=(n_s,),
        in_specs=[new, new, new] + pages + pages,
        out_specs=pl.BlockSpec((dec_seq, A_W), lambda n, pt: (n, 0)),
        scratch_shapes=[pltpu.VMEM((H_A * dec_seq, n_pages * cache_kt.shape[-1]), F32)],
    )
    return pl.pallas_call(
        functools.partial(_moba_sample_kernel, n_pages=n_pages),
        grid_spec=grid_spec,
        out_shape=jax.ShapeDtypeStruct((n_s * dec_seq, A_W), F32),
        compiler_params=_params(("arbitrary",)),
        name="moba_sample",
    )(page_table.reshape(-1), q_a, k_a, v_a, *([cache_kt] * n_pages), *([cache_vt] * n_pages))


def _ret_sample_kernel(q_ref, k_ref, v_ref, g_ref, s0_ref, dmat_ref, dq_ref, dk_ref, dc_ref, o_ref, st_ref):
    t = q_ref.shape[0]
    lane = lax.broadcasted_iota(jnp.int32, (1, LANES), 1)
    per = LANES // DK_R
    zero_half = jnp.zeros((DK_R, DV_R), F32)
    for h in range(H_R):
        hh = h % per
        hm = (lane // DK_R) == hh
        qk_l = slice((h // per) * LANES, (h // per + 1) * LANES)
        v_l = slice(h * DV_R, (h + 1) * DV_R)
        q = _pad_rows(jnp.where(hm, q_ref[:, qk_l], 0.0), LANES)
        k = _pad_rows(jnp.where(hm, k_ref[:, qk_l], 0.0), LANES)
        v = _pad_rows(v_ref[:, v_l], LANES)
        g = _pad_rows(g_ref[:, v_l], LANES)
        halves = [zero_half] * per
        halves[hh] = s0_ref[h]
        o, s = _ret_chunk(q, k, v, g, jnp.concatenate(halves, axis=0),
                          dmat_ref[h], dq_ref[h], dk_ref[h], dc_ref[h])
        o_ref[:, v_l] = o[:t, :]
        st_ref[h] = s[hh * DK_R:(hh + 1) * DK_R, :]


def _ret_sample(q_r, k_r, v_r, g_r, s0, dec_seq):
    n_s = s0.shape[0]
    dmat, dq, dk, dc = _ret_tables_bcast(dec_seq, LANES)
    qk = pl.BlockSpec((dec_seq, R_QK), lambda n: (n, 0))
    vg = pl.BlockSpec((dec_seq, R_V), lambda n: (n, 0))
    st = pl.BlockSpec((None, H_R, DK_R, DV_R), lambda n: (n, 0, 0, 0))
    full = lambda a: pl.BlockSpec(a.shape, lambda n: (0,) * a.ndim)
    return pl.pallas_call(
        _ret_sample_kernel,
        grid=(n_s,),
        in_specs=[qk, qk, vg, vg, st, full(dmat), full(dq), full(dk), full(dc)],
        out_specs=[pl.BlockSpec((dec_seq, R_V), lambda n: (n, 0)), st],
        out_shape=[jax.ShapeDtypeStruct((n_s * dec_seq, R_V), F32), jax.ShapeDtypeStruct(s0.shape, F32)],
        compiler_params=_params(("arbitrary",)),
        name="ret_sample",
    )(q_r, k_r, v_r, g_r, s0, dmat, dq, dk, dc)


ROUTER_ROWS = SUBLANES + N_EXPERTS


def _route(xn, wr_ref, br_ref, eid_ref, we_ref):
    xh, xl = _split_bf16(xn)
    lt = _nt(wr_ref[...], jnp.concatenate([xh, xh, xl], axis=1)) + br_ref[...]
    tm = lt.shape[1]
    r8 = lax.broadcasted_iota(jnp.int32, (SUBLANES, tm), 0)
    lg = jnp.where(r8 < MOE_GROUPS, lt[:SUBLANES, :], NEG_INF)
    mg = jnp.max(lg, axis=0, keepdims=True)
    wg = 1.0 / jnp.sum(jnp.exp(lg - mg), axis=0, keepdims=True)
    gidx = jnp.min(jnp.where(lg == mg, r8, SUBLANES), axis=0, keepdims=True)
    le = jnp.zeros((MOE_PER_GROUP, tm), F32)
    for gi in range(MOE_GROUPS):
        lo = SUBLANES + gi * MOE_PER_GROUP
        le = le + jnp.where(gidx == gi, lt[lo:lo + MOE_PER_GROUP, :], 0.0)
    v1 = jnp.max(le, axis=0, keepdims=True)
    i1 = jnp.min(jnp.where(le == v1, r8, MOE_PER_GROUP), axis=0, keepdims=True)
    le2 = jnp.where(r8 == i1, -jnp.inf, le)
    v2 = jnp.max(le2, axis=0, keepdims=True)
    i2 = jnp.min(jnp.where(le2 == v2, r8, MOE_PER_GROUP), axis=0, keepdims=True)
    e21 = jnp.exp(v2 - v1)
    w1 = wg / (1.0 + e21)
    eid_ref[0:1, :] = gidx * MOE_PER_GROUP + i1
    eid_ref[1:2, :] = gidx * MOE_PER_GROUP + i2
    we_ref[0:1, :] = w1
    we_ref[1:2, :] = w1 * e21


def _proj_router_kernel(*refs, n_in, glu, n_ptiles):
    a_refs, w_refs = refs[:2 * n_in], refs[2 * n_in:3 * n_in]
    k = 3 * n_in
    b_ref = refs[k] if glu else None
    k += int(glu)
    hp_ref, hs_ref, g_ref, wr_ref, br_ref, ho_ref, xn_ref, eid_ref, we_ref = refs[k:k + 9]
    acc = None
    for j, w_ref in enumerate(w_refs):
        a = _pick_rows(n_ptiles, a_refs[2 * j], a_refs[2 * j + 1])
        part = _mm(a.astype(BF16), w_ref[...])
        acc = part if acc is None else acc + part
    if glu:
        acc = acc + b_ref[...]
        half = acc.shape[1] // 2
        acc = acc[:, :half] * jax.nn.sigmoid(acc[:, half:])
    hn = _pick_rows(n_ptiles, hp_ref, hs_ref) + acc
    ho_ref[...] = hn
    xn = _rms_rows(hn, g_ref[...])
    xn_ref[...] = xn
    _route(xn, wr_ref, br_ref, eid_ref, we_ref)


def _router_weights(w_group, b_group, w_expert, b_expert):
    d = w_group.shape[0]
    wt = jnp.concatenate([w_group.T, jnp.zeros((SUBLANES - MOE_GROUPS, d), F32), w_expert.T], axis=0)
    hi = wt.astype(BF16)
    lo = (wt - hi.astype(F32)).astype(BF16)
    bias = jnp.concatenate([b_group, jnp.zeros((SUBLANES - MOE_GROUPS,), F32), b_expert]).reshape(-1, 1)
    return jnp.concatenate([hi, lo, hi], axis=1), bias


def _proj_router(acts, weights, bias, h, g, router, glu):
    d = h[0].shape[1]
    n = h[0].shape[0] + h[1].shape[0]
    tm = ROW_TILE
    n_ptiles = h[0].shape[0] // tm
    wr, br = router
    row = lambda w: pl.BlockSpec((tm, w), lambda i: (i, 0))
    full = lambda a: pl.BlockSpec(a.shape, lambda i: (0,) * a.ndim)
    tok = pl.BlockSpec((MOE_TOPK, tm), lambda i: (0, i))
    operands = ([a for pair in acts for a in pair] + list(weights) + ([bias.reshape(1, -1)] if glu else [])
                + [h[0], h[1], g.reshape(1, d), wr, br])
    in_specs = ([spec for pair in acts for spec in _split_rows(tm, pair[0].shape[1], n_ptiles)]
                + [full(w) for w in weights]
                + ([pl.BlockSpec((1, bias.shape[0]), lambda i: (0, 0))] if glu else [])
                + [*_split_rows(tm, d, n_ptiles), pl.BlockSpec((1, d), lambda i: (0, 0)), full(wr), full(br)])
    return pl.pallas_call(
        functools.partial(_proj_router_kernel, n_in=len(acts), glu=glu, n_ptiles=n_ptiles),
        grid=(n // tm,),
        in_specs=in_specs,
        out_specs=[row(d), row(d), tok, tok],
        out_shape=[jax.ShapeDtypeStruct((n, d), F32), jax.ShapeDtypeStruct((n, d), F32),
                   jax.ShapeDtypeStruct((MOE_TOPK, n), jnp.int32), jax.ShapeDtypeStruct((MOE_TOPK, n), F32)],
        compiler_params=_params(("arbitrary",)),
        name="glu_router" if glu else "out_proj_router",
    )(*operands)


def _moe_plan(eid):
    n_tok = eid.shape[1]
    n_assign = MOE_TOPK * n_tok
    blk = MOE_ROWS
    n_blocks = -(-n_assign // blk) + N_EXPERTS
    e_flat = eid.reshape(-1)
    experts = jnp.arange(N_EXPERTS, dtype=jnp.int32)
    counts = jnp.sum((e_flat[:, None] == experts[None, :]).astype(jnp.int32), axis=0)
    order = jnp.argsort(e_flat).astype(jnp.int32)
    pc = (counts + blk - 1) // blk * blk
    pend = jnp.cumsum(pc)
    pstart = pend - pc
    start = jnp.cumsum(counts) - counts
    n_steps = n_blocks + 2
    first_row = jnp.arange(n_steps, dtype=jnp.int32) * blk
    blk_e = jnp.minimum(jnp.sum((pend[None, :] <= first_row[:, None]).astype(jnp.int32), axis=1), N_EXPERTS - 1)
    lane = jnp.arange(blk, dtype=jnp.int32)[None, :]
    off = first_row[:, None] + lane - pstart[blk_e][:, None]
    valid = off < counts[blk_e][:, None]
    a_row = order[jnp.clip(start[blk_e][:, None] + off, 0, n_assign - 1)]
    a_row = jnp.where(valid, a_row, 0)
    src = a_row % n_tok
    dst = jnp.concatenate([n_assign + lane, jnp.where(valid, a_row, n_assign + lane)], axis=0)
    n_used = (pend[-1] // blk).astype(jnp.int32).reshape(1)
    return blk_e, n_used, src.reshape(-1), dst.reshape(-1), n_steps


def _moe_kernel(blk_e_ref, n_used_ref, src_ref, dst_ref, x_hbm, w13_ref, w2_ref, y_hbm,
                xbuf, ybuf, w13b, w2b, gsem, ssem):
    b = pl.program_id(0)
    n_used = n_used_ref[0]
    blk = MOE_ROWS
    slot = b % 2

    def gather_row(block, slot, r):
        return pltpu.make_async_copy(x_hbm.at[pl.ds(src_ref[block * blk + r], 1), :],
                                     xbuf.at[slot, pl.ds(r, 1), :], gsem.at[slot])

    def scatter_row(block, slot, r):
        return pltpu.make_async_copy(ybuf.at[slot, pl.ds(r, 1), :],
                                     y_hbm.at[pl.ds(dst_ref[block * blk + r], 1), :], ssem.at[slot])

    def gather_all(slot):
        return pltpu.make_async_copy(x_hbm.at[pl.ds(0, blk), :], xbuf.at[slot], gsem.at[slot])

    def scatter_all(slot):
        return pltpu.make_async_copy(ybuf.at[slot], y_hbm.at[pl.ds(0, blk), :], ssem.at[slot])

    def issue_row_copies():
        for r in range(blk):
            gather_row(b + 1, 1 - slot, r).start(priority=r % 2)
        for r in range(blk):
            scatter_row(b, 1 - slot, r).start(priority=r % 2)

    @pl.when(b == 0)
    def _():
        xbuf[...] = jnp.zeros(xbuf.shape, xbuf.dtype)
        ybuf[...] = jnp.zeros(ybuf.shape, ybuf.dtype)
        fill = pltpu.make_async_copy(ybuf.at[0], y_hbm.at[pl.ds(y_hbm.shape[0] - blk, blk), :], ssem.at[0])
        fill.start()
        fill.wait()
        for r in range(blk):
            gather_row(0, 0, r).start(priority=r % 2)

    @pl.when(b <= n_used)
    def _():
        gather_all(slot).wait()

    @pl.when((b >= 1) & (b <= n_used + 1))
    def _():
        scatter_all(slot).wait()

    @pl.when(b < n_used)
    def _():
        @pl.when((b == 0) | (blk_e_ref[b] != blk_e_ref[jnp.maximum(b - 1, 0)]))
        def _():
            w13b[...] = w13_ref[...].astype(BF16)
            w2b[...] = w2_ref[...].astype(BF16)

        issue_row_copies()
        hb = _mm(xbuf[slot].astype(BF16), w13b[...])
        half = hb.shape[1] // 2
        gate = hb[:, :half]
        act = gate * jax.nn.sigmoid(gate) * hb[:, half:]
        ybuf[slot] = _mm(act.astype(BF16), w2b[...])

    @pl.when(b == n_used)
    def _():
        for r in range(blk):
            scatter_row(b, 1 - slot, r).start(priority=r % 2)


def _moe_experts(xn, eid, w13, w2, layer):
    n_tok, d = xn.shape
    blk = MOE_ROWS
    blk_e, n_used, src_tok, dst_row, n_steps = _moe_plan(eid)
    grid_spec = pltpu.PrefetchScalarGridSpec(
        num_scalar_prefetch=4,
        grid=(n_steps,),
        in_specs=[pl.BlockSpec(memory_space=pl.ANY),
                  pl.BlockSpec((None, None) + w13.shape[2:], lambda b, be, *_: (layer, be[b], 0, 0)),
                  pl.BlockSpec((None, None) + w2.shape[2:], lambda b, be, *_: (layer, be[b], 0, 0))],
        out_specs=pl.BlockSpec(memory_space=pl.ANY),
        scratch_shapes=[pltpu.VMEM((2, blk, d), F32), pltpu.VMEM((2, blk, d), F32),
                        pltpu.VMEM(w13.shape[2:], BF16), pltpu.VMEM(w2.shape[2:], BF16),
                        pltpu.SemaphoreType.DMA((2,)), pltpu.SemaphoreType.DMA((2,))],
    )
    return pl.pallas_call(
        _moe_kernel,
        grid_spec=grid_spec,
        out_shape=jax.ShapeDtypeStruct((MOE_TOPK * n_tok + blk, d), F32),
        compiler_params=_params(("arbitrary",)),
        name="moe_experts",
    )(blk_e, n_used, src_tok, dst_row, xn, w13, w2)


def _combine_kernel(h_ref, y0_ref, y1_ref, w_ref, g_ref, o_ref, *, norm):
    w = w_ref[...]
    h = h_ref[...] + (y0_ref[...] * w[:, 0:1] + y1_ref[...] * w[:, 1:2])
    o_ref[...] = _rms_rows(h, g_ref[...]) if norm else h


def _moe_combine(h, y, we, g, norm, row_lo, n_rows):
    n, d = h.shape
    tm = ROW_TILE
    lo = row_lo // tm
    return pl.pallas_call(
        functools.partial(_combine_kernel, norm=norm),
        grid=(n_rows // tm,),
        in_specs=[pl.BlockSpec((tm, d), lambda i: (lo + i, 0)), pl.BlockSpec((tm, d), lambda i: (lo + i, 0)),
                  pl.BlockSpec((tm, d), lambda i: (n // tm + lo + i, 0)),
                  pl.BlockSpec((tm, MOE_TOPK), lambda i: (lo + i, 0)), pl.BlockSpec((1, d), lambda i: (0, 0))],
        out_specs=pl.BlockSpec((tm, d), lambda i: (i, 0)),
        out_shape=jax.ShapeDtypeStruct((n_rows, d), F32),
        compiler_params=_params(("arbitrary",)),
        name="moe_combine_norm" if norm else "moe_combine",
    )(h, y, y, we.T, g.reshape(1, d))


S5_CH = LANES
S5_ST = (LANES // S5_GROUP) * S5_STATE
S5_SCAN_LANES = 512


def _s5_kernel(h_ref, g_ref, bre_ref, bim_ref, are_ref, aim_ref, cre_ref, cim_ref, d_ref, s0r_ref, s0i_ref,
               zg_ref, fr_ref, fi_ref, xr, xi, sr, si, *, s, steps):
    c = pl.program_id(1)

    @pl.when(c == 0)
    def _():
        sr[...] = s0r_ref[...]
        si[...] = s0i_ref[...]

    u = _rms_rows(h_ref[...], g_ref[...])
    ub = u.astype(BF16)
    n_ch = u.shape[1] // S5_CH
    for j in range(n_ch):
        uj = ub[:, j * S5_CH:(j + 1) * S5_CH]
        xr[:, j * S5_ST:(j + 1) * S5_ST] = _mm(uj, bre_ref[j])
        xi[:, j * S5_ST:(j + 1) * S5_ST] = _mm(uj, bim_ref[j])

    for lo in range(0, xr.shape[1], S5_SCAN_LANES):
        ls = slice(lo, lo + S5_SCAN_LANES)
        ar = jnp.broadcast_to(are_ref[:, ls], (s, S5_SCAN_LANES))
        ai = jnp.broadcast_to(aim_ref[:, ls], (s, S5_SCAN_LANES))

        pr, pi = sr[:, ls], si[:, ls]
        for t in range(steps):
            rows = slice(t * s, (t + 1) * s)
            pr, pi = ar * pr - ai * pi + xr[rows, ls], ar * pi + ai * pr + xi[rows, ls]
            xr[rows, ls] = pr
            xi[rows, ls] = pi
        sr[:, ls] = pr
        si[:, ls] = pi

    for j in range(n_ch):
        cs = slice(j * S5_CH, (j + 1) * S5_CH)
        ss = slice(j * S5_ST, (j + 1) * S5_ST)
        y = _mm(xr[:, ss].astype(BF16), cre_ref[j]) - _mm(xi[:, ss].astype(BF16), cim_ref[j])
        zg_ref[:, cs] = jax.nn.gelu(y + d_ref[:, cs] * u[:, cs])

    @pl.when(c == pl.num_programs(1) - 1)
    def _():
        fr_ref[...] = sr[...]
        fi_ref[...] = si[...]


def _s5_weights(lam_re, lam_im, log_dt, b_re, b_im, c_re, c_im):
    dt = jnp.exp(log_dt)[:, None]
    mag = jnp.exp(lam_re * dt)
    ang = lam_im * dt
    ab_re = mag * jnp.cos(ang)
    ab_im = mag * jnp.sin(ang)
    den = lam_re * lam_re + lam_im * lam_im
    nr = ab_re - 1.0
    co_re = (nr * lam_re + ab_im * lam_im) / den
    co_im = (ab_im * lam_re - nr * lam_im) / den
    bb_re = co_re[..., None] * b_re - co_im[..., None] * b_im
    bb_im = co_re[..., None] * b_im + co_im[..., None] * b_re
    per = S5_CH // S5_GROUP
    n_ch = lam_re.shape[0] // per
    eye = jnp.eye(per, dtype=F32)

    def in_blocks(bb):
        w = bb.transpose(0, 2, 1).reshape(n_ch, per, S5_GROUP, S5_STATE)
        return jnp.einsum('jgcp,gh->jgchp', w, eye).reshape(n_ch, S5_CH, S5_ST).astype(BF16)

    def out_blocks(cc):
        w = cc.transpose(0, 2, 1).reshape(n_ch, per, S5_STATE, S5_GROUP)
        return jnp.einsum('jgpc,gh->jgphc', w, eye).reshape(n_ch, S5_ST, S5_CH).astype(BF16)

    return (in_blocks(bb_re), in_blocks(bb_im), ab_re.reshape(1, -1), ab_im.reshape(1, -1),
            out_blocks(c_re), out_blocks(c_im))


def _s5_scan(h, g, weights, d_skip, s0_re, s0_im, s, steps):
    n, d = h.shape
    n_groups = s0_re.shape[0] // s
    rows = steps * s
    chunks = n // (n_groups * rows)
    bre, bim, are, aim, cre, cim = weights
    n_state = are.shape[1]
    full = lambda a: pl.BlockSpec(a.shape, lambda gi, c: (0,) * a.ndim)
    row = pl.BlockSpec((rows, d), lambda gi, c: (gi * chunks + c, 0))
    state = pl.BlockSpec((s, n_state), lambda gi, c: (gi, 0))
    return pl.pallas_call(
        functools.partial(_s5_kernel, s=s, steps=steps),
        grid=(n_groups, chunks),
        in_specs=[row, pl.BlockSpec((1, d), lambda gi, c: (0, 0)), full(bre), full(bim), full(are), full(aim),
                  full(cre), full(cim), pl.BlockSpec((1, d), lambda gi, c: (0, 0)), state, state],
        out_specs=[row, state, state],
        out_shape=[jax.ShapeDtypeStruct((n, d), F32), jax.ShapeDtypeStruct(s0_re.shape, F32),
                   jax.ShapeDtypeStruct(s0_re.shape, F32)],
        scratch_shapes=[pltpu.VMEM((rows, n_state), F32), pltpu.VMEM((rows, n_state), F32),
                        pltpu.VMEM((s, n_state), F32), pltpu.VMEM((s, n_state), F32)],
        compiler_params=_params(("arbitrary", "arbitrary")),
        name="s5_scan",
    )(h, g.reshape(1, d), bre, bim, are, aim, cre, cim, d_skip.reshape(1, d), s0_re, s0_im)


S5_PROMPT_STEPS = 32
S5_SAMPLE_SEQS = 32


def kernel(x_prompt, x_sample, cache_k, cache_v, state_ret, state_s5_re, state_s5_im, page_table, norm1, norm2, norm_f, w_in_even, w_out_even, s5_lam_re, s5_lam_im, s5_log_dt, s5_b_re, s5_b_im, s5_c_re, s5_c_im, s5_d, s5_w_glu, s5_b_glu, moe_w_group, moe_b_group, moe_w_expert, moe_b_expert, moe_w13, moe_w2):
    n_p, seq, d = x_prompt.shape
    n_s, dec_seq, _ = x_sample.shape
    pool, page = cache_k.shape[1], cache_k.shape[2]
    past_len = page_table.shape[1] * page
    np_rows, ns_rows = n_p * seq, n_s * dec_seq
    routers = [_router_weights(moe_w_group[li], moe_b_group[li], moe_w_expert[li], moe_b_expert[li])
               for li in range(2)]

    def moe(li, xn, eid):
        return _moe_experts(xn, eid, moe_w13, moe_w2, li)

    xp, xs = x_prompt.reshape(np_rows, d), x_sample.reshape(ns_rows, d)
    w_in = w_in_even[0].astype(BF16)
    qa_p, ka_p, va_p, qr_p, kr_p, vr_p, gr_p = _in_proj(xp, norm1[0], w_in, jnp.arange(seq))
    qa_s, ka_s, va_s, qr_s, kr_s, vr_s, gr_s = _in_proj(xs, norm1[0], w_in,
                                                        past_len + jnp.arange(ROW_TILE) % dec_seq)
    oa_p = _moba_prompt(qa_p, ka_p, va_p, n_p, seq)
    or_p, ret_p = _ret_prompt(qr_p, kr_p, vr_p, gr_p, n_p, seq)
    oa_s = _moba_sample(qa_s, ka_s, va_s, cache_k[0].transpose(0, 2, 3, 1), cache_v[0].transpose(0, 2, 3, 1),
                        page_table, dec_seq)
    or_s, ret_s = _ret_sample(qr_s, kr_s, vr_s, gr_s, state_ret[0], dec_seq)
    w_out = w_out_even[0].astype(BF16)
    h, xn, eid, we = _proj_router([(oa_p, oa_s), (or_p, or_s)], [w_out[:A_W], w_out[A_W:]], None, (xp, xs),
                                  norm2[0], routers[0], False)
    y = moe(0, xn, eid)
    hp = _moe_combine(h, y, we, norm2[0], False, 0, np_rows)
    hs = _moe_combine(h, y, we, norm2[0], False, np_rows, ns_rows)

    sg = S5_SAMPLE_SEQS
    hp = hp.reshape(n_p, seq, d).transpose(1, 0, 2).reshape(np_rows, d)
    hs = hs.reshape(n_s // sg, sg, dec_seq, d).transpose(0, 2, 1, 3).reshape(ns_rows, d)
    s5w = _s5_weights(s5_lam_re[0], s5_lam_im[0], s5_log_dt[0], s5_b_re[0], s5_b_im[0], s5_c_re[0], s5_c_im[0])
    n_state = s5_lam_re.shape[1] * s5_lam_re.shape[2]
    zeros = jnp.zeros((n_p, n_state), F32)
    zg_p, s5r_p, s5i_p = _s5_scan(hp, norm1[1], s5w, s5_d[0], zeros, zeros, n_p, S5_PROMPT_STEPS)
    zg_s, s5r_s, s5i_s = _s5_scan(hs, norm1[1], s5w, s5_d[0], state_s5_re[0].reshape(n_s, n_state),
                                  state_s5_im[0].reshape(n_s, n_state), sg, dec_seq)
    h, xn, eid, we = _proj_router([(zg_p, zg_s)], [s5_w_glu[0].astype(BF16)], s5_b_glu[0], (hp, hs),
                                  norm2[1], routers[1], True)
    y = moe(1, xn, eid)
    y_prompt = _moe_combine(h, y, we, norm_f, True, 0, np_rows).reshape(seq, n_p, d).transpose(1, 0, 2)
    y_sample = _moe_combine(h, y, we, norm_f, True, np_rows, ns_rows)
    y_sample = y_sample.reshape(n_s // sg, dec_seq, sg, d).transpose(0, 2, 1, 3).reshape(n_s, dec_seq, d)

    kv_p = lambda a: a.reshape(1, n_p, seq // page, page, H_A, HD_A)
    kv_s = lambda a: a.reshape(1, n_s, dec_seq, H_A, HD_A)
    st = lambda a, n: a.reshape((1, n) + s5_lam_re.shape[1:])
    return (y_prompt, y_sample, kv_p(ka_p), kv_p(va_p), kv_s(ka_s), kv_s(va_s), ret_p[None], ret_s[None],
            st(s5r_p, n_p), st(s5i_p, n_p), st(s5r_s, n_s), st(s5i_s, n_s))
```

```python
import functools
import math

import jax
import jax.numpy as jnp
from jax import lax
from jax.experimental import pallas as pl
from jax.experimental.pallas import tpu as pltpu

F32 = jnp.float32
BF16 = jnp.bfloat16

H_A, HD_A = 8, 64
ROT_DIM = HD_A // 4
ROPE_THETA = 500000.0
MOBA_BLOCK = 256
MOBA_TOPK = 3
H_R, DK_R, DV_R = 8, 64, 128
RET_CHUNK = 128
S5_GROUP, S5_STATE = 16, 64
MOE_GROUPS, MOE_PER_GROUP, MOE_TOPK = 4, 8, 2
N_EXPERTS = MOE_GROUPS * MOE_PER_GROUP
A_W = H_A * HD_A
R_QK = H_R * DK_R
R_V = H_R * DV_R
NEG_INF = -1e30
EPS = 1e-6

LANES = 128
SUBLANES = 8
VMEM_LIMIT = 56 * 1024 * 1024

ROW_TILE = 256
PROJ_TILE = 512
MOE_ROWS = 256


def _nt(a, b):
    return lax.dot_general(a, b, (((1,), (1,)), ((), ())), preferred_element_type=F32)


def _tn(a, b):
    return lax.dot_general(a, b, (((0,), (0,)), ((), ())), preferred_element_type=F32)


def _mm(a, b):
    return jnp.dot(a, b, preferred_element_type=F32)


def _rms_rows(x, g):
    return x * lax.rsqrt(jnp.mean(x * x, axis=-1, keepdims=True) + EPS) * g


def _split_bf16(x):
    hi = x.astype(BF16)
    lo = (x - hi.astype(F32)).astype(BF16)
    return hi, lo


def _params(sem):
    return pltpu.CompilerParams(dimension_semantics=sem, vmem_limit_bytes=VMEM_LIMIT)


def _rotate_into(out_ref, z, c_ref, s_ref, shift, first, scale):
    for j in range(z.shape[1] // LANES):
        sl = slice(j * LANES, (j + 1) * LANES)
        zc = z[:, sl]
        up = pltpu.roll(zc, LANES - shift, axis=1)
        dn = pltpu.roll(zc, shift, axis=1)
        r = zc * c_ref[:, sl] + jnp.where(first, up, dn) * s_ref[:, sl]
        out_ref[:, sl] = r if scale is None else r * scale


def _pick_rows(n_ptiles, p_ref, s_ref):
    return jnp.where(pl.program_id(0) < n_ptiles, p_ref[...], s_ref[...])


def _split_rows(tm, width, n_ptiles):
    return (pl.BlockSpec((tm, width), lambda i: (jnp.minimum(i, n_ptiles - 1), 0)),
            pl.BlockSpec((tm, width), lambda i: (jnp.maximum(i - n_ptiles, 0), 0)))


def _in_proj_kernel(x_ref, g_ref, w_ref, ca_ref, sa_ref, cr_ref, sr_ref,
                    qa_ref, ka_ref, va_ref, qr_ref, kr_ref, vr_ref, gr_ref):
    xn = _rms_rows(x_ref[...], g_ref[...]).astype(BF16)
    lane = lax.broadcasted_iota(jnp.int32, (1, LANES), 1)
    first_a = (lane % HD_A) < (ROT_DIM // 2)
    first_r = (lane % 2) == 0

    def sec(lo, width):
        return _mm(xn, w_ref[:, lo:lo + width])

    _rotate_into(qa_ref, sec(0, A_W), ca_ref, sa_ref, ROT_DIM // 2, first_a, None)
    _rotate_into(ka_ref, sec(A_W, A_W), ca_ref, sa_ref, ROT_DIM // 2, first_a, None)
    va_ref[...] = sec(2 * A_W, A_W)
    _rotate_into(qr_ref, sec(3 * A_W, R_QK), cr_ref, sr_ref, 1, first_r, None)
    _rotate_into(kr_ref, sec(3 * A_W + R_QK, R_QK), cr_ref, sr_ref, 1, first_r, DK_R ** -0.5)
    vr_ref[...] = sec(3 * A_W + 2 * R_QK, R_V)
    gr_ref[...] = sec(3 * A_W + 2 * R_QK + R_V, R_V)


def _rope_tables(pos):
    half = ROT_DIM // 2
    inv = ROPE_THETA ** (-jnp.arange(half, dtype=F32) / half)
    ang = pos.astype(F32)[:, None] * inv[None, :]
    cos, sin = jnp.cos(ang), jnp.sin(ang)
    rest = HD_A - ROT_DIM
    c = jnp.concatenate([cos, cos, jnp.ones((pos.shape[0], rest), F32)], axis=-1)
    s = jnp.concatenate([-sin, sin, jnp.zeros((pos.shape[0], rest), F32)], axis=-1)
    return jnp.tile(c, (1, H_A)), jnp.tile(s, (1, H_A))


def _retnet_tables(pos):
    n = DK_R // 2
    inv = 1.0 / (10000.0 ** jnp.linspace(0.0, 1.0, n, dtype=F32))
    ang = pos.astype(F32)[:, None] * inv[None, :]
    cos, sin = jnp.cos(ang), jnp.sin(ang)
    c = jnp.repeat(cos, 2, axis=-1)
    s = jnp.stack([-sin, sin], axis=-1).reshape(pos.shape[0], DK_R)
    return jnp.tile(c, (1, H_R)), jnp.tile(s, (1, H_R))


def _in_proj(x, g, w_bf16, pos):
    n, d = x.shape
    tm = ROW_TILE
    period_tiles = pos.shape[0] // tm
    ca, sa = _rope_tables(pos)
    cr, sr = _retnet_tables(pos)
    row = lambda w: pl.BlockSpec((tm, w), lambda i: (i, 0))
    tab = pl.BlockSpec((tm, A_W), lambda i: (i % period_tiles, 0))
    widths = (A_W, A_W, A_W, R_QK, R_QK, R_V, R_V)
    return pl.pallas_call(
        _in_proj_kernel,
        grid=(n // tm,),
        in_specs=[row(d), pl.BlockSpec((1, d), lambda i: (0, 0)),
                  pl.BlockSpec(w_bf16.shape, lambda i: (0, 0)), tab, tab, tab, tab],
        out_specs=[row(w) for w in widths],
        out_shape=[jax.ShapeDtypeStruct((n, w), F32) for w in widths],
        compiler_params=_params(("arbitrary",)),
        name="in_proj",
    )(x, g.reshape(1, d), w_bf16, ca, sa, cr, sr)


def _moba_select(q_f32, kmean, n_valid, eye):
    n_blk = kmean.shape[0]
    qh, ql = _split_bf16(q_f32)
    kh, kl = _split_bf16(kmean)
    st = _nt(jnp.concatenate([kh, kl, kh], axis=1), jnp.concatenate([qh, qh, ql], axis=1))
    jrow = lax.broadcasted_iota(jnp.int32, st.shape, 0)
    rank = jnp.zeros(st.shape, F32)
    for jp in range(n_blk):
        sj = st[jp:jp + 1, :]
        beats = (sj > st) | ((sj == st) & (jp < jrow))
        rank = rank + jnp.where(beats & (jp < n_valid), 1.0, 0.0)
    sel_t = jnp.where((jrow < n_valid) & (rank < MOBA_TOPK), 1.0, 0.0)
    sel_t = jnp.concatenate([sel_t, jnp.zeros((LANES - n_blk, st.shape[1]), F32)], axis=0).astype(BF16)
    return _nt(eye, sel_t)


def _moba_prompt_kernel(q_ref, k_ref, v_ref, o_ref, kb_ref, vb_ref, km_ref, bias_ref, s_ref):
    b = pl.program_id(2)
    blk = MOBA_BLOCK
    n_blk = k_ref.shape[0] // blk
    half = blk // 2

    lane = lax.broadcasted_iota(jnp.int32, (1, LANES), 1)
    r_i = lax.broadcasted_iota(jnp.int32, (blk, blk), 0)
    c_i = lax.broadcasted_iota(jnp.int32, (blk, blk), 1)
    heads = range(LANES // HD_A)
    hms = [(lane // HD_A) == hh for hh in heads]

    @pl.when(b == 0)
    def _():
        kb_ref[...] = k_ref[...].astype(BF16)
        vb_ref[...] = v_ref[...].astype(BF16)
        for j in range(n_blk):
            km_ref[j:j + 1, :] = jnp.mean(k_ref[j * blk:(j + 1) * blk, :], axis=0, keepdims=True)
        eye = jnp.where(r_i == c_i, 1.0, 0.0).astype(BF16)
        for hh in heads:
            bias_ref[hh, 0:blk, :] = jnp.full((blk, LANES), NEG_INF, BF16)
        for qb in range(1, n_blk):
            rows = slice(qb * blk, (qb + 1) * blk)
            for hh in heads:
                selq = _moba_select(jnp.where(hms[hh], q_ref[rows, :], 0.0), km_ref[...], qb, eye)
                bias_ref[hh, rows, :] = jnp.where(selq > 0.5, 0.0, NEG_INF).astype(BF16)

    own = pl.ds(pl.multiple_of(b * blk, blk), blk)
    q = q_ref[own, :]
    fold = lambda x: (x[:, :half], x[:, half:])
    q_aug, s_own, mx = [], [], []
    for hh in heads:
        qs = (jnp.where(hms[hh], q, 0.0) * (HD_A ** -0.5)).astype(BF16)
        so = jnp.where(c_i <= r_i, _nt(qs, kb_ref[own, :]), NEG_INF)
        q_aug.append(jnp.concatenate([qs, bias_ref[hh, own, :]], axis=1))
        s_own.append(so)
        mx.append(jnp.maximum(*fold(so)))

    def rows_of(j):
        return pl.ds(pl.multiple_of(j * blk, blk), blk)

    def scores(j, mx):
        onehot = jnp.broadcast_to(jnp.where(lane == j, 1.0, 0.0).astype(BF16), (blk, LANES))
        k_aug = jnp.concatenate([kb_ref[rows_of(j), :], onehot], axis=1)
        out = []
        for hh in heads:
            s = _nt(q_aug[hh], k_aug)
            s_ref[hh, j] = s
            out.append(jnp.maximum(mx[hh], jnp.maximum(*fold(s))))
        return tuple(out)

    mx = lax.fori_loop(0, b, scores, tuple(mx))
    ms = [jnp.max(mx[hh], axis=1, keepdims=True) for hh in heads]

    def weights(s, hh, vj):
        p = jnp.exp(s - ms[hh])
        lo, hi = fold(p)
        return lo + hi, _mm(p.astype(BF16), vj)

    def accumulate(j, carry):
        vj = vb_ref[rows_of(j), :]
        out = []
        for hh in heads:
            l_part, pv = weights(s_ref[hh, j], hh, vj)
            out.append((carry[hh][0] + l_part, carry[hh][1] + pv))
        return tuple(out)

    carry = lax.fori_loop(0, b, accumulate, tuple(weights(s_own[hh], hh, vb_ref[own, :]) for hh in heads))
    out = jnp.zeros(q.shape, F32)
    for hh in heads:
        l_part, acc = carry[hh]
        out = out + jnp.where(hms[hh], acc / jnp.sum(l_part, axis=1, keepdims=True), 0.0)
    o_ref[...] = out


def _moba_prompt(q_a, k_a, v_a, n_seq, seq):
    blk = MOBA_BLOCK
    n_blk = seq // blk
    ospec = pl.BlockSpec((blk, LANES), lambda s, h, b: (s * n_blk + b, h))
    kspec = pl.BlockSpec((seq, LANES), lambda s, h, b: (s, h))
    per = LANES // HD_A
    return pl.pallas_call(
        _moba_prompt_kernel,
        grid=(n_seq, A_W // LANES, n_blk),
        in_specs=[kspec, kspec, kspec],
        out_specs=ospec,
        out_shape=jax.ShapeDtypeStruct((n_seq * seq, A_W), F32),
        scratch_shapes=[pltpu.VMEM((seq, LANES), BF16), pltpu.VMEM((seq, LANES), BF16),
                        pltpu.VMEM((n_blk, LANES), F32), pltpu.VMEM((per, seq, LANES), BF16),
                        pltpu.VMEM((per, n_blk, blk, blk), F32)],
        compiler_params=_params(("arbitrary", "arbitrary", "arbitrary")),
        name="moba_prompt",
    )(q_a, k_a, v_a)


def _ret_decay_tables(c):
    log_g = jnp.log(1.0 - 2.0 ** (-5.0 - jnp.arange(H_R, dtype=F32)))
    i = jnp.arange(c, dtype=F32)
    diff = i[:, None] - i[None, :]
    dmat = jnp.where(diff >= 0, jnp.exp(jnp.maximum(diff, 0.0)[None] * log_g[:, None, None]), 0.0)
    dq = jnp.exp((i + 1.0)[None, :] * log_g[:, None])
    dk = jnp.exp((c - 1.0 - i)[None, :] * log_g[:, None])
    dc = jnp.exp(c * log_g)
    return dmat, dq, dk, dc


def _ret_chunk(q, k, v, g, s, dmat, dq, dk, dc):
    att = _nt(q.astype(BF16), k.astype(BF16)) * dmat
    o = _mm(att.astype(BF16), v.astype(BF16)) + _mm((q * dq).astype(BF16), s.astype(BF16))
    s = s * dc + _tn((k * dk).astype(BF16), v.astype(BF16))
    o = o * lax.rsqrt(jnp.mean(o * o, axis=-1, keepdims=True) + EPS)
    return o * (g * jax.nn.sigmoid(g)), s


def _ret_prompt_kernel(q_ref, k_ref, v_ref, g_ref, dmat_ref, dq_ref, dk_ref, dc_ref, o_ref, st_ref):
    c = RET_CHUNK
    per = LANES // DK_R
    lane = lax.broadcasted_iota(jnp.int32, (1, LANES), 1)

    def chunk(i, states):
        rows = pl.ds(pl.multiple_of(i * c, c), c)
        q_all, k_all = q_ref[rows, :], k_ref[rows, :]
        out = []
        for hh in range(per):
            hm = (lane // DK_R) == hh
            vl = slice(hh * DV_R, (hh + 1) * DV_R)
            o, s = _ret_chunk(jnp.where(hm, q_all, 0.0), jnp.where(hm, k_all, 0.0), v_ref[rows, vl], g_ref[rows, vl],
                              states[hh], dmat_ref[hh], dq_ref[hh], dk_ref[hh], dc_ref[hh])
            o_ref[rows, vl] = o
            out.append(s)
        return tuple(out)

    states = lax.fori_loop(0, q_ref.shape[0] // c, chunk, tuple(jnp.zeros((LANES, DV_R), F32) for _ in range(per)),
                           unroll=2)
    for hh in range(per):
        st_ref[hh] = states[hh][hh * DK_R:(hh + 1) * DK_R, :]


def _ret_tables_bcast(c, rows):
    dmat, dq, dk, dc = _ret_decay_tables(c)
    pad = rows - c
    dmat = jnp.pad(dmat, ((0, 0), (0, pad), (0, pad)))
    dq = jnp.broadcast_to(jnp.pad(dq, ((0, 0), (0, pad)))[:, :, None], (H_R, rows, LANES))
    dk = jnp.broadcast_to(jnp.pad(dk, ((0, 0), (0, pad)))[:, :, None], (H_R, rows, LANES))
    dc = jnp.broadcast_to(dc[:, None, None], (H_R, 1, LANES))
    return dmat, dq, dk, dc


def _ret_prompt(q_r, k_r, v_r, g_r, n_seq, seq):
    c = RET_CHUNK
    dmat, dq, dk, dc = _ret_tables_bcast(c, c)
    per = LANES // DK_R
    qk = pl.BlockSpec((seq, LANES), lambda s, h: (s, h))
    vg = pl.BlockSpec((seq, per * DV_R), lambda s, h: (s, h))
    tab = lambda r: pl.BlockSpec((per, r, LANES), lambda s, h: (h, 0, 0))
    return pl.pallas_call(
        _ret_prompt_kernel,
        grid=(n_seq, H_R // per),
        in_specs=[qk, qk, vg, vg, tab(c), tab(c), tab(c), tab(1)],
        out_specs=[vg, pl.BlockSpec((None, per, DK_R, DV_R), lambda s, h: (s, h, 0, 0))],
        out_shape=[jax.ShapeDtypeStruct((n_seq * seq, R_V), F32),
                   jax.ShapeDtypeStruct((n_seq, H_R, DK_R, DV_R), F32)],
        compiler_params=_params(("arbitrary", "arbitrary")),
        name="ret_prompt",
    )(q_r, k_r, v_r, g_r, dmat, dq, dk, dc)


def _pad_rows(x, rows):
    return jnp.concatenate([x, jnp.zeros((rows - x.shape[0], x.shape[1]), x.dtype)], axis=0)


def _moba_sample_kernel(pt_ref, q_ref, kn_ref, vn_ref, *rest, n_pages):
    k_refs, v_refs = rest[:n_pages], rest[n_pages:2 * n_pages]
    o_ref, s_ref = rest[2 * n_pages], rest[2 * n_pages + 1]
    q = q_ref[...]
    t = q.shape[0]
    page = k_refs[0].shape[-1]
    per_blk = MOBA_BLOCK // page
    n_blk = n_pages // per_blk
    lane = lax.broadcasted_iota(jnp.int32, (1, A_W), 1)
    qbd = jnp.concatenate([jnp.where((lane // HD_A) == h, q, 0.0) for h in range(H_A)], axis=0)
    qs = (qbd * (HD_A ** -0.5)).astype(BF16)
    rows = qbd.shape[0]

    bsum = [jnp.zeros((rows, 1), F32) for _ in range(n_blk)]
    for p in range(n_pages):
        sp = _mm(qs, k_refs[p][...].reshape(A_W, page).astype(BF16))
        s_ref[:, p * page:(p + 1) * page] = sp
        bsum[p // per_blk] = bsum[p // per_blk] + jnp.sum(sp, axis=1, keepdims=True)
    sel = []
    for j in range(n_blk):
        rank = jnp.zeros((rows, 1), F32)
        for jp in range(n_blk):
            if jp != j:
                beats = (bsum[jp] > bsum[j]) | (bsum[jp] == bsum[j]) if jp < j else bsum[jp] > bsum[j]
                rank = rank + jnp.where(beats, 1.0, 0.0)
        sel.append(rank < MOBA_TOPK)

    qi = lax.broadcasted_iota(jnp.int32, (rows, LANES), 0) % t
    causal = lax.broadcasted_iota(jnp.int32, (rows, LANES), 1) <= qi
    s0 = jnp.where(causal, _nt(qs, _pad_rows(kn_ref[...], LANES).astype(BF16)), NEG_INF)
    m = jnp.max(s0, axis=1, keepdims=True)
    p0 = jnp.exp(s0 - m)
    carry = (m, jnp.sum(p0, axis=1, keepdims=True), _mm(p0.astype(BF16), _pad_rows(vn_ref[...], LANES).astype(BF16)))
    for p in range(n_pages):
        m, l, acc = carry
        s = jnp.where(sel[p // per_blk], s_ref[:, p * page:(p + 1) * page], NEG_INF)
        m_new = jnp.maximum(m, jnp.max(s, axis=1, keepdims=True))
        alpha = jnp.exp(m - m_new)
        pr = jnp.exp(s - m_new)
        carry = (m_new, alpha * l + jnp.sum(pr, axis=1, keepdims=True),
                 alpha * acc + _nt(pr.astype(BF16), v_refs[p][...].reshape(A_W, page).astype(BF16)))
    _, l, acc = carry
    o = acc / l
    out = jnp.zeros((t, A_W), F32)
    for h in range(H_A):
        out = out + jnp.where((lane // HD_A) == h, o[h * t:(h + 1) * t, :], 0.0)
    o_ref[...] = out


def _moba_sample(q_a, k_a, v_a, cache_kt, cache_vt, page_table, dec_seq):
    n_s, n_pages = page_table.shape
    new = pl.BlockSpec((dec_seq, A_W), lambda n, pt: (n, 0))
    pages = [pl.BlockSpec((None,) + cache_kt.shape[1:], lambda n, pt, p=p: (pt[n * n_pages + p], 0, 0, 0))
             for p in range(n_pages)]
    grid_spec = pltpu.PrefetchScalarGridSpec(
        num_scalar_prefetch=1,
        grid=(n_s,),
        in_specs=[new, new, new] + pages + pages,
        out_specs=pl.BlockSpec((dec_seq, A_W), lambda n, pt: (n, 0)),
        scratch_shapes=[pltpu.VMEM((H_A * dec_seq, n_pages * cache_kt.shape[-1]), F32)],
    )
    return pl.pallas_call(
        functools.partial(_moba_sample_kernel, n_pages=n_pages),
        grid_spec=grid_spec,
        out_shape=jax.ShapeDtypeStruct((n_s * dec_seq, A_W), F32),
        compiler_params=_params(("arbitrary",)),
        name="moba_sample",
    )(page_table.reshape(-1), q_a, k_a, v_a, *([cache_kt] * n_pages), *([cache_vt] * n_pages))


RET_SAMPLE_SEQS = 4


def _ret_sample_kernel(q_ref, k_ref, v_ref, g_ref, s0_ref, dmat_ref, dq_ref, dk_ref, dc_ref, o_ref, st_ref):
    n_seq = s0_ref.shape[0]
    t = q_ref.shape[0] // n_seq
    lane = lax.broadcasted_iota(jnp.int32, (1, LANES), 1)
    per = LANES // DK_R
    zero_half = jnp.zeros((DK_R, DV_R), F32)
    for i in range(n_seq):
        rows = slice(i * t, (i + 1) * t)
        for h in range(H_R):
            hh = h % per
            hm = (lane // DK_R) == hh
            qk_l = slice((h // per) * LANES, (h // per + 1) * LANES)
            v_l = slice(h * DV_R, (h + 1) * DV_R)
            q = _pad_rows(jnp.where(hm, q_ref[rows, qk_l], 0.0), LANES)
            k = _pad_rows(jnp.where(hm, k_ref[rows, qk_l], 0.0), LANES)
            v = _pad_rows(v_ref[rows, v_l], LANES)
            g = _pad_rows(g_ref[rows, v_l], LANES)
            halves = [zero_half] * per
            halves[hh] = s0_ref[i, h]
            o, s = _ret_chunk(q, k, v, g, jnp.concatenate(halves, axis=0),
                              dmat_ref[h], dq_ref[h], dk_ref[h], dc_ref[h])
            o_ref[rows, v_l] = o[:t, :]
            st_ref[i, h] = s[hh * DK_R:(hh + 1) * DK_R, :]


def _ret_sample(q_r, k_r, v_r, g_r, s0, dec_seq):
    n_s = s0.shape[0]
    ns = RET_SAMPLE_SEQS
    dmat, dq, dk, dc = _ret_tables_bcast(dec_seq, LANES)
    qk = pl.BlockSpec((ns * dec_seq, R_QK), lambda n: (n, 0))
    vg = pl.BlockSpec((ns * dec_seq, R_V), lambda n: (n, 0))
    st = pl.BlockSpec((ns, H_R, DK_R, DV_R), lambda n: (n, 0, 0, 0))
    full = lambda a: pl.BlockSpec(a.shape, lambda n: (0,) * a.ndim)
    return pl.pallas_call(
        _ret_sample_kernel,
        grid=(n_s // ns,),
        in_specs=[qk, qk, vg, vg, st, full(dmat), full(dq), full(dk), full(dc)],
        out_specs=[vg, st],
        out_shape=[jax.ShapeDtypeStruct((n_s * dec_seq, R_V), F32), jax.ShapeDtypeStruct(s0.shape, F32)],
        compiler_params=_params(("arbitrary",)),
        name="ret_sample",
    )(q_r, k_r, v_r, g_r, s0, dmat, dq, dk, dc)


ROUTER_ROWS = SUBLANES + N_EXPERTS


def _route(xn, wr_ref, br_ref, eid_ref, we_ref):
    xh, xl = _split_bf16(xn)
    lt = _nt(wr_ref[...], jnp.concatenate([xh, xh, xl], axis=1)) + br_ref[...]
    tm = lt.shape[1]
    r8 = lax.broadcasted_iota(jnp.int32, (SUBLANES, tm), 0)
    lg = jnp.where(r8 < MOE_GROUPS, lt[:SUBLANES, :], NEG_INF)
    mg = jnp.max(lg, axis=0, keepdims=True)
    wg = 1.0 / jnp.sum(jnp.exp(lg - mg), axis=0, keepdims=True)
    gidx = jnp.min(jnp.where(lg == mg, r8, SUBLANES), axis=0, keepdims=True)
    le = jnp.zeros((MOE_PER_GROUP, tm), F32)
    for gi in range(MOE_GROUPS):
        lo = SUBLANES + gi * MOE_PER_GROUP
        le = le + jnp.where(gidx == gi, lt[lo:lo + MOE_PER_GROUP, :], 0.0)
    v1 = jnp.max(le, axis=0, keepdims=True)
    i1 = jnp.min(jnp.where(le == v1, r8, MOE_PER_GROUP), axis=0, keepdims=True)
    le2 = jnp.where(r8 == i1, -jnp.inf, le)
    v2 = jnp.max(le2, axis=0, keepdims=True)
    i2 = jnp.min(jnp.where(le2 == v2, r8, MOE_PER_GROUP), axis=0, keepdims=True)
    e21 = jnp.exp(v2 - v1)
    w1 = wg / (1.0 + e21)
    eid_ref[0:1, :] = gidx * MOE_PER_GROUP + i1
    eid_ref[1:2, :] = gidx * MOE_PER_GROUP + i2
    we_ref[0:1, :] = w1
    we_ref[1:2, :] = w1 * e21


def _proj_router_kernel(*refs, n_in, glu, n_ptiles):
    a_refs, w_refs = refs[:2 * n_in], refs[2 * n_in:3 * n_in]
    k = 3 * n_in
    b_ref = refs[k] if glu else None
    k += int(glu)
    hp_ref, hs_ref, g_ref, wr_ref, br_ref, ho_ref, xn_ref, eid_ref, we_ref = refs[k:k + 9]
    acc = None
    for j, w_ref in enumerate(w_refs):
        a = _pick_rows(n_ptiles, a_refs[2 * j], a_refs[2 * j + 1])
        part = _mm(a.astype(BF16), w_ref[...])
        acc = part if acc is None else acc + part
    if glu:
        acc = acc + b_ref[...]
        half = acc.shape[1] // 2
        acc = acc[:, :half] * jax.nn.sigmoid(acc[:, half:])
    hn = _pick_rows(n_ptiles, hp_ref, hs_ref) + acc
    ho_ref[...] = hn
    xn = _rms_rows(hn, g_ref[...])
    xn_ref[...] = xn
    _route(xn, wr_ref, br_ref, eid_ref, we_ref)


def _router_weights(w_group, b_group, w_expert, b_expert):
    d = w_group.shape[0]
    wt = jnp.concatenate([w_group.T, jnp.zeros((SUBLANES - MOE_GROUPS, d), F32), w_expert.T], axis=0)
    hi = wt.astype(BF16)
    lo = (wt - hi.astype(F32)).astype(BF16)
    bias = jnp.concatenate([b_group, jnp.zeros((SUBLANES - MOE_GROUPS,), F32), b_expert]).reshape(-1, 1)
    return jnp.concatenate([hi, lo, hi], axis=1), bias


def _proj_router(acts, weights, bias, h, g, router, glu):
    d = h[0].shape[1]
    n = h[0].shape[0] + h[1].shape[0]
    tm = PROJ_TILE
    n_ptiles = h[0].shape[0] // tm
    wr, br = router
    row = lambda w: pl.BlockSpec((tm, w), lambda i: (i, 0))
    full = lambda a: pl.BlockSpec(a.shape, lambda i: (0,) * a.ndim)
    tok = pl.BlockSpec((MOE_TOPK, tm), lambda i: (0, i))
    operands = ([a for pair in acts for a in pair] + list(weights) + ([bias.reshape(1, -1)] if glu else [])
                + [h[0], h[1], g.reshape(1, d), wr, br])
    in_specs = ([spec for pair in acts for spec in _split_rows(tm, pair[0].shape[1], n_ptiles)]
                + [full(w) for w in weights]
                + ([pl.BlockSpec((1, bias.shape[0]), lambda i: (0, 0))] if glu else [])
                + [*_split_rows(tm, d, n_ptiles), pl.BlockSpec((1, d), lambda i: (0, 0)), full(wr), full(br)])
    return pl.pallas_call(
        functools.partial(_proj_router_kernel, n_in=len(acts), glu=glu, n_ptiles=n_ptiles),
        grid=(n // tm,),
        in_specs=in_specs,
        out_specs=[row(d), row(d), tok, tok],
        out_shape=[jax.ShapeDtypeStruct((n, d), F32), jax.ShapeDtypeStruct((n, d), F32),
                   jax.ShapeDtypeStruct((MOE_TOPK, n), jnp.int32), jax.ShapeDtypeStruct((MOE_TOPK, n), F32)],
        compiler_params=_params(("arbitrary",)),
        name="glu_router" if glu else "out_proj_router",
    )(*operands)


def _moe_plan(eid):
    n_tok = eid.shape[1]
    n_assign = MOE_TOPK * n_tok
    blk = MOE_ROWS
    n_blocks = -(-n_assign // blk) + N_EXPERTS
    e_flat = eid.reshape(-1)
    experts = jnp.arange(N_EXPERTS, dtype=jnp.int32)
    counts = jnp.sum((e_flat[:, None] == experts[None, :]).astype(jnp.int32), axis=0)
    order = jnp.argsort(e_flat).astype(jnp.int32)
    pc = (counts + blk - 1) // blk * blk
    pend = jnp.cumsum(pc)
    pstart = pend - pc
    start = jnp.cumsum(counts) - counts
    n_steps = n_blocks + 2
    first_row = jnp.arange(n_steps, dtype=jnp.int32) * blk
    blk_e = jnp.minimum(jnp.sum((pend[None, :] <= first_row[:, None]).astype(jnp.int32), axis=1), N_EXPERTS - 1)
    lane = jnp.arange(blk, dtype=jnp.int32)[None, :]
    off = first_row[:, None] + lane - pstart[blk_e][:, None]
    valid = off < counts[blk_e][:, None]
    a_row = order[jnp.clip(start[blk_e][:, None] + off, 0, n_assign - 1)]
    a_row = jnp.where(valid, a_row, 0)
    src = a_row % n_tok
    dst = jnp.concatenate([n_assign + lane, jnp.where(valid, a_row, n_assign + lane)], axis=0)
    n_used = (pend[-1] // blk).astype(jnp.int32).reshape(1)
    return blk_e, n_used, src.reshape(-1), dst.reshape(-1), n_steps


def _moe_kernel(blk_e_ref, n_used_ref, src_ref, dst_ref, x_hbm, w13_ref, w2_ref, y_hbm,
                xbuf, ybuf, w13b, w2b, gsem, ssem):
    b = pl.program_id(0)
    n_used = n_used_ref[0]
    blk = MOE_ROWS
    slot = b % 2

    def gather_row(block, slot, r):
        return pltpu.make_async_copy(x_hbm.at[pl.ds(src_ref[block * blk + r], 1), :],
                                     xbuf.at[slot, pl.ds(r, 1), :], gsem.at[slot])

    def scatter_row(block, slot, r):
        return pltpu.make_async_copy(ybuf.at[slot, pl.ds(r, 1), :],
                                     y_hbm.at[pl.ds(dst_ref[block * blk + r], 1), :], ssem.at[slot])

    def gather_all(slot):
        return pltpu.make_async_copy(x_hbm.at[pl.ds(0, blk), :], xbuf.at[slot], gsem.at[slot])

    def scatter_all(slot):
        return pltpu.make_async_copy(ybuf.at[slot], y_hbm.at[pl.ds(0, blk), :], ssem.at[slot])

    def issue_row_copies():
        for r in range(blk):
            gather_row(b + 1, 1 - slot, r).start(priority=r % 2)
        for r in range(blk):
            scatter_row(b, 1 - slot, r).start(priority=r % 2)

    @pl.when(b == 0)
    def _():
        xbuf[...] = jnp.zeros(xbuf.shape, xbuf.dtype)
        ybuf[...] = jnp.zeros(ybuf.shape, ybuf.dtype)
        fill = pltpu.make_async_copy(ybuf.at[0], y_hbm.at[pl.ds(y_hbm.shape[0] - blk, blk), :], ssem.at[0])
        fill.start()
        fill.wait()
        for r in range(blk):
            gather_row(0, 0, r).start(priority=r % 2)

    @pl.when(b <= n_used)
    def _():
        gather_all(slot).wait()

    @pl.when((b >= 1) & (b <= n_used + 1))
    def _():
        scatter_all(slot).wait()

    @pl.when(b < n_used)
    def _():
        @pl.when((b == 0) | (blk_e_ref[b] != blk_e_ref[jnp.maximum(b - 1, 0)]))
        def _():
            w13b[...] = w13_ref[...].astype(BF16)
            w2b[...] = w2_ref[...].astype(BF16)

        issue_row_copies()
        hb = _mm(xbuf[slot].astype(BF16), w13b[...])
        half = hb.shape[1] // 2
        gate = hb[:, :half]
        act = gate * jax.nn.sigmoid(gate) * hb[:, half:]
        ybuf[slot] = _mm(act.astype(BF16), w2b[...])

    @pl.when(b == n_used)
    def _():
        for r in range(blk):
            scatter_row(b, 1 - slot, r).start(priority=r % 2)


def _moe_experts(xn, eid, w13, w2, layer):
    n_tok, d = xn.shape
    blk = MOE_ROWS
    blk_e, n_used, src_tok, dst_row, n_steps = _moe_plan(eid)
    grid_spec = pltpu.PrefetchScalarGridSpec(
        num_scalar_prefetch=4,
        grid=(n_steps,),
        in_specs=[pl.BlockSpec(memory_space=pl.ANY),
                  pl.BlockSpec((None, None) + w13.shape[2:], lambda b, be, *_: (layer, be[b], 0, 0)),
                  pl.BlockSpec((None, None) + w2.shape[2:], lambda b, be, *_: (layer, be[b], 0, 0))],
        out_specs=pl.BlockSpec(memory_space=pl.ANY),
        scratch_shapes=[pltpu.VMEM((2, blk, d), F32), pltpu.VMEM((2, blk, d), F32),
                        pltpu.VMEM(w13.shape[2:], BF16), pltpu.VMEM(w2.shape[2:], BF16),
                        pltpu.SemaphoreType.DMA((2,)), pltpu.SemaphoreType.DMA((2,))],
    )
    return pl.pallas_call(
        _moe_kernel,
        grid_spec=grid_spec,
        out_shape=jax.ShapeDtypeStruct((MOE_TOPK * n_tok + blk, d), F32),
        compiler_params=_params(("arbitrary",)),
        name="moe_experts",
    )(blk_e, n_used, src_tok, dst_row, xn, w13, w2)


def _combine_kernel(h_ref, y0_ref, y1_ref, w_ref, g_ref, o_ref, *, norm):
    w = w_ref[...]
    h = h_ref[...] + (y0_ref[...] * w[:, 0:1] + y1_ref[...] * w[:, 1:2])
    o_ref[...] = _rms_rows(h, g_ref[...]) if norm else h


def _moe_combine(h, y, we, g, norm, row_lo, n_rows):
    n, d = h.shape
    tm = ROW_TILE
    lo = row_lo // tm
    return pl.pallas_call(
        functools.partial(_combine_kernel, norm=norm),
        grid=(n_rows // tm,),
        in_specs=[pl.BlockSpec((tm, d), lambda i: (lo + i, 0)), pl.BlockSpec((tm, d), lambda i: (lo + i, 0)),
                  pl.BlockSpec((tm, d), lambda i: (n // tm + lo + i, 0)),
                  pl.BlockSpec((tm, MOE_TOPK), lambda i: (lo + i, 0)), pl.BlockSpec((1, d), lambda i: (0, 0))],
        out_specs=pl.BlockSpec((tm, d), lambda i: (i, 0)),
        out_shape=jax.ShapeDtypeStruct((n_rows, d), F32),
        compiler_params=_params(("arbitrary",)),
        name="moe_combine_norm" if norm else "moe_combine",
    )(h, y, y, we.T, g.reshape(1, d))


S5_CH = LANES
S5_ST = (LANES // S5_GROUP) * S5_STATE
S5_SCAN_LANES = 512


def _s5_kernel(h_ref, g_ref, bre_ref, bim_ref, are_ref, aim_ref, cre_ref, cim_ref, d_ref, s0r_ref, s0i_ref,
               zg_ref, fr_ref, fi_ref, xr, xi, sr, si, *, s, steps):
    c = pl.program_id(1)

    @pl.when(c == 0)
    def _():
        sr[...] = s0r_ref[...]
        si[...] = s0i_ref[...]

    u = _rms_rows(h_ref[...], g_ref[...])
    ub = u.astype(BF16)
    n_ch = u.shape[1] // S5_CH
    for j in range(n_ch):
        uj = ub[:, j * S5_CH:(j + 1) * S5_CH]
        xr[:, j * S5_ST:(j + 1) * S5_ST] = _mm(uj, bre_ref[j])
        xi[:, j * S5_ST:(j + 1) * S5_ST] = _mm(uj, bim_ref[j])

    for lo in range(0, xr.shape[1], S5_SCAN_LANES):
        ls = slice(lo, lo + S5_SCAN_LANES)
        ar = jnp.broadcast_to(are_ref[:, ls], (s, S5_SCAN_LANES))
        ai = jnp.broadcast_to(aim_ref[:, ls], (s, S5_SCAN_LANES))

        pr, pi = sr[:, ls], si[:, ls]
        for t in range(steps):
            rows = slice(t * s, (t + 1) * s)
            pr, pi = ar * pr - ai * pi + xr[rows, ls], ar * pi + ai * pr + xi[rows, ls]
            xr[rows, ls] = pr
            xi[rows, ls] = pi
        sr[:, ls] = pr
        si[:, ls] = pi

    for j in range(n_ch):
        cs = slice(j * S5_CH, (j + 1) * S5_CH)
        ss = slice(j * S5_ST, (j + 1) * S5_ST)
        y = _mm(xr[:, ss].astype(BF16), cre_ref[j]) - _mm(xi[:, ss].astype(BF16), cim_ref[j])
        zg_ref[:, cs] = jax.nn.gelu(y + d_ref[:, cs] * u[:, cs])

    @pl.when(c == pl.num_programs(1) - 1)
    def _():
        fr_ref[...] = sr[...]
        fi_ref[...] = si[...]


def _s5_weights(lam_re, lam_im, log_dt, b_re, b_im, c_re, c_im):
    dt = jnp.exp(log_dt)[:, None]
    mag = jnp.exp(lam_re * dt)
    ang = lam_im * dt
    ab_re = mag * jnp.cos(ang)
    ab_im = mag * jnp.sin(ang)
    den = lam_re * lam_re + lam_im * lam_im
    nr = ab_re - 1.0
    co_re = (nr * lam_re + ab_im * lam_im) / den
    co_im = (ab_im * lam_re - nr * lam_im) / den
    bb_re = co_re[..., None] * b_re - co_im[..., None] * b_im
    bb_im = co_re[..., None] * b_im + co_im[..., None] * b_re
    per = S5_CH // S5_GROUP
    n_ch = lam_re.shape[0] // per
    eye = jnp.eye(per, dtype=F32)

    def in_blocks(bb):
        w = bb.transpose(0, 2, 1).reshape(n_ch, per, S5_GROUP, S5_STATE)
        return jnp.einsum('jgcp,gh->jgchp', w, eye).reshape(n_ch, S5_CH, S5_ST).astype(BF16)

    def out_blocks(cc):
        w = cc.transpose(0, 2, 1).reshape(n_ch, per, S5_STATE, S5_GROUP)
        return jnp.einsum('jgpc,gh->jgphc', w, eye).reshape(n_ch, S5_ST, S5_CH).astype(BF16)

    return (in_blocks(bb_re), in_blocks(bb_im), ab_re.reshape(1, -1), ab_im.reshape(1, -1),
            out_blocks(c_re), out_blocks(c_im))


def _s5_scan(h, g, weights, d_skip, s0_re, s0_im, s, steps):
    n, d = h.shape
    n_groups = s0_re.shape[0] // s
    rows = steps * s
    chunks = n // (n_groups * rows)
    bre, bim, are, aim, cre, cim = weights
    n_state = are.shape[1]
    full = lambda a: pl.BlockSpec(a.shape, lambda gi, c: (0,) * a.ndim)
    row = pl.BlockSpec((rows, d), lambda gi, c: (gi * chunks + c, 0))
    state = pl.BlockSpec((s, n_state), lambda gi, c: (gi, 0))
    return pl.pallas_call(
        functools.partial(_s5_kernel, s=s, steps=steps),
        grid=(n_groups, chunks),
        in_specs=[row, pl.BlockSpec((1, d), lambda gi, c: (0, 0)), full(bre), full(bim), full(are), full(aim),
                  full(cre), full(cim), pl.BlockSpec((1, d), lambda gi, c: (0, 0)), state, state],
        out_specs=[row, state, state],
        out_shape=[jax.ShapeDtypeStruct((n, d), F32), jax.ShapeDtypeStruct(s0_re.shape, F32),
                   jax.ShapeDtypeStruct(s0_re.shape, F32)],
        scratch_shapes=[pltpu.VMEM((rows, n_state), F32), pltpu.VMEM((rows, n_state), F32),
                        pltpu.VMEM((s, n_state), F32), pltpu.VMEM((s, n_state), F32)],
        compiler_params=_params(("arbitrary", "arbitrary")),
        name="s5_scan",
    )(h, g.reshape(1, d), bre, bim, are, aim, cre, cim, d_skip.reshape(1, d), s0_re, s0_im)


S5_PROMPT_STEPS = 32
S5_SAMPLE_SEQS = 32


def kernel(x_prompt, x_sample, cache_k, cache_v, state_ret, state_s5_re, state_s5_im, page_table, norm1, norm2, norm_f, w_in_even, w_out_even, s5_lam_re, s5_lam_im, s5_log_dt, s5_b_re, s5_b_im, s5_c_re, s5_c_im, s5_d, s5_w_glu, s5_b_glu, moe_w_group, moe_b_group, moe_w_expert, moe_b_expert, moe_w13, moe_w2):
    n_p, seq, d = x_prompt.shape
    n_s, dec_seq, _ = x_sample.shape
    pool, page = cache_k.shape[1], cache_k.shape[2]
    past_len = page_table.shape[1] * page
    np_rows, ns_rows = n_p * seq, n_s * dec_seq
    routers = [_router_weights(moe_w_group[li], moe_b_group[li], moe_w_expert[li], moe_b_expert[li])
               for li in range(2)]

    def moe(li, xn, eid):
        return _moe_experts(xn, eid, moe_w13, moe_w2, li)

    xp, xs = x_prompt.reshape(np_rows, d), x_sample.reshape(ns_rows, d)
    w_in = w_in_even[0].astype(BF16)
    qa_p, ka_p, va_p, qr_p, kr_p, vr_p, gr_p = _in_proj(xp, norm1[0], w_in, jnp.arange(seq))
    qa_s, ka_s, va_s, qr_s, kr_s, vr_s, gr_s = _in_proj(xs, norm1[0], w_in,
                                                        past_len + jnp.arange(ROW_TILE) % dec_seq)
    oa_p = _moba_prompt(qa_p, ka_p, va_p, n_p, seq)
    or_p, ret_p = _ret_prompt(qr_p, kr_p, vr_p, gr_p, n_p, seq)
    oa_s = _moba_sample(qa_s, ka_s, va_s, cache_k[0].transpose(0, 2, 3, 1), cache_v[0].transpose(0, 2, 3, 1),
                        page_table, dec_seq)
    or_s, ret_s = _ret_sample(qr_s, kr_s, vr_s, gr_s, state_ret[0], dec_seq)
    w_out = w_out_even[0].astype(BF16)
    h, xn, eid, we = _proj_router([(oa_p, oa_s), (or_p, or_s)], [w_out[:A_W], w_out[A_W:]], None, (xp, xs),
                                  norm2[0], routers[0], False)
    y = moe(0, xn, eid)
    hp = _moe_combine(h, y, we, norm2[0], False, 0, np_rows)
    hs = _moe_combine(h, y, we, norm2[0], False, np_rows, ns_rows)

    sg = S5_SAMPLE_SEQS
    hp = hp.reshape(n_p, seq, d).transpose(1, 0, 2).reshape(np_rows, d)
    hs = hs.reshape(n_s // sg, sg, dec_seq, d).transpose(0, 2, 1, 3).reshape(ns_rows, d)
    s5w = _s5_weights(s5_lam_re[0], s5_lam_im[0], s5_log_dt[0], s5_b_re[0], s5_b_im[0], s5_c_re[0], s5_c_im[0])
    n_state = s5_lam_re.shape[1] * s5_lam_re.shape[2]
    zeros = jnp.zeros((n_p, n_state), F32)
    zg_p, s5r_p, s5i_p = _s5_scan(hp, norm1[1], s5w, s5_d[0], zeros, zeros, n_p, S5_PROMPT_STEPS)
    zg_s, s5r_s, s5i_s = _s5_scan(hs, norm1[1], s5w, s5_d[0], state_s5_re[0].reshape(n_s, n_state),
                                  state_s5_im[0].reshape(n_s, n_state), sg, dec_seq)
    h, xn, eid, we = _proj_router([(zg_p, zg_s)], [s5_w_glu[0].astype(BF16)], s5_b_glu[0], (hp, hs),
                                  norm2[1], routers[1], True)
    y = moe(1, xn, eid)
    y_prompt = _moe_combine(h, y, we, norm_f, True, 0, np_rows).reshape(seq, n_p, d).transpose(1, 0, 2)
    y_sample = _moe_combine(h, y, we, norm_f, True, np_rows, ns_rows)
    y_sample = y_sample.reshape(n_s // sg, dec_seq, sg, d).transpose(0, 2, 1, 3).reshape(n_s, dec_seq, d)

    kv_p = lambda a: a.reshape(1, n_p, seq // page, page, H_A, HD_A)
    kv_s = lambda a: a.reshape(1, n_s, dec_seq, H_A, HD_A)
    st = lambda a, n: a.reshape((1, n) + s5_lam_re.shape[1:])
    return (y_prompt, y_sample, kv_p(ka_p), kv_p(va_p), kv_s(ka_s), kv_s(va_s), ret_p[None], ret_s[None],
            st(s5r_p, n_p), st(s5i_p, n_p), st(s5r_s, n_s), st(s5i_s, n_s))
```

```python
import functools
import math

import jax
import jax.numpy as jnp
from jax import lax
from jax.experimental import pallas as pl
from jax.experimental.pallas import tpu as pltpu

F32 = jnp.float32
BF16 = jnp.bfloat16

H_A, HD_A = 8, 64
ROT_DIM = HD_A // 4
ROPE_THETA = 500000.0
MOBA_BLOCK = 256
MOBA_TOPK = 3
H_R, DK_R, DV_R = 8, 64, 128
RET_CHUNK = 128
S5_GROUP, S5_STATE = 16, 64
MOE_GROUPS, MOE_PER_GROUP, MOE_TOPK = 4, 8, 2
N_EXPERTS = MOE_GROUPS * MOE_PER_GROUP
A_W = H_A * HD_A
R_QK = H_R * DK_R
R_V = H_R * DV_R
NEG_INF = -1e30
EPS = 1e-6

LANES = 128
SUBLANES = 8
VMEM_LIMIT = 56 * 1024 * 1024

ROW_TILE = 256
PROJ_TILE = 512
MOE_ROWS = 256
TOKEN_SUBLANES = 8


def _nt(a, b):
    return lax.dot_general(a, b, (((1,), (1,)), ((), ())), preferred_element_type=F32)


def _tn(a, b):
    return lax.dot_general(a, b, (((0,), (0,)), ((), ())), preferred_element_type=F32)


def _mm(a, b):
    return jnp.dot(a, b, preferred_element_type=F32)


def _rms_rows(x, g):
    return x * lax.rsqrt(jnp.mean(x * x, axis=-1, keepdims=True) + EPS) * g


def _split_bf16(x):
    hi = x.astype(BF16)
    lo = (x - hi.astype(F32)).astype(BF16)
    return hi, lo


def _params(sem):
    return pltpu.CompilerParams(dimension_semantics=sem, vmem_limit_bytes=VMEM_LIMIT)


def _rotate_into(out_ref, z, c_ref, s_ref, shift, first, scale):
    for j in range(z.shape[1] // LANES):
        sl = slice(j * LANES, (j + 1) * LANES)
        zc = z[:, sl]
        up = pltpu.roll(zc, LANES - shift, axis=1)
        dn = pltpu.roll(zc, shift, axis=1)
        r = zc * c_ref[:, sl] + jnp.where(first, up, dn) * s_ref[:, sl]
        out_ref[:, sl] = r if scale is None else r * scale


def _pick_rows(n_ptiles, p_ref, s_ref):
    return jnp.where(pl.program_id(0) < n_ptiles, p_ref[...], s_ref[...])


def _split_rows(tm, width, n_ptiles):
    return (pl.BlockSpec((tm, width), lambda i: (jnp.minimum(i, n_ptiles - 1), 0)),
            pl.BlockSpec((tm, width), lambda i: (jnp.maximum(i - n_ptiles, 0), 0)))


def _store_pages_t(src_ref, dst_ref):
    n_pages, _, _, page = dst_ref.shape
    per = LANES // HD_A
    for j in range(A_W // LANES):
        t = src_ref[:, j * LANES:(j + 1) * LANES].T
        for p in range(n_pages):
            dst_ref[p, j * per:(j + 1) * per] = t[:, p * page:(p + 1) * page].reshape(per, HD_A, page)


def _in_proj_kernel(x_ref, g_ref, w_ref, ca_ref, sa_ref, cr_ref, sr_ref,
                    qa_ref, ka_ref, va_ref, qr_ref, kr_ref, vr_ref, gr_ref, *page_refs):
    xn = _rms_rows(x_ref[...], g_ref[...]).astype(BF16)
    lane = lax.broadcasted_iota(jnp.int32, (1, LANES), 1)
    first_a = (lane % HD_A) < (ROT_DIM // 2)
    first_r = (lane % 2) == 0

    def sec(lo, width):
        return _mm(xn, w_ref[:, lo:lo + width])

    _rotate_into(qa_ref, sec(0, A_W), ca_ref, sa_ref, ROT_DIM // 2, first_a, None)
    _rotate_into(ka_ref, sec(A_W, A_W), ca_ref, sa_ref, ROT_DIM // 2, first_a, None)
    va_ref[...] = sec(2 * A_W, A_W)
    _rotate_into(qr_ref, sec(3 * A_W, R_QK), cr_ref, sr_ref, 1, first_r, None)
    _rotate_into(kr_ref, sec(3 * A_W + R_QK, R_QK), cr_ref, sr_ref, 1, first_r, DK_R ** -0.5)
    vr_ref[...] = sec(3 * A_W + 2 * R_QK, R_V)
    gr_ref[...] = sec(3 * A_W + 2 * R_QK + R_V, R_V)
    if page_refs:
        _store_pages_t(ka_ref, page_refs[0])
        _store_pages_t(va_ref, page_refs[1])


def _rope_tables(pos):
    half = ROT_DIM // 2
    inv = ROPE_THETA ** (-jnp.arange(half, dtype=F32) / half)
    ang = pos.astype(F32)[:, None] * inv[None, :]
    cos, sin = jnp.cos(ang), jnp.sin(ang)
    rest = HD_A - ROT_DIM
    c = jnp.concatenate([cos, cos, jnp.ones((pos.shape[0], rest), F32)], axis=-1)
    s = jnp.concatenate([-sin, sin, jnp.zeros((pos.shape[0], rest), F32)], axis=-1)
    return jnp.tile(c, (1, H_A)), jnp.tile(s, (1, H_A))


def _retnet_tables(pos):
    n = DK_R // 2
    inv = 1.0 / (10000.0 ** jnp.linspace(0.0, 1.0, n, dtype=F32))
    ang = pos.astype(F32)[:, None] * inv[None, :]
    cos, sin = jnp.cos(ang), jnp.sin(ang)
    c = jnp.repeat(cos, 2, axis=-1)
    s = jnp.stack([-sin, sin], axis=-1).reshape(pos.shape[0], DK_R)
    return jnp.tile(c, (1, H_R)), jnp.tile(s, (1, H_R))


def _in_proj(x, g, w_bf16, pos, page=None):
    n, d = x.shape
    tm = ROW_TILE
    period_tiles = pos.shape[0] // tm
    ca, sa = _rope_tables(pos)
    cr, sr = _retnet_tables(pos)
    row = lambda w: pl.BlockSpec((tm, w), lambda i: (i, 0))
    tab = pl.BlockSpec((tm, A_W), lambda i: (i % period_tiles, 0))
    widths = (A_W, A_W, A_W, R_QK, R_QK, R_V, R_V)
    out_specs = [row(w) for w in widths]
    out_shape = [jax.ShapeDtypeStruct((n, w), F32) for w in widths]
    if page is not None:
        out_specs += [pl.BlockSpec((tm // page, H_A, HD_A, page), lambda i: (i, 0, 0, 0))] * 2
        out_shape += [jax.ShapeDtypeStruct((n // page, H_A, HD_A, page), F32)] * 2
    return pl.pallas_call(
        _in_proj_kernel,
        grid=(n // tm,),
        in_specs=[row(d), pl.BlockSpec((1, d), lambda i: (0, 0)),
                  pl.BlockSpec(w_bf16.shape, lambda i: (0, 0)), tab, tab, tab, tab],
        out_specs=out_specs,
        out_shape=out_shape,
        compiler_params=_params(("arbitrary",)),
        name="in_proj",
    )(x, g.reshape(1, d), w_bf16, ca, sa, cr, sr)


def _moba_select(q_f32, kmean, n_valid, eye):
    n_blk = kmean.shape[0]
    qh, ql = _split_bf16(q_f32)
    kh, kl = _split_bf16(kmean)
    st = _nt(jnp.concatenate([kh, kl, kh], axis=1), jnp.concatenate([qh, qh, ql], axis=1))
    jrow = lax.broadcasted_iota(jnp.int32, st.shape, 0)
    rank = jnp.zeros(st.shape, F32)
    for jp in range(n_blk):
        sj = st[jp:jp + 1, :]
        beats = (sj > st) | ((sj == st) & (jp < jrow))
        rank = rank + jnp.where(beats & (jp < n_valid), 1.0, 0.0)
    sel_t = jnp.where((jrow < n_valid) & (rank < MOBA_TOPK), 1.0, 0.0)
    sel_t = jnp.concatenate([sel_t, jnp.zeros((LANES - n_blk, st.shape[1]), F32)], axis=0).astype(BF16)
    return _nt(eye, sel_t)


def _moba_prompt_kernel(q_ref, k_ref, v_ref, o_ref, kb_ref, vb_ref, km_ref, bias_ref, s_ref):
    b = pl.program_id(2)
    blk = MOBA_BLOCK
    n_blk = k_ref.shape[0] // blk
    half = blk // 2

    lane = lax.broadcasted_iota(jnp.int32, (1, LANES), 1)
    r_i = lax.broadcasted_iota(jnp.int32, (blk, blk), 0)
    c_i = lax.broadcasted_iota(jnp.int32, (blk, blk), 1)
    heads = range(LANES // HD_A)
    hms = [(lane // HD_A) == hh for hh in heads]

    @pl.when(b == 0)
    def _():
        kb_ref[...] = k_ref[...].astype(BF16)
        vb_ref[...] = v_ref[...].astype(BF16)
        for j in range(n_blk):
            km_ref[j:j + 1, :] = jnp.mean(k_ref[j * blk:(j + 1) * blk, :], axis=0, keepdims=True)
        eye = jnp.where(r_i == c_i, 1.0, 0.0).astype(BF16)
        for hh in heads:
            bias_ref[hh, 0:blk, :] = jnp.full((blk, LANES), NEG_INF, BF16)
        for qb in range(1, n_blk):
            rows = slice(qb * blk, (qb + 1) * blk)
            for hh in heads:
                selq = _moba_select(jnp.where(hms[hh], q_ref[rows, :], 0.0), km_ref[...], qb, eye)
                bias_ref[hh, rows, :] = jnp.where(selq > 0.5, 0.0, NEG_INF).astype(BF16)

    own = pl.ds(pl.multiple_of(b * blk, blk), blk)
    q = q_ref[own, :]
    fold = lambda x: (x[:, :half], x[:, half:])
    q_aug, s_own, mx = [], [], []
    for hh in heads:
        qs = (jnp.where(hms[hh], q, 0.0) * (HD_A ** -0.5)).astype(BF16)
        so = jnp.where(c_i <= r_i, _nt(qs, kb_ref[own, :]), NEG_INF)
        q_aug.append(jnp.concatenate([qs, bias_ref[hh, own, :]], axis=1))
        s_own.append(so)
        mx.append(jnp.maximum(*fold(so)))

    def rows_of(j):
        return pl.ds(pl.multiple_of(j * blk, blk), blk)

    def scores(j, mx):
        onehot = jnp.broadcast_to(jnp.where(lane == j, 1.0, 0.0).astype(BF16), (blk, LANES))
        k_aug = jnp.concatenate([kb_ref[rows_of(j), :], onehot], axis=1)
        out = []
        for hh in heads:
            s = _nt(q_aug[hh], k_aug)
            s_ref[hh, j] = s
            out.append(jnp.maximum(mx[hh], jnp.maximum(*fold(s))))
        return tuple(out)

    mx = lax.fori_loop(0, b, scores, tuple(mx))
    ms = [jnp.max(mx[hh], axis=1, keepdims=True) for hh in heads]

    def weights(s, hh, vj):
        p = jnp.exp(s - ms[hh])
        lo, hi = fold(p)
        return lo + hi, _mm(p.astype(BF16), vj)

    def accumulate(j, carry):
        vj = vb_ref[rows_of(j), :]
        out = []
        for hh in heads:
            l_part, pv = weights(s_ref[hh, j], hh, vj)
            out.append((carry[hh][0] + l_part, carry[hh][1] + pv))
        return tuple(out)

    carry = lax.fori_loop(0, b, accumulate, tuple(weights(s_own[hh], hh, vb_ref[own, :]) for hh in heads))
    out = jnp.zeros(q.shape, F32)
    for hh in heads:
        l_part, acc = carry[hh]
        out = out + jnp.where(hms[hh], acc / jnp.sum(l_part, axis=1, keepdims=True), 0.0)
    o_ref[...] = out


def _moba_prompt(q_a, k_a, v_a, n_seq, seq):
    blk = MOBA_BLOCK
    n_blk = seq // blk
    ospec = pl.BlockSpec((blk, LANES), lambda s, h, b: (s * n_blk + b, h))
    kspec = pl.BlockSpec((seq, LANES), lambda s, h, b: (s, h))
    per = LANES // HD_A
    return pl.pallas_call(
        _moba_prompt_kernel,
        grid=(n_seq, A_W // LANES, n_blk),
        in_specs=[kspec, kspec, kspec],
        out_specs=ospec,
        out_shape=jax.ShapeDtypeStruct((n_seq * seq, A_W), F32),
        scratch_shapes=[pltpu.VMEM((seq, LANES), BF16), pltpu.VMEM((seq, LANES), BF16),
                        pltpu.VMEM((n_blk, LANES), F32), pltpu.VMEM((per, seq, LANES), BF16),
                        pltpu.VMEM((per, n_blk, blk, blk), F32)],
        compiler_params=_params(("arbitrary", "arbitrary", "arbitrary")),
        name="moba_prompt",
    )(q_a, k_a, v_a)


def _ret_decay_tables(c):
    log_g = jnp.log(1.0 - 2.0 ** (-5.0 - jnp.arange(H_R, dtype=F32)))
    i = jnp.arange(c, dtype=F32)
    diff = i[:, None] - i[None, :]
    dmat = jnp.where(diff >= 0, jnp.exp(jnp.maximum(diff, 0.0)[None] * log_g[:, None, None]), 0.0)
    dq = jnp.exp((i + 1.0)[None, :] * log_g[:, None])
    dk = jnp.exp((c - 1.0 - i)[None, :] * log_g[:, None])
    dc = jnp.exp(c * log_g)
    return dmat, dq, dk, dc


def _ret_chunk(q, k, v, g, s, dmat, dq, dk, dc):
    att = _nt(q.astype(BF16), k.astype(BF16)) * dmat
    o = _mm(att.astype(BF16), v.astype(BF16)) + _mm((q * dq).astype(BF16), s.astype(BF16))
    s = s * dc + _tn((k * dk).astype(BF16), v.astype(BF16))
    o = o * lax.rsqrt(jnp.mean(o * o, axis=-1, keepdims=True) + EPS)
    return o * (g * jax.nn.sigmoid(g)), s


def _ret_prompt_kernel(q_ref, k_ref, v_ref, g_ref, dmat_ref, dq_ref, dk_ref, dc_ref, o_ref, st_ref):
    c = RET_CHUNK
    per = LANES // DK_R
    lane = lax.broadcasted_iota(jnp.int32, (1, LANES), 1)

    def chunk(i, states):
        rows = pl.ds(pl.multiple_of(i * c, c), c)
        q_all, k_all = q_ref[rows, :], k_ref[rows, :]
        out = []
        for hh in range(per):
            hm = (lane // DK_R) == hh
            vl = slice(hh * DV_R, (hh + 1) * DV_R)
            o, s = _ret_chunk(jnp.where(hm, q_all, 0.0), jnp.where(hm, k_all, 0.0), v_ref[rows, vl], g_ref[rows, vl],
                              states[hh], dmat_ref[hh], dq_ref[hh], dk_ref[hh], dc_ref[hh])
            o_ref[rows, vl] = o
            out.append(s)
        return tuple(out)

    states = lax.fori_loop(0, q_ref.shape[0] // c, chunk, tuple(jnp.zeros((LANES, DV_R), F32) for _ in range(per)),
                           unroll=2)
    for hh in range(per):
        st_ref[hh] = states[hh][hh * DK_R:(hh + 1) * DK_R, :]


def _ret_tables_bcast(c, rows):
    dmat, dq, dk, dc = _ret_decay_tables(c)
    pad = rows - c
    dmat = jnp.pad(dmat, ((0, 0), (0, pad), (0, pad)))
    dq = jnp.broadcast_to(jnp.pad(dq, ((0, 0), (0, pad)))[:, :, None], (H_R, rows, LANES))
    dk = jnp.broadcast_to(jnp.pad(dk, ((0, 0), (0, pad)))[:, :, None], (H_R, rows, LANES))
    dc = jnp.broadcast_to(dc[:, None, None], (H_R, 1, LANES))
    return dmat, dq, dk, dc


def _ret_prompt(q_r, k_r, v_r, g_r, n_seq, seq):
    c = RET_CHUNK
    dmat, dq, dk, dc = _ret_tables_bcast(c, c)
    per = LANES // DK_R
    qk = pl.BlockSpec((seq, LANES), lambda s, h: (s, h))
    vg = pl.BlockSpec((seq, per * DV_R), lambda s, h: (s, h))
    tab = lambda r: pl.BlockSpec((per, r, LANES), lambda s, h: (h, 0, 0))
    return pl.pallas_call(
        _ret_prompt_kernel,
        grid=(n_seq, H_R // per),
        in_specs=[qk, qk, vg, vg, tab(c), tab(c), tab(c), tab(1)],
        out_specs=[vg, pl.BlockSpec((None, per, DK_R, DV_R), lambda s, h: (s, h, 0, 0))],
        out_shape=[jax.ShapeDtypeStruct((n_seq * seq, R_V), F32),
                   jax.ShapeDtypeStruct((n_seq, H_R, DK_R, DV_R), F32)],
        compiler_params=_params(("arbitrary", "arbitrary")),
        name="ret_prompt",
    )(q_r, k_r, v_r, g_r, dmat, dq, dk, dc)


def _pad_rows(x, rows):
    return jnp.concatenate([x, jnp.zeros((rows - x.shape[0], x.shape[1]), x.dtype)], axis=0)


def _moba_sample_kernel(pt_ref, q_ref, kn_ref, vn_ref, *rest, n_pages):
    k_refs, v_refs = rest[:n_pages], rest[n_pages:2 * n_pages]
    o_ref, s_ref = rest[2 * n_pages], rest[2 * n_pages + 1]
    q = q_ref[...]
    t = q.shape[0]
    page = k_refs[0].shape[-1]
    per_blk = MOBA_BLOCK // page
    n_blk = n_pages // per_blk
    lane = lax.broadcasted_iota(jnp.int32, (1, A_W), 1)
    qbd = jnp.concatenate([jnp.where((lane // HD_A) == h, q, 0.0) for h in range(H_A)], axis=0)
    qs = (qbd * (HD_A ** -0.5)).astype(BF16)
    rows = qbd.shape[0]

    bsum = [jnp.zeros((rows, 1), F32) for _ in range(n_blk)]
    for p in range(n_pages):
        sp = _mm(qs, k_refs[p][...].reshape(A_W, page).astype(BF16))
        s_ref[:, p * page:(p + 1) * page] = sp
        bsum[p // per_blk] = bsum[p // per_blk] + jnp.sum(sp, axis=1, keepdims=True)
    sel = []
    for j in range(n_blk):
        rank = jnp.zeros((rows, 1), F32)
        for jp in range(n_blk):
            if jp != j:
                beats = (bsum[jp] > bsum[j]) | (bsum[jp] == bsum[j]) if jp < j else bsum[jp] > bsum[j]
                rank = rank + jnp.where(beats, 1.0, 0.0)
        sel.append(rank < MOBA_TOPK)

    qi = lax.broadcasted_iota(jnp.int32, (rows, LANES), 0) % t
    causal = lax.broadcasted_iota(jnp.int32, (rows, LANES), 1) <= qi
    s0 = jnp.where(causal, _nt(qs, _pad_rows(kn_ref[...], LANES).astype(BF16)), NEG_INF)
    m = jnp.max(s0, axis=1, keepdims=True)
    p0 = jnp.exp(s0 - m)
    carry = (m, jnp.sum(p0, axis=1, keepdims=True), _mm(p0.astype(BF16), _pad_rows(vn_ref[...], LANES).astype(BF16)))
    for p in range(n_pages):
        m, l, acc = carry
        s = jnp.where(sel[p // per_blk], s_ref[:, p * page:(p + 1) * page], NEG_INF)
        m_new = jnp.maximum(m, jnp.max(s, axis=1, keepdims=True))
        alpha = jnp.exp(m - m_new)
        pr = jnp.exp(s - m_new)
        carry = (m_new, alpha * l + jnp.sum(pr, axis=1, keepdims=True),
                 alpha * acc + _nt(pr.astype(BF16), v_refs[p][...].reshape(A_W, page).astype(BF16)))
    _, l, acc = carry
    o = acc / l
    out = jnp.zeros((t, A_W), F32)
    for h in range(H_A):
        out = out + jnp.where((lane // HD_A) == h, o[h * t:(h + 1) * t, :], 0.0)
    o_ref[...] = out


def _moba_sample(q_a, k_a, v_a, cache_kt, cache_vt, page_table, dec_seq):
    n_s, n_pages = page_table.shape
    new = pl.BlockSpec((dec_seq, A_W), lambda n, pt: (n, 0))
    pages = [pl.BlockSpec((None,) + cache_kt.shape[1:], lambda n, pt, p=p: (pt[n * n_pages + p], 0, 0, 0))
             for p in range(n_pages)]
    grid_spec = pltpu.PrefetchScalarGridSpec(
        num_scalar_prefetch=1,
        grid=(n_s,),
        in_specs=[new, new, new] + pages + pages,
        out_specs=pl.BlockSpec((dec_seq, A_W), lambda n, pt: (n, 0)),
        scratch_shapes=[pltpu.VMEM((H_A * dec_seq, n_pages * cache_kt.shape[-1]), F32)],
    )
    return pl.pallas_call(
        functools.partial(_moba_sample_kernel, n_pages=n_pages),
        grid_spec=grid_spec,
        out_shape=jax.ShapeDtypeStruct((n_s * dec_seq, A_W), F32),
        compiler_params=_params(("arbitrary",)),
        name="moba_sample",
    )(page_table.reshape(-1), q_a, k_a, v_a, *([cache_kt] * n_pages), *([cache_vt] * n_pages))


RET_SAMPLE_SEQS = 4


def _ret_sample_kernel(q_ref, k_ref, v_ref, g_ref, s0_ref, dmat_ref, dq_ref, dk_ref, dc_ref, o_ref, st_ref):
    n_seq = s0_ref.shape[0]
    t = q_ref.shape[0] // n_seq
    lane = lax.broadcasted_iota(jnp.int32, (1, LANES), 1)
    per = LANES // DK_R
    zero_half = jnp.zeros((DK_R, DV_R), F32)
    for i in range(n_seq):
        rows = slice(i * t, (i + 1) * t)
        for h in range(H_R):
            hh = h % per
            hm = (lane // DK_R) == hh
            qk_l = slice((h // per) * LANES, (h // per + 1) * LANES)
            v_l = slice(h * DV_R, (h + 1) * DV_R)
            q = _pad_rows(jnp.where(hm, q_ref[rows, qk_l], 0.0), LANES)
            k = _pad_rows(jnp.where(hm, k_ref[rows, qk_l], 0.0), LANES)
            v = _pad_rows(v_ref[rows, v_l], LANES)
            g = _pad_rows(g_ref[rows, v_l], LANES)
            halves = [zero_half] * per
            halves[hh] = s0_ref[i, h]
            o, s = _ret_chunk(q, k, v, g, jnp.concatenate(halves, axis=0),
                              dmat_ref[h], dq_ref[h], dk_ref[h], dc_ref[h])
            o_ref[rows, v_l] = o[:t, :]
            st_ref[i, h] = s[hh * DK_R:(hh + 1) * DK_R, :]


def _ret_sample(q_r, k_r, v_r, g_r, s0, dec_seq):
    n_s = s0.shape[0]
    ns = RET_SAMPLE_SEQS
    dmat, dq, dk, dc = _ret_tables_bcast(dec_seq, LANES)
    qk = pl.BlockSpec((ns * dec_seq, R_QK), lambda n: (n, 0))
    vg = pl.BlockSpec((ns * dec_seq, R_V), lambda n: (n, 0))
    st = pl.BlockSpec((ns, H_R, DK_R, DV_R), lambda n: (n, 0, 0, 0))
    full = lambda a: pl.BlockSpec(a.shape, lambda n: (0,) * a.ndim)
    return pl.pallas_call(
        _ret_sample_kernel,
        grid=(n_s // ns,),
        in_specs=[qk, qk, vg, vg, st, full(dmat), full(dq), full(dk), full(dc)],
        out_specs=[vg, st],
        out_shape=[jax.ShapeDtypeStruct((n_s * dec_seq, R_V), F32), jax.ShapeDtypeStruct(s0.shape, F32)],
        compiler_params=_params(("arbitrary",)),
        name="ret_sample",
    )(q_r, k_r, v_r, g_r, s0, dmat, dq, dk, dc)


ROUTER_ROWS = SUBLANES + N_EXPERTS


def _route(xn, wr_ref, br_ref, eid_ref, we_ref):
    xh, xl = _split_bf16(xn)
    lt = _nt(wr_ref[...], jnp.concatenate([xh, xh, xl], axis=1)) + br_ref[...]
    tm = lt.shape[1]
    r8 = lax.broadcasted_iota(jnp.int32, (SUBLANES, tm), 0)
    lg = jnp.where(r8 < MOE_GROUPS, lt[:SUBLANES, :], NEG_INF)
    mg = jnp.max(lg, axis=0, keepdims=True)
    wg = 1.0 / jnp.sum(jnp.exp(lg - mg), axis=0, keepdims=True)
    gidx = jnp.min(jnp.where(lg == mg, r8, SUBLANES), axis=0, keepdims=True)
    le = jnp.zeros((MOE_PER_GROUP, tm), F32)
    for gi in range(MOE_GROUPS):
        lo = SUBLANES + gi * MOE_PER_GROUP
        le = le + jnp.where(gidx == gi, lt[lo:lo + MOE_PER_GROUP, :], 0.0)
    v1 = jnp.max(le, axis=0, keepdims=True)
    i1 = jnp.min(jnp.where(le == v1, r8, MOE_PER_GROUP), axis=0, keepdims=True)
    le2 = jnp.where(r8 == i1, -jnp.inf, le)
    v2 = jnp.max(le2, axis=0, keepdims=True)
    i2 = jnp.min(jnp.where(le2 == v2, r8, MOE_PER_GROUP), axis=0, keepdims=True)
    e21 = jnp.exp(v2 - v1)
    w1 = wg / (1.0 + e21)
    eid_ref[0:1, :] = gidx * MOE_PER_GROUP + i1
    eid_ref[1:2, :] = gidx * MOE_PER_GROUP + i2
    we_ref[0:1, :] = w1
    we_ref[1:2, :] = w1 * e21


def _proj_router_kernel(*refs, n_in, glu, n_ptiles):
    a_refs, w_refs = refs[:2 * n_in], refs[2 * n_in:3 * n_in]
    k = 3 * n_in
    b_ref = refs[k] if glu else None
    k += int(glu)
    hp_ref, hs_ref, g_ref, wr_ref, br_ref, ho_ref, xn_ref, eid_ref, we_ref = refs[k:k + 9]
    acc = None
    for j, w_ref in enumerate(w_refs):
        a = _pick_rows(n_ptiles, a_refs[2 * j], a_refs[2 * j + 1])
        part = _mm(a.astype(BF16), w_ref[...])
        acc = part if acc is None else acc + part
    if glu:
        acc = acc + b_ref[...]
        half = acc.shape[1] // 2
        acc = acc[:, :half] * jax.nn.sigmoid(acc[:, half:])
    hn = _pick_rows(n_ptiles, hp_ref, hs_ref) + acc
    ho_ref[...] = hn
    xn = _rms_rows(hn, g_ref[...])
    for j in range(TOKEN_SUBLANES):
        xn_ref[pl.ds(j, xn.shape[0], stride=TOKEN_SUBLANES), :] = xn[:, j * LANES:(j + 1) * LANES]
    _route(xn, wr_ref, br_ref, eid_ref, we_ref)


def _router_weights(w_group, b_group, w_expert, b_expert):
    d = w_group.shape[0]
    wt = jnp.concatenate([w_group.T, jnp.zeros((SUBLANES - MOE_GROUPS, d), F32), w_expert.T], axis=0)
    hi = wt.astype(BF16)
    lo = (wt - hi.astype(F32)).astype(BF16)
    bias = jnp.concatenate([b_group, jnp.zeros((SUBLANES - MOE_GROUPS,), F32), b_expert]).reshape(-1, 1)
    return jnp.concatenate([hi, lo, hi], axis=1), bias


def _proj_router(acts, weights, bias, h, g, router, glu):
    d = h[0].shape[1]
    n = h[0].shape[0] + h[1].shape[0]
    tm = PROJ_TILE
    n_ptiles = h[0].shape[0] // tm
    wr, br = router
    row = lambda w: pl.BlockSpec((tm, w), lambda i: (i, 0))
    full = lambda a: pl.BlockSpec(a.shape, lambda i: (0,) * a.ndim)
    tok = pl.BlockSpec((MOE_TOPK, tm), lambda i: (0, i))
    operands = ([a for pair in acts for a in pair] + list(weights) + ([bias.reshape(1, -1)] if glu else [])
                + [h[0], h[1], g.reshape(1, d), wr, br])
    in_specs = ([spec for pair in acts for spec in _split_rows(tm, pair[0].shape[1], n_ptiles)]
                + [full(w) for w in weights]
                + ([pl.BlockSpec((1, bias.shape[0]), lambda i: (0, 0))] if glu else [])
                + [*_split_rows(tm, d, n_ptiles), pl.BlockSpec((1, d), lambda i: (0, 0)), full(wr), full(br)])
    return pl.pallas_call(
        functools.partial(_proj_router_kernel, n_in=len(acts), glu=glu, n_ptiles=n_ptiles),
        grid=(n // tm,),
        in_specs=in_specs,
        out_specs=[row(d), pl.BlockSpec((tm * TOKEN_SUBLANES, LANES), lambda i: (i, 0)), tok, tok],
        out_shape=[jax.ShapeDtypeStruct((n, d), F32), jax.ShapeDtypeStruct((n * TOKEN_SUBLANES, LANES), F32),
                   jax.ShapeDtypeStruct((MOE_TOPK, n), jnp.int32), jax.ShapeDtypeStruct((MOE_TOPK, n), F32)],
        compiler_params=_params(("arbitrary",)),
        name="glu_router" if glu else "out_proj_router",
    )(*operands)


def _moe_plan(eid):
    n_tok = eid.shape[1]
    n_assign = MOE_TOPK * n_tok
    blk = MOE_ROWS
    n_blocks = -(-n_assign // blk) + N_EXPERTS
    e_flat = eid.reshape(-1)
    experts = jnp.arange(N_EXPERTS, dtype=jnp.int32)
    counts = jnp.sum((e_flat[:, None] == experts[None, :]).astype(jnp.int32), axis=0)
    order = jnp.argsort(e_flat).astype(jnp.int32)
    pc = (counts + blk - 1) // blk * blk
    pend = jnp.cumsum(pc)
    pstart = pend - pc
    start = jnp.cumsum(counts) - counts
    n_steps = n_blocks + 2
    first_row = jnp.arange(n_steps, dtype=jnp.int32) * blk
    blk_e = jnp.minimum(jnp.sum((pend[None, :] <= first_row[:, None]).astype(jnp.int32), axis=1), N_EXPERTS - 1)
    lane = jnp.arange(blk, dtype=jnp.int32)[None, :]
    off = first_row[:, None] + lane - pstart[blk_e][:, None]
    valid = off < counts[blk_e][:, None]
    a_row = order[jnp.clip(start[blk_e][:, None] + off, 0, n_assign - 1)]
    a_row = jnp.where(valid, a_row, 0)
    src = a_row % n_tok
    dst = jnp.concatenate([n_assign + lane, jnp.where(valid, a_row, n_assign + lane)], axis=0)
    n_used = (pend[-1] // blk).astype(jnp.int32).reshape(1)
    return blk_e, n_used, src.reshape(-1), dst.reshape(-1), n_steps


def _moe_kernel(blk_e_ref, n_used_ref, src_ref, dst_ref, x_hbm, w13_ref, w2_ref, y_hbm,
                xbuf, ybuf, w13b, w2b, gsem, ssem):
    b = pl.program_id(0)
    n_used = n_used_ref[0]
    blk = MOE_ROWS
    slot = b % 2

    ts = TOKEN_SUBLANES

    def gather_row(block, slot, r):
        tok_row = pl.multiple_of(src_ref[block * blk + r] * ts, ts)
        return pltpu.make_async_copy(x_hbm.at[pl.ds(tok_row, ts), :],
                                     xbuf.at[slot, pl.ds(r * ts, ts), :], gsem.at[slot])

    def scatter_row(block, slot, r):
        return pltpu.make_async_copy(ybuf.at[slot, pl.ds(r, 1), :],
                                     y_hbm.at[pl.ds(dst_ref[block * blk + r], 1), :], ssem.at[slot])

    def gather_all(slot):
        return pltpu.make_async_copy(x_hbm.at[pl.ds(0, blk * ts), :], xbuf.at[slot], gsem.at[slot])

    def scatter_all(slot):
        return pltpu.make_async_copy(ybuf.at[slot], y_hbm.at[pl.ds(0, blk), :], ssem.at[slot])

    def issue_row_copies():
        for r in range(blk):
            gather_row(b + 1, 1 - slot, r).start(priority=r % 2)
        for r in range(blk):
            scatter_row(b, 1 - slot, r).start(priority=r % 2)

    @pl.when(b == 0)
    def _():
        xbuf[...] = jnp.zeros(xbuf.shape, xbuf.dtype)
        ybuf[...] = jnp.zeros(ybuf.shape, ybuf.dtype)
        fill = pltpu.make_async_copy(ybuf.at[0], y_hbm.at[pl.ds(y_hbm.shape[0] - blk, blk), :], ssem.at[0])
        fill.start()
        fill.wait()
        for r in range(blk):
            gather_row(0, 0, r).start(priority=r % 2)

    @pl.when(b <= n_used)
    def _():
        gather_all(slot).wait()

    @pl.when((b >= 1) & (b <= n_used + 1))
    def _():
        scatter_all(slot).wait()

    @pl.when(b < n_used)
    def _():
        @pl.when((b == 0) | (blk_e_ref[b] != blk_e_ref[jnp.maximum(b - 1, 0)]))
        def _():
            w13b[...] = w13_ref[...].astype(BF16)
            w2b[...] = w2_ref[...].astype(BF16)

        issue_row_copies()
        x = jnp.concatenate([xbuf[slot, pl.ds(j, blk, stride=ts), :] for j in range(ts)], axis=1)
        hb = _mm(x.astype(BF16), w13b[...])
        half = hb.shape[1] // 2
        gate = hb[:, :half]
        act = gate * jax.nn.sigmoid(gate) * hb[:, half:]
        ybuf[slot] = _mm(act.astype(BF16), w2b[...])

    @pl.when(b == n_used)
    def _():
        for r in range(blk):
            scatter_row(b, 1 - slot, r).start(priority=r % 2)


def _moe_experts(xn, eid, w13, w2, layer):
    n_tok, d = xn.shape[0] // TOKEN_SUBLANES, w13.shape[2]
    blk = MOE_ROWS
    blk_e, n_used, src_tok, dst_row, n_steps = _moe_plan(eid)
    grid_spec = pltpu.PrefetchScalarGridSpec(
        num_scalar_prefetch=4,
        grid=(n_steps,),
        in_specs=[pl.BlockSpec(memory_space=pl.ANY),
                  pl.BlockSpec((None, None) + w13.shape[2:], lambda b, be, *_: (layer, be[b], 0, 0)),
                  pl.BlockSpec((None, None) + w2.shape[2:], lambda b, be, *_: (layer, be[b], 0, 0))],
        out_specs=pl.BlockSpec(memory_space=pl.ANY),
        scratch_shapes=[pltpu.VMEM((2, blk * TOKEN_SUBLANES, LANES), F32), pltpu.VMEM((2, blk, d), F32),
                        pltpu.VMEM(w13.shape[2:], BF16), pltpu.VMEM(w2.shape[2:], BF16),
                        pltpu.SemaphoreType.DMA((2,)), pltpu.SemaphoreType.DMA((2,))],
    )
    return pl.pallas_call(
        _moe_kernel,
        grid_spec=grid_spec,
        out_shape=jax.ShapeDtypeStruct((MOE_TOPK * n_tok + blk, d), F32),
        compiler_params=_params(("arbitrary",)),
        name="moe_experts",
    )(blk_e, n_used, src_tok, dst_row, xn, w13, w2)


def _combine_kernel(h_ref, y0_ref, y1_ref, w_ref, g_ref, o_ref, *, norm):
    w = w_ref[...]
    h = h_ref[...] + (y0_ref[...] * w[:, 0:1] + y1_ref[...] * w[:, 1:2])
    o_ref[...] = _rms_rows(h, g_ref[...]) if norm else h


def _moe_combine(h, y, we, g, norm, row_lo, n_rows):
    n, d = h.shape
    tm = ROW_TILE
    lo = row_lo // tm
    return pl.pallas_call(
        functools.partial(_combine_kernel, norm=norm),
        grid=(n_rows // tm,),
        in_specs=[pl.BlockSpec((tm, d), lambda i: (lo + i, 0)), pl.BlockSpec((tm, d), lambda i: (lo + i, 0)),
                  pl.BlockSpec((tm, d), lambda i: (n // tm + lo + i, 0)),
                  pl.BlockSpec((tm, MOE_TOPK), lambda i: (lo + i, 0)), pl.BlockSpec((1, d), lambda i: (0, 0))],
        out_specs=pl.BlockSpec((tm, d), lambda i: (i, 0)),
        out_shape=jax.ShapeDtypeStruct((n_rows, d), F32),
        compiler_params=_params(("arbitrary",)),
        name="moe_combine_norm" if norm else "moe_combine",
    )(h, y, y, we.T, g.reshape(1, d))


S5_CH = LANES
S5_ST = (LANES // S5_GROUP) * S5_STATE
S5_SCAN_LANES = 512


def _s5_kernel(h_ref, g_ref, bre_ref, bim_ref, are_ref, aim_ref, cre_ref, cim_ref, d_ref, s0r_ref, s0i_ref,
               zg_ref, fr_ref, fi_ref, xr, xi, sr, si, *, s, steps):
    c = pl.program_id(1)

    @pl.when(c == 0)
    def _():
        sr[...] = s0r_ref[...]
        si[...] = s0i_ref[...]

    u = _rms_rows(h_ref[...], g_ref[...])
    ub = u.astype(BF16)
    n_ch = u.shape[1] // S5_CH
    for j in range(n_ch):
        uj = ub[:, j * S5_CH:(j + 1) * S5_CH]
        xr[:, j * S5_ST:(j + 1) * S5_ST] = _mm(uj, bre_ref[j])
        xi[:, j * S5_ST:(j + 1) * S5_ST] = _mm(uj, bim_ref[j])

    for lo in range(0, xr.shape[1], S5_SCAN_LANES):
        ls = slice(lo, lo + S5_SCAN_LANES)
        ar = jnp.broadcast_to(are_ref[:, ls], (s, S5_SCAN_LANES))
        ai = jnp.broadcast_to(aim_ref[:, ls], (s, S5_SCAN_LANES))

        pr, pi = sr[:, ls], si[:, ls]
        for t in range(steps):
            rows = slice(t * s, (t + 1) * s)
            pr, pi = ar * pr - ai * pi + xr[rows, ls], ar * pi + ai * pr + xi[rows, ls]
            xr[rows, ls] = pr
            xi[rows, ls] = pi
        sr[:, ls] = pr
        si[:, ls] = pi

    for j in range(n_ch):
        cs = slice(j * S5_CH, (j + 1) * S5_CH)
        ss = slice(j * S5_ST, (j + 1) * S5_ST)
        y = _mm(xr[:, ss].astype(BF16), cre_ref[j]) - _mm(xi[:, ss].astype(BF16), cim_ref[j])
        zg_ref[:, cs] = jax.nn.gelu(y + d_ref[:, cs] * u[:, cs])

    @pl.when(c == pl.num_programs(1) - 1)
    def _():
        fr_ref[...] = sr[...]
        fi_ref[...] = si[...]


def _s5_weights(lam_re, lam_im, log_dt, b_re, b_im, c_re, c_im):
    dt = jnp.exp(log_dt)[:, None]
    mag = jnp.exp(lam_re * dt)
    ang = lam_im * dt
    ab_re = mag * jnp.cos(ang)
    ab_im = mag * jnp.sin(ang)
    den = lam_re * lam_re + lam_im * lam_im
    nr = ab_re - 1.0
    co_re = (nr * lam_re + ab_im * lam_im) / den
    co_im = (ab_im * lam_re - nr * lam_im) / den
    bb_re = co_re[..., None] * b_re - co_im[..., None] * b_im
    bb_im = co_re[..., None] * b_im + co_im[..., None] * b_re
    per = S5_CH // S5_GROUP
    n_ch = lam_re.shape[0] // per
    eye = jnp.eye(per, dtype=F32)

    def in_blocks(bb):
        w = bb.transpose(0, 2, 1).reshape(n_ch, per, S5_GROUP, S5_STATE)
        return jnp.einsum('jgcp,gh->jgchp', w, eye).reshape(n_ch, S5_CH, S5_ST).astype(BF16)

    def out_blocks(cc):
        w = cc.transpose(0, 2, 1).reshape(n_ch, per, S5_STATE, S5_GROUP)
        return jnp.einsum('jgpc,gh->jgphc', w, eye).reshape(n_ch, S5_ST, S5_CH).astype(BF16)

    return (in_blocks(bb_re), in_blocks(bb_im), ab_re.reshape(1, -1), ab_im.reshape(1, -1),
            out_blocks(c_re), out_blocks(c_im))


def _s5_scan(h, g, weights, d_skip, s0_re, s0_im, s, steps):
    n, d = h.shape
    n_groups = s0_re.shape[0] // s
    rows = steps * s
    chunks = n // (n_groups * rows)
    bre, bim, are, aim, cre, cim = weights
    n_state = are.shape[1]
    full = lambda a: pl.BlockSpec(a.shape, lambda gi, c: (0,) * a.ndim)
    row = pl.BlockSpec((rows, d), lambda gi, c: (gi * chunks + c, 0))
    state = pl.BlockSpec((s, n_state), lambda gi, c: (gi, 0))
    return pl.pallas_call(
        functools.partial(_s5_kernel, s=s, steps=steps),
        grid=(n_groups, chunks),
        in_specs=[row, pl.BlockSpec((1, d), lambda gi, c: (0, 0)), full(bre), full(bim), full(are), full(aim),
                  full(cre), full(cim), pl.BlockSpec((1, d), lambda gi, c: (0, 0)), state, state],
        out_specs=[row, state, state],
        out_shape=[jax.ShapeDtypeStruct((n, d), F32), jax.ShapeDtypeStruct(s0_re.shape, F32),
                   jax.ShapeDtypeStruct(s0_re.shape, F32)],
        scratch_shapes=[pltpu.VMEM((rows, n_state), F32), pltpu.VMEM((rows, n_state), F32),
                        pltpu.VMEM((s, n_state), F32), pltpu.VMEM((s, n_state), F32)],
        compiler_params=_params(("arbitrary", "arbitrary")),
        name="s5_scan",
    )(h, g.reshape(1, d), bre, bim, are, aim, cre, cim, d_skip.reshape(1, d), s0_re, s0_im)


S5_PROMPT_STEPS = 32
S5_SAMPLE_SEQS = 32


def kernel(x_prompt, x_sample, cache_k, cache_v, state_ret, state_s5_re, state_s5_im, page_table, norm1, norm2, norm_f, w_in_even, w_out_even, s5_lam_re, s5_lam_im, s5_log_dt, s5_b_re, s5_b_im, s5_c_re, s5_c_im, s5_d, s5_w_glu, s5_b_glu, moe_w_group, moe_b_group, moe_w_expert, moe_b_expert, moe_w13, moe_w2):
    n_p, seq, d = x_prompt.shape
    n_s, dec_seq, _ = x_sample.shape
    pool, page = cache_k.shape[1], cache_k.shape[2]
    past_len = page_table.shape[1] * page
    np_rows, ns_rows = n_p * seq, n_s * dec_seq
    routers = [_router_weights(moe_w_group[li], moe_b_group[li], moe_w_expert[li], moe_b_expert[li])
               for li in range(2)]

    def moe(li, xn, eid):
        return _moe_experts(xn, eid, moe_w13, moe_w2, li)

    xp, xs = x_prompt.reshape(np_rows, d), x_sample.reshape(ns_rows, d)
    w_in = w_in_even[0].astype(BF16)
    qa_p, ka_p, va_p, qr_p, kr_p, vr_p, gr_p, kt_p, vt_p = _in_proj(xp, norm1[0], w_in, jnp.arange(seq), page)
    qa_s, ka_s, va_s, qr_s, kr_s, vr_s, gr_s = _in_proj(xs, norm1[0], w_in,
                                                        past_len + jnp.arange(ROW_TILE) % dec_seq)
    oa_p = _moba_prompt(qa_p, ka_p, va_p, n_p, seq)
    or_p, ret_p = _ret_prompt(qr_p, kr_p, vr_p, gr_p, n_p, seq)
    oa_s = _moba_sample(qa_s, ka_s, va_s, cache_k[0].transpose(0, 2, 3, 1), cache_v[0].transpose(0, 2, 3, 1),
                        page_table, dec_seq)
    or_s, ret_s = _ret_sample(qr_s, kr_s, vr_s, gr_s, state_ret[0], dec_seq)
    w_out = w_out_even[0].astype(BF16)
    h, xn, eid, we = _proj_router([(oa_p, oa_s), (or_p, or_s)], [w_out[:A_W], w_out[A_W:]], None, (xp, xs),
                                  norm2[0], routers[0], False)
    y = moe(0, xn, eid)
    hp = _moe_combine(h, y, we, norm2[0], False, 0, np_rows)
    hs = _moe_combine(h, y, we, norm2[0], False, np_rows, ns_rows)

    sg = S5_SAMPLE_SEQS
    hp = hp.reshape(n_p, seq, d).transpose(1, 0, 2).reshape(np_rows, d)
    hs = hs.reshape(n_s // sg, sg, dec_seq, d).transpose(0, 2, 1, 3).reshape(ns_rows, d)
    s5w = _s5_weights(s5_lam_re[0], s5_lam_im[0], s5_log_dt[0], s5_b_re[0], s5_b_im[0], s5_c_re[0], s5_c_im[0])
    n_state = s5_lam_re.shape[1] * s5_lam_re.shape[2]
    zeros = jnp.zeros((n_p, n_state), F32)
    zg_p, s5r_p, s5i_p = _s5_scan(hp, norm1[1], s5w, s5_d[0], zeros, zeros, n_p, S5_PROMPT_STEPS)
    zg_s, s5r_s, s5i_s = _s5_scan(hs, norm1[1], s5w, s5_d[0], state_s5_re[0].reshape(n_s, n_state),
                                  state_s5_im[0].reshape(n_s, n_state), sg, dec_seq)
    h, xn, eid, we = _proj_router([(zg_p, zg_s)], [s5_w_glu[0].astype(BF16)], s5_b_glu[0], (hp, hs),
                                  norm2[1], routers[1], True)
    y = moe(1, xn, eid)
    y_prompt = _moe_combine(h, y, we, norm_f, True, 0, np_rows).reshape(seq, n_p, d).transpose(1, 0, 2)
    y_sample = _moe_combine(h, y, we, norm_f, True, np_rows, ns_rows)
    y_sample = y_sample.reshape(n_s // sg, dec_seq, sg, d).transpose(0, 2, 1, 3).reshape(n_s, dec_seq, d)

    kv_p = lambda a: a.reshape(1, n_p, seq // page, H_A, HD_A, page).transpose(0, 1, 2, 5, 3, 4)
    kv_s = lambda a: a.reshape(1, n_s, dec_seq, H_A, HD_A)
    st = lambda a, n: a.reshape((1, n) + s5_lam_re.shape[1:])
    return (y_prompt, y_sample, kv_p(kt_p), kv_p(vt_p), kv_s(ka_s), kv_s(va_s), ret_p[None], ret_s[None],
            st(s5r_p, n_p), st(s5i_p, n_p), st(s5r_s, n_s), st(s5i_s, n_s))
```

```python
import functools
import math

import jax
import jax.numpy as jnp
from jax import lax
from jax.experimental import pallas as pl
from jax.experimental.pallas import tpu as pltpu

F32 = jnp.float32
BF16 = jnp.bfloat16

H_A, HD_A = 8, 64
ROT_DIM = HD_A // 4
ROPE_THETA = 500000.0
MOBA_BLOCK = 256
MOBA_TOPK = 3
H_R, DK_R, DV_R = 8, 64, 128
RET_CHUNK = 128
S5_GROUP, S5_STATE = 16, 64
MOE_GROUPS, MOE_PER_GROUP, MOE_TOPK = 4, 8, 2
N_EXPERTS = MOE_GROUPS * MOE_PER_GROUP
A_W = H_A * HD_A
R_QK = H_R * DK_R
R_V = H_R * DV_R
NEG_INF = -1e30
EPS = 1e-6

LANES = 128
SUBLANES = 8
VMEM_LIMIT = 56 * 1024 * 1024

ROW_TILE = 256
PROJ_TILE = 512
MOE_ROWS = 256
TOKEN_SUBLANES = 8


def _nt(a, b):
    return lax.dot_general(a, b, (((1,), (1,)), ((), ())), preferred_element_type=F32)


def _tn(a, b):
    return lax.dot_general(a, b, (((0,), (0,)), ((), ())), preferred_element_type=F32)


def _mm(a, b):
    return jnp.dot(a, b, preferred_element_type=F32)


def _rms_rows(x, g):
    return x * lax.rsqrt(jnp.mean(x * x, axis=-1, keepdims=True) + EPS) * g


def _split_bf16(x):
    hi = x.astype(BF16)
    lo = (x - hi.astype(F32)).astype(BF16)
    return hi, lo


def _params(sem):
    return pltpu.CompilerParams(dimension_semantics=sem, vmem_limit_bytes=VMEM_LIMIT)


def _rotate_into(out_ref, z, c_ref, s_ref, shift, first, scale):
    for j in range(z.shape[1] // LANES):
        sl = slice(j * LANES, (j + 1) * LANES)
        zc = z[:, sl]
        up = pltpu.roll(zc, LANES - shift, axis=1)
        dn = pltpu.roll(zc, shift, axis=1)
        r = zc * c_ref[:, sl] + jnp.where(first, up, dn) * s_ref[:, sl]
        out_ref[:, sl] = r if scale is None else r * scale


def _pick_rows(n_ptiles, p_ref, s_ref):
    return jnp.where(pl.program_id(0) < n_ptiles, p_ref[...], s_ref[...])


def _split_rows(tm, width, n_ptiles):
    return (pl.BlockSpec((tm, width), lambda i: (jnp.minimum(i, n_ptiles - 1), 0)),
            pl.BlockSpec((tm, width), lambda i: (jnp.maximum(i - n_ptiles, 0), 0)))


def _store_pages_t(src_ref, dst_ref):
    n_pages, _, _, page = dst_ref.shape
    per = LANES // HD_A
    for j in range(A_W // LANES):
        t = src_ref[:, j * LANES:(j + 1) * LANES].T
        for p in range(n_pages):
            dst_ref[p, j * per:(j + 1) * per] = t[:, p * page:(p + 1) * page].reshape(per, HD_A, page)


def _in_proj_kernel(x_ref, g_ref, w_ref, ca_ref, sa_ref, cr_ref, sr_ref,
                    qa_ref, ka_ref, va_ref, qr_ref, kr_ref, vr_ref, gr_ref, *page_refs):
    xn = _rms_rows(x_ref[...], g_ref[...]).astype(BF16)
    lane = lax.broadcasted_iota(jnp.int32, (1, LANES), 1)
    first_a = (lane % HD_A) < (ROT_DIM // 2)
    first_r = (lane % 2) == 0

    def sec(lo, width):
        return _mm(xn, w_ref[:, lo:lo + width])

    _rotate_into(qa_ref, sec(0, A_W), ca_ref, sa_ref, ROT_DIM // 2, first_a, None)
    _rotate_into(ka_ref, sec(A_W, A_W), ca_ref, sa_ref, ROT_DIM // 2, first_a, None)
    va_ref[...] = sec(2 * A_W, A_W)
    _rotate_into(qr_ref, sec(3 * A_W, R_QK), cr_ref, sr_ref, 1, first_r, None)
    _rotate_into(kr_ref, sec(3 * A_W + R_QK, R_QK), cr_ref, sr_ref, 1, first_r, DK_R ** -0.5)
    vr_ref[...] = sec(3 * A_W + 2 * R_QK, R_V)
    gr_ref[...] = sec(3 * A_W + 2 * R_QK + R_V, R_V)
    if page_refs:
        _store_pages_t(ka_ref, page_refs[0])
        _store_pages_t(va_ref, page_refs[1])


def _rope_tables(pos):
    half = ROT_DIM // 2
    inv = ROPE_THETA ** (-jnp.arange(half, dtype=F32) / half)
    ang = pos.astype(F32)[:, None] * inv[None, :]
    cos, sin = jnp.cos(ang), jnp.sin(ang)
    rest = HD_A - ROT_DIM
    c = jnp.concatenate([cos, cos, jnp.ones((pos.shape[0], rest), F32)], axis=-1)
    s = jnp.concatenate([-sin, sin, jnp.zeros((pos.shape[0], rest), F32)], axis=-1)
    return jnp.tile(c, (1, H_A)), jnp.tile(s, (1, H_A))


def _retnet_tables(pos):
    n = DK_R // 2
    inv = 1.0 / (10000.0 ** jnp.linspace(0.0, 1.0, n, dtype=F32))
    ang = pos.astype(F32)[:, None] * inv[None, :]
    cos, sin = jnp.cos(ang), jnp.sin(ang)
    c = jnp.repeat(cos, 2, axis=-1)
    s = jnp.stack([-sin, sin], axis=-1).reshape(pos.shape[0], DK_R)
    return jnp.tile(c, (1, H_R)), jnp.tile(s, (1, H_R))


def _in_proj(x, g, w_bf16, pos, page=None):
    n, d = x.shape
    tm = ROW_TILE
    period_tiles = pos.shape[0] // tm
    ca, sa = _rope_tables(pos)
    cr, sr = _retnet_tables(pos)
    row = lambda w: pl.BlockSpec((tm, w), lambda i: (i, 0))
    tab = pl.BlockSpec((tm, A_W), lambda i: (i % period_tiles, 0))
    widths = (A_W, A_W, A_W, R_QK, R_QK, R_V, R_V)
    out_specs = [row(w) for w in widths]
    out_shape = [jax.ShapeDtypeStruct((n, w), F32) for w in widths]
    if page is not None:
        out_specs += [pl.BlockSpec((tm // page, H_A, HD_A, page), lambda i: (i, 0, 0, 0))] * 2
        out_shape += [jax.ShapeDtypeStruct((n // page, H_A, HD_A, page), F32)] * 2
    return pl.pallas_call(
        _in_proj_kernel,
        grid=(n // tm,),
        in_specs=[row(d), pl.BlockSpec((1, d), lambda i: (0, 0)),
                  pl.BlockSpec(w_bf16.shape, lambda i: (0, 0)), tab, tab, tab, tab],
        out_specs=out_specs,
        out_shape=out_shape,
        compiler_params=_params(("arbitrary",)),
        name="in_proj",
    )(x, g.reshape(1, d), w_bf16, ca, sa, cr, sr)


def _moba_select(q_f32, kmean, n_valid, eye):
    n_blk = kmean.shape[0]
    qh, ql = _split_bf16(q_f32)
    kh, kl = _split_bf16(kmean)
    st = _nt(jnp.concatenate([kh, kl, kh], axis=1), jnp.concatenate([qh, qh, ql], axis=1))
    jrow = lax.broadcasted_iota(jnp.int32, st.shape, 0)
    rank = jnp.zeros(st.shape, F32)
    for jp in range(n_blk):
        sj = st[jp:jp + 1, :]
        beats = (sj > st) | ((sj == st) & (jp < jrow))
        rank = rank + jnp.where(beats & (jp < n_valid), 1.0, 0.0)
    sel_t = jnp.where((jrow < n_valid) & (rank < MOBA_TOPK), 1.0, 0.0)
    sel_t = jnp.concatenate([sel_t, jnp.zeros((LANES - n_blk, st.shape[1]), F32)], axis=0).astype(BF16)
    return _nt(eye, sel_t)


def _moba_prompt_kernel(q_ref, k_ref, v_ref, o_ref, kb_ref, vb_ref, km_ref, bias_ref, s_ref):
    b = pl.program_id(2)
    blk = MOBA_BLOCK
    n_blk = k_ref.shape[0] // blk
    half = blk // 2

    lane = lax.broadcasted_iota(jnp.int32, (1, LANES), 1)
    r_i = lax.broadcasted_iota(jnp.int32, (blk, blk), 0)
    c_i = lax.broadcasted_iota(jnp.int32, (blk, blk), 1)
    heads = range(LANES // HD_A)
    hms = [(lane // HD_A) == hh for hh in heads]

    @pl.when(b == 0)
    def _():
        kb_ref[...] = k_ref[...].astype(BF16)
        vb_ref[...] = v_ref[...].astype(BF16)
        for j in range(n_blk):
            km_ref[j:j + 1, :] = jnp.mean(k_ref[j * blk:(j + 1) * blk, :], axis=0, keepdims=True)
        eye = jnp.where(r_i == c_i, 1.0, 0.0).astype(BF16)
        for hh in heads:
            bias_ref[hh, 0:blk, :] = jnp.full((blk, LANES), NEG_INF, BF16)
        for qb in range(1, n_blk):
            rows = slice(qb * blk, (qb + 1) * blk)
            for hh in heads:
                selq = _moba_select(jnp.where(hms[hh], q_ref[rows, :], 0.0), km_ref[...], qb, eye)
                bias_ref[hh, rows, :] = jnp.where(selq > 0.5, 0.0, NEG_INF).astype(BF16)

    own = pl.ds(pl.multiple_of(b * blk, blk), blk)
    q = q_ref[own, :]
    fold = lambda x: (x[:, :half], x[:, half:])
    q_aug, s_own, mx = [], [], []
    for hh in heads:
        qs = (jnp.where(hms[hh], q, 0.0) * (HD_A ** -0.5)).astype(BF16)
        so = jnp.where(c_i <= r_i, _nt(qs, kb_ref[own, :]), NEG_INF)
        q_aug.append(jnp.concatenate([qs, bias_ref[hh, own, :]], axis=1))
        s_own.append(so)
        mx.append(jnp.maximum(*fold(so)))

    def rows_of(j):
        return pl.ds(pl.multiple_of(j * blk, blk), blk)

    def scores(j, mx):
        onehot = jnp.broadcast_to(jnp.where(lane == j, 1.0, 0.0).astype(BF16), (blk, LANES))
        k_aug = jnp.concatenate([kb_ref[rows_of(j), :], onehot], axis=1)
        out = []
        for hh in heads:
            s = _nt(q_aug[hh], k_aug)
            s_ref[hh, j] = s
            out.append(jnp.maximum(mx[hh], jnp.maximum(*fold(s))))
        return tuple(out)

    mx = lax.fori_loop(0, b, scores, tuple(mx))
    ms = [jnp.max(mx[hh], axis=1, keepdims=True) for hh in heads]

    def weights(s, hh, vj):
        p = jnp.exp(s - ms[hh])
        lo, hi = fold(p)
        return lo + hi, _mm(p.astype(BF16), vj)

    def accumulate(j, carry):
        vj = vb_ref[rows_of(j), :]
        out = []
        for hh in heads:
            l_part, pv = weights(s_ref[hh, j], hh, vj)
            out.append((carry[hh][0] + l_part, carry[hh][1] + pv))
        return tuple(out)

    carry = lax.fori_loop(0, b, accumulate, tuple(weights(s_own[hh], hh, vb_ref[own, :]) for hh in heads))
    out = jnp.zeros(q.shape, F32)
    for hh in heads:
        l_part, acc = carry[hh]
        out = out + jnp.where(hms[hh], acc / jnp.sum(l_part, axis=1, keepdims=True), 0.0)
    o_ref[...] = out


def _moba_prompt(q_a, k_a, v_a, n_seq, seq):
    blk = MOBA_BLOCK
    n_blk = seq // blk
    ospec = pl.BlockSpec((blk, LANES), lambda s, h, b: (s * n_blk + b, h))
    kspec = pl.BlockSpec((seq, LANES), lambda s, h, b: (s, h))
    per = LANES // HD_A
    return pl.pallas_call(
        _moba_prompt_kernel,
        grid=(n_seq, A_W // LANES, n_blk),
        in_specs=[kspec, kspec, kspec],
        out_specs=ospec,
        out_shape=jax.ShapeDtypeStruct((n_seq * seq, A_W), F32),
        scratch_shapes=[pltpu.VMEM((seq, LANES), BF16), pltpu.VMEM((seq, LANES), BF16),
                        pltpu.VMEM((n_blk, LANES), F32), pltpu.VMEM((per, seq, LANES), BF16),
                        pltpu.VMEM((per, n_blk, blk, blk), F32)],
        compiler_params=_params(("arbitrary", "arbitrary", "arbitrary")),
        name="moba_prompt",
    )(q_a, k_a, v_a)


def _ret_decay_tables(c):
    log_g = jnp.log(1.0 - 2.0 ** (-5.0 - jnp.arange(H_R, dtype=F32)))
    i = jnp.arange(c, dtype=F32)
    diff = i[:, None] - i[None, :]
    dmat = jnp.where(diff >= 0, jnp.exp(jnp.maximum(diff, 0.0)[None] * log_g[:, None, None]), 0.0)
    dq = jnp.exp((i + 1.0)[None, :] * log_g[:, None])
    dk = jnp.exp((c - 1.0 - i)[None, :] * log_g[:, None])
    dc = jnp.exp(c * log_g)
    return dmat, dq, dk, dc


def _ret_chunk(q, k, v, g, s, dmat, dq, dk, dc):
    att = _nt(q.astype(BF16), k.astype(BF16)) * dmat
    o = _mm(att.astype(BF16), v.astype(BF16)) + _mm((q * dq).astype(BF16), s.astype(BF16))
    s = s * dc + _tn((k * dk).astype(BF16), v.astype(BF16))
    o = o * lax.rsqrt(jnp.mean(o * o, axis=-1, keepdims=True) + EPS)
    return o * (g * jax.nn.sigmoid(g)), s


def _ret_prompt_kernel(q_ref, k_ref, v_ref, g_ref, dmat_ref, dq_ref, dk_ref, dc_ref, o_ref, st_ref):
    c = RET_CHUNK
    per = LANES // DK_R
    lane = lax.broadcasted_iota(jnp.int32, (1, LANES), 1)

    def chunk(i, states):
        rows = pl.ds(pl.multiple_of(i * c, c), c)
        q_all, k_all = q_ref[rows, :], k_ref[rows, :]
        out = []
        for hh in range(per):
            hm = (lane // DK_R) == hh
            vl = slice(hh * DV_R, (hh + 1) * DV_R)
            o, s = _ret_chunk(jnp.where(hm, q_all, 0.0), jnp.where(hm, k_all, 0.0), v_ref[rows, vl], g_ref[rows, vl],
                              states[hh], dmat_ref[hh], dq_ref[hh], dk_ref[hh], dc_ref[hh])
            o_ref[rows, vl] = o
            out.append(s)
        return tuple(out)

    states = lax.fori_loop(0, q_ref.shape[0] // c, chunk, tuple(jnp.zeros((LANES, DV_R), F32) for _ in range(per)),
                           unroll=4)
    for hh in range(per):
        st_ref[hh] = states[hh][hh * DK_R:(hh + 1) * DK_R, :]


def _ret_tables_bcast(c, rows):
    dmat, dq, dk, dc = _ret_decay_tables(c)
    pad = rows - c
    dmat = jnp.pad(dmat, ((0, 0), (0, pad), (0, pad)))
    dq = jnp.broadcast_to(jnp.pad(dq, ((0, 0), (0, pad)))[:, :, None], (H_R, rows, LANES))
    dk = jnp.broadcast_to(jnp.pad(dk, ((0, 0), (0, pad)))[:, :, None], (H_R, rows, LANES))
    dc = jnp.broadcast_to(dc[:, None, None], (H_R, 1, LANES))
    return dmat, dq, dk, dc


def _ret_prompt(q_r, k_r, v_r, g_r, n_seq, seq):
    c = RET_CHUNK
    dmat, dq, dk, dc = _ret_tables_bcast(c, c)
    per = LANES // DK_R
    qk = pl.BlockSpec((seq, LANES), lambda s, h: (s, h))
    vg = pl.BlockSpec((seq, per * DV_R), lambda s, h: (s, h))
    tab = lambda r: pl.BlockSpec((per, r, LANES), lambda s, h: (h, 0, 0))
    return pl.pallas_call(
        _ret_prompt_kernel,
        grid=(n_seq, H_R // per),
        in_specs=[qk, qk, vg, vg, tab(c), tab(c), tab(c), tab(1)],
        out_specs=[vg, pl.BlockSpec((None, per, DK_R, DV_R), lambda s, h: (s, h, 0, 0))],
        out_shape=[jax.ShapeDtypeStruct((n_seq * seq, R_V), F32),
                   jax.ShapeDtypeStruct((n_seq, H_R, DK_R, DV_R), F32)],
        compiler_params=_params(("arbitrary", "arbitrary")),
        name="ret_prompt",
    )(q_r, k_r, v_r, g_r, dmat, dq, dk, dc)


def _pad_rows(x, rows):
    return jnp.concatenate([x, jnp.zeros((rows - x.shape[0], x.shape[1]), x.dtype)], axis=0)


def _moba_sample_kernel(pt_ref, q_ref, kn_ref, vn_ref, *rest, n_pages):
    k_refs, v_refs = rest[:n_pages], rest[n_pages:2 * n_pages]
    o_ref, s_ref = rest[2 * n_pages], rest[2 * n_pages + 1]
    q = q_ref[...]
    t = q.shape[0]
    page = k_refs[0].shape[-1]
    per_blk = MOBA_BLOCK // page
    n_blk = n_pages // per_blk
    lane = lax.broadcasted_iota(jnp.int32, (1, A_W), 1)
    qbd = jnp.concatenate([jnp.where((lane // HD_A) == h, q, 0.0) for h in range(H_A)], axis=0)
    qs = (qbd * (HD_A ** -0.5)).astype(BF16)
    rows = qbd.shape[0]

    bsum = [jnp.zeros((rows, 1), F32) for _ in range(n_blk)]
    for p in range(n_pages):
        sp = _mm(qs, k_refs[p][...].reshape(A_W, page).astype(BF16))
        s_ref[:, p * page:(p + 1) * page] = sp
        bsum[p // per_blk] = bsum[p // per_blk] + jnp.sum(sp, axis=1, keepdims=True)
    sel = []
    for j in range(n_blk):
        rank = jnp.zeros((rows, 1), F32)
        for jp in range(n_blk):
            if jp != j:
                beats = (bsum[jp] > bsum[j]) | (bsum[jp] == bsum[j]) if jp < j else bsum[jp] > bsum[j]
                rank = rank + jnp.where(beats, 1.0, 0.0)
        sel.append(rank < MOBA_TOPK)

    qi = lax.broadcasted_iota(jnp.int32, (rows, LANES), 0) % t
    causal = lax.broadcasted_iota(jnp.int32, (rows, LANES), 1) <= qi
    s0 = jnp.where(causal, _nt(qs, _pad_rows(kn_ref[...], LANES).astype(BF16)), NEG_INF)
    m = jnp.max(s0, axis=1, keepdims=True)
    p0 = jnp.exp(s0 - m)
    carry = (m, jnp.sum(p0, axis=1, keepdims=True), _mm(p0.astype(BF16), _pad_rows(vn_ref[...], LANES).astype(BF16)))
    for p in range(n_pages):
        m, l, acc = carry
        s = jnp.where(sel[p // per_blk], s_ref[:, p * page:(p + 1) * page], NEG_INF)
        m_new = jnp.maximum(m, jnp.max(s, axis=1, keepdims=True))
        alpha = jnp.exp(m - m_new)
        pr = jnp.exp(s - m_new)
        carry = (m_new, alpha * l + jnp.sum(pr, axis=1, keepdims=True),
                 alpha * acc + _nt(pr.astype(BF16), v_refs[p][...].reshape(A_W, page).astype(BF16)))
    _, l, acc = carry
    o = acc / l
    out = jnp.zeros((t, A_W), F32)
    for h in range(H_A):
        out = out + jnp.where((lane // HD_A) == h, o[h * t:(h + 1) * t, :], 0.0)
    o_ref[...] = out


def _moba_sample(q_a, k_a, v_a, cache_kt, cache_vt, page_table, dec_seq):
    n_s, n_pages = page_table.shape
    new = pl.BlockSpec((dec_seq, A_W), lambda n, pt: (n, 0))
    pages = [pl.BlockSpec((None,) + cache_kt.shape[1:], lambda n, pt, p=p: (pt[n * n_pages + p], 0, 0, 0))
             for p in range(n_pages)]
    grid_spec = pltpu.PrefetchScalarGridSpec(
        num_scalar_prefetch=1,
        grid=(n_s,),
        in_specs=[new, new, new] + pages + pages,
        out_specs=pl.BlockSpec((dec_seq, A_W), lambda n, pt: (n, 0)),
        scratch_shapes=[pltpu.VMEM((H_A * dec_seq, n_pages * cache_kt.shape[-1]), F32)],
    )
    return pl.pallas_call(
        functools.partial(_moba_sample_kernel, n_pages=n_pages),
        grid_spec=grid_spec,
        out_shape=jax.ShapeDtypeStruct((n_s * dec_seq, A_W), F32),
        compiler_params=_params(("arbitrary",)),
        name="moba_sample",
    )(page_table.reshape(-1), q_a, k_a, v_a, *([cache_kt] * n_pages), *([cache_vt] * n_pages))


RET_SAMPLE_SEQS = 4


def _ret_sample_kernel(q_ref, k_ref, v_ref, g_ref, s0_ref, dmat_ref, dq_ref, dk_ref, dc_ref, o_ref, st_ref):
    n_seq = s0_ref.shape[0]
    t = q_ref.shape[0] // n_seq
    lane = lax.broadcasted_iota(jnp.int32, (1, LANES), 1)
    per = LANES // DK_R
    zero_half = jnp.zeros((DK_R, DV_R), F32)
    for i in range(n_seq):
        rows = slice(i * t, (i + 1) * t)
        for h in range(H_R):
            hh = h % per
            hm = (lane // DK_R) == hh
            qk_l = slice((h // per) * LANES, (h // per + 1) * LANES)
            v_l = slice(h * DV_R, (h + 1) * DV_R)
            q = _pad_rows(jnp.where(hm, q_ref[rows, qk_l], 0.0), LANES)
            k = _pad_rows(jnp.where(hm, k_ref[rows, qk_l], 0.0), LANES)
            v = _pad_rows(v_ref[rows, v_l], LANES)
            g = _pad_rows(g_ref[rows, v_l], LANES)
            halves = [zero_half] * per
            halves[hh] = s0_ref[i, h]
            o, s = _ret_chunk(q, k, v, g, jnp.concatenate(halves, axis=0),
                              dmat_ref[h], dq_ref[h], dk_ref[h], dc_ref[h])
            o_ref[rows, v_l] = o[:t, :]
            st_ref[i, h] = s[hh * DK_R:(hh + 1) * DK_R, :]


def _ret_sample(q_r, k_r, v_r, g_r, s0, dec_seq):
    n_s = s0.shape[0]
    ns = RET_SAMPLE_SEQS
    dmat, dq, dk, dc = _ret_tables_bcast(dec_seq, LANES)
    qk = pl.BlockSpec((ns * dec_seq, R_QK), lambda n: (n, 0))
    vg = pl.BlockSpec((ns * dec_seq, R_V), lambda n: (n, 0))
    st = pl.BlockSpec((ns, H_R, DK_R, DV_R), lambda n: (n, 0, 0, 0))
    full = lambda a: pl.BlockSpec(a.shape, lambda n: (0,) * a.ndim)
    return pl.pallas_call(
        _ret_sample_kernel,
        grid=(n_s // ns,),
        in_specs=[qk, qk, vg, vg, st, full(dmat), full(dq), full(dk), full(dc)],
        out_specs=[vg, st],
        out_shape=[jax.ShapeDtypeStruct((n_s * dec_seq, R_V), F32), jax.ShapeDtypeStruct(s0.shape, F32)],
        compiler_params=_params(("arbitrary",)),
        name="ret_sample",
    )(q_r, k_r, v_r, g_r, s0, dmat, dq, dk, dc)


ROUTER_ROWS = SUBLANES + N_EXPERTS


def _route(xn, wr_ref, br_ref, eid_ref, we_ref):
    xh, xl = _split_bf16(xn)
    lt = _nt(wr_ref[...], jnp.concatenate([xh, xh, xl], axis=1)) + br_ref[...]
    tm = lt.shape[1]
    r8 = lax.broadcasted_iota(jnp.int32, (SUBLANES, tm), 0)
    lg = jnp.where(r8 < MOE_GROUPS, lt[:SUBLANES, :], NEG_INF)
    mg = jnp.max(lg, axis=0, keepdims=True)
    wg = 1.0 / jnp.sum(jnp.exp(lg - mg), axis=0, keepdims=True)
    gidx = jnp.min(jnp.where(lg == mg, r8, SUBLANES), axis=0, keepdims=True)
    le = jnp.zeros((MOE_PER_GROUP, tm), F32)
    for gi in range(MOE_GROUPS):
        lo = SUBLANES + gi * MOE_PER_GROUP
        le = le + jnp.where(gidx == gi, lt[lo:lo + MOE_PER_GROUP, :], 0.0)
    v1 = jnp.max(le, axis=0, keepdims=True)
    i1 = jnp.min(jnp.where(le == v1, r8, MOE_PER_GROUP), axis=0, keepdims=True)
    le2 = jnp.where(r8 == i1, -jnp.inf, le)
    v2 = jnp.max(le2, axis=0, keepdims=True)
    i2 = jnp.min(jnp.where(le2 == v2, r8, MOE_PER_GROUP), axis=0, keepdims=True)
    e21 = jnp.exp(v2 - v1)
    w1 = wg / (1.0 + e21)
    eid_ref[0:1, :] = gidx * MOE_PER_GROUP + i1
    eid_ref[1:2, :] = gidx * MOE_PER_GROUP + i2
    we_ref[0:1, :] = w1
    we_ref[1:2, :] = w1 * e21


def _proj_router_kernel(*refs, n_in, glu, n_ptiles):
    a_refs, w_refs = refs[:2 * n_in], refs[2 * n_in:3 * n_in]
    k = 3 * n_in
    b_ref = refs[k] if glu else None
    k += int(glu)
    hp_ref, hs_ref, g_ref, wr_ref, br_ref, ho_ref, xn_ref, eid_ref, we_ref = refs[k:k + 9]
    acc = None
    for j, w_ref in enumerate(w_refs):
        a = _pick_rows(n_ptiles, a_refs[2 * j], a_refs[2 * j + 1])
        part = _mm(a.astype(BF16), w_ref[...])
        acc = part if acc is None else acc + part
    if glu:
        acc = acc + b_ref[...]
        half = acc.shape[1] // 2
        acc = acc[:, :half] * jax.nn.sigmoid(acc[:, half:])
    hn = _pick_rows(n_ptiles, hp_ref, hs_ref) + acc
    ho_ref[...] = hn
    xn = _rms_rows(hn, g_ref[...])
    for j in range(TOKEN_SUBLANES):
        xn_ref[pl.ds(j, xn.shape[0], stride=TOKEN_SUBLANES), :] = xn[:, j * LANES:(j + 1) * LANES]
    _route(xn, wr_ref, br_ref, eid_ref, we_ref)


def _router_weights(w_group, b_group, w_expert, b_expert):
    d = w_group.shape[0]
    wt = jnp.concatenate([w_group.T, jnp.zeros((SUBLANES - MOE_GROUPS, d), F32), w_expert.T], axis=0)
    hi = wt.astype(BF16)
    lo = (wt - hi.astype(F32)).astype(BF16)
    bias = jnp.concatenate([b_group, jnp.zeros((SUBLANES - MOE_GROUPS,), F32), b_expert]).reshape(-1, 1)
    return jnp.concatenate([hi, lo, hi], axis=1), bias


def _proj_router(acts, weights, bias, h, g, router, glu):
    d = h[0].shape[1]
    n = h[0].shape[0] + h[1].shape[0]
    tm = PROJ_TILE
    n_ptiles = h[0].shape[0] // tm
    wr, br = router
    row = lambda w: pl.BlockSpec((tm, w), lambda i: (i, 0))
    full = lambda a: pl.BlockSpec(a.shape, lambda i: (0,) * a.ndim)
    tok = pl.BlockSpec((MOE_TOPK, tm), lambda i: (0, i))
    operands = ([a for pair in acts for a in pair] + list(weights) + ([bias.reshape(1, -1)] if glu else [])
                + [h[0], h[1], g.reshape(1, d), wr, br])
    in_specs = ([spec for pair in acts for spec in _split_rows(tm, pair[0].shape[1], n_ptiles)]
                + [full(w) for w in weights]
                + ([pl.BlockSpec((1, bias.shape[0]), lambda i: (0, 0))] if glu else [])
                + [*_split_rows(tm, d, n_ptiles), pl.BlockSpec((1, d), lambda i: (0, 0)), full(wr), full(br)])
    return pl.pallas_call(
        functools.partial(_proj_router_kernel, n_in=len(acts), glu=glu, n_ptiles=n_ptiles),
        grid=(n // tm,),
        in_specs=in_specs,
        out_specs=[row(d), pl.BlockSpec((tm * TOKEN_SUBLANES, LANES), lambda i: (i, 0)), tok, tok],
        out_shape=[jax.ShapeDtypeStruct((n, d), F32), jax.ShapeDtypeStruct((n * TOKEN_SUBLANES, LANES), F32),
                   jax.ShapeDtypeStruct((MOE_TOPK, n), jnp.int32), jax.ShapeDtypeStruct((MOE_TOPK, n), F32)],
        compiler_params=_params(("arbitrary",)),
        name="glu_router" if glu else "out_proj_router",
    )(*operands)


def _moe_plan(eid):
    n_tok = eid.shape[1]
    n_assign = MOE_TOPK * n_tok
    blk = MOE_ROWS
    n_blocks = -(-n_assign // blk) + N_EXPERTS
    e_flat = eid.reshape(-1)
    experts = jnp.arange(N_EXPERTS, dtype=jnp.int32)
    counts = jnp.sum((e_flat[:, None] == experts[None, :]).astype(jnp.int32), axis=0)
    order = jnp.argsort(e_flat).astype(jnp.int32)
    pc = (counts + blk - 1) // blk * blk
    pend = jnp.cumsum(pc)
    pstart = pend - pc
    start = jnp.cumsum(counts) - counts
    n_steps = n_blocks + 2
    first_row = jnp.arange(n_steps, dtype=jnp.int32) * blk
    blk_e = jnp.minimum(jnp.sum((pend[None, :] <= first_row[:, None]).astype(jnp.int32), axis=1), N_EXPERTS - 1)
    lane = jnp.arange(blk, dtype=jnp.int32)[None, :]
    off = first_row[:, None] + lane - pstart[blk_e][:, None]
    valid = off < counts[blk_e][:, None]
    a_row = order[jnp.clip(start[blk_e][:, None] + off, 0, n_assign - 1)]
    a_row = jnp.where(valid, a_row, 0)
    src = a_row % n_tok
    dst = jnp.concatenate([n_assign + lane, jnp.where(valid, a_row, n_assign + lane)], axis=0)
    n_used = (pend[-1] // blk).astype(jnp.int32).reshape(1)
    return blk_e, n_used, src.reshape(-1), dst.reshape(-1), n_steps


def _moe_kernel(blk_e_ref, n_used_ref, src_ref, dst_ref, x_hbm, w13_ref, w2_ref, y_hbm,
                xbuf, ybuf, w13b, w2b, gsem, ssem):
    b = pl.program_id(0)
    n_used = n_used_ref[0]
    blk = MOE_ROWS
    slot = b % 2

    ts = TOKEN_SUBLANES

    def gather_row(block, slot, r):
        tok_row = pl.multiple_of(src_ref[block * blk + r] * ts, ts)
        return pltpu.make_async_copy(x_hbm.at[pl.ds(tok_row, ts), :],
                                     xbuf.at[slot, pl.ds(r * ts, ts), :], gsem.at[slot])

    def scatter_row(block, slot, r):
        return pltpu.make_async_copy(ybuf.at[slot, pl.ds(r, 1), :],
                                     y_hbm.at[pl.ds(dst_ref[block * blk + r], 1), :], ssem.at[slot])

    def gather_all(slot):
        return pltpu.make_async_copy(x_hbm.at[pl.ds(0, blk * ts), :], xbuf.at[slot], gsem.at[slot])

    def scatter_all(slot):
        return pltpu.make_async_copy(ybuf.at[slot], y_hbm.at[pl.ds(0, blk), :], ssem.at[slot])

    def issue_row_copies():
        for r in range(blk):
            gather_row(b + 1, 1 - slot, r).start(priority=r % 2)
        for r in range(blk):
            scatter_row(b, 1 - slot, r).start(priority=r % 2)

    @pl.when(b == 0)
    def _():
        xbuf[...] = jnp.zeros(xbuf.shape, xbuf.dtype)
        ybuf[...] = jnp.zeros(ybuf.shape, ybuf.dtype)
        fill = pltpu.make_async_copy(ybuf.at[0], y_hbm.at[pl.ds(y_hbm.shape[0] - blk, blk), :], ssem.at[0])
        fill.start()
        fill.wait()
        for r in range(blk):
            gather_row(0, 0, r).start(priority=r % 2)

    @pl.when(b <= n_used)
    def _():
        gather_all(slot).wait()

    @pl.when((b >= 1) & (b <= n_used + 1))
    def _():
        scatter_all(slot).wait()

    @pl.when(b < n_used)
    def _():
        @pl.when((b == 0) | (blk_e_ref[b] != blk_e_ref[jnp.maximum(b - 1, 0)]))
        def _():
            w13b[...] = w13_ref[...].astype(BF16)
            w2b[...] = w2_ref[...].astype(BF16)

        issue_row_copies()
        x = jnp.concatenate([xbuf[slot, pl.ds(j, blk, stride=ts), :] for j in range(ts)], axis=1)
        hb = _mm(x.astype(BF16), w13b[...])
        half = hb.shape[1] // 2
        gate = hb[:, :half]
        act = gate * jax.nn.sigmoid(gate) * hb[:, half:]
        ybuf[slot] = _mm(act.astype(BF16), w2b[...])

    @pl.when(b == n_used)
    def _():
        for r in range(blk):
            scatter_row(b, 1 - slot, r).start(priority=r % 2)


def _moe_experts(xn, eid, w13, w2, layer):
    n_tok, d = xn.shape[0] // TOKEN_SUBLANES, w13.shape[2]
    blk = MOE_ROWS
    blk_e, n_used, src_tok, dst_row, n_steps = _moe_plan(eid)
    grid_spec = pltpu.PrefetchScalarGridSpec(
        num_scalar_prefetch=4,
        grid=(n_steps,),
        in_specs=[pl.BlockSpec(memory_space=pl.ANY),
                  pl.BlockSpec((None, None) + w13.shape[2:], lambda b, be, *_: (layer, be[b], 0, 0)),
                  pl.BlockSpec((None, None) + w2.shape[2:], lambda b, be, *_: (layer, be[b], 0, 0))],
        out_specs=pl.BlockSpec(memory_space=pl.ANY),
        scratch_shapes=[pltpu.VMEM((2, blk * TOKEN_SUBLANES, LANES), F32), pltpu.VMEM((2, blk, d), F32),
                        pltpu.VMEM(w13.shape[2:], BF16), pltpu.VMEM(w2.shape[2:], BF16),
                        pltpu.SemaphoreType.DMA((2,)), pltpu.SemaphoreType.DMA((2,))],
    )
    return pl.pallas_call(
        _moe_kernel,
        grid_spec=grid_spec,
        out_shape=jax.ShapeDtypeStruct((MOE_TOPK * n_tok + blk, d), F32),
        compiler_params=_params(("arbitrary",)),
        name="moe_experts",
    )(blk_e, n_used, src_tok, dst_row, xn, w13, w2)


def _combine_kernel(h_ref, y0_ref, y1_ref, w_ref, g_ref, o_ref, *, norm):
    w = w_ref[...]
    h = h_ref[...] + (y0_ref[...] * w[:, 0:1] + y1_ref[...] * w[:, 1:2])
    o_ref[...] = _rms_rows(h, g_ref[...]) if norm else h


def _moe_combine(h, y, we, g, norm, row_lo, n_rows):
    n, d = h.shape
    tm = ROW_TILE
    lo = row_lo // tm
    return pl.pallas_call(
        functools.partial(_combine_kernel, norm=norm),
        grid=(n_rows // tm,),
        in_specs=[pl.BlockSpec((tm, d), lambda i: (lo + i, 0)), pl.BlockSpec((tm, d), lambda i: (lo + i, 0)),
                  pl.BlockSpec((tm, d), lambda i: (n // tm + lo + i, 0)),
                  pl.BlockSpec((tm, MOE_TOPK), lambda i: (lo + i, 0)), pl.BlockSpec((1, d), lambda i: (0, 0))],
        out_specs=pl.BlockSpec((tm, d), lambda i: (i, 0)),
        out_shape=jax.ShapeDtypeStruct((n_rows, d), F32),
        compiler_params=_params(("arbitrary",)),
        name="moe_combine_norm" if norm else "moe_combine",
    )(h, y, y, we.T, g.reshape(1, d))


S5_CH = LANES
S5_ST = (LANES // S5_GROUP) * S5_STATE
S5_SCAN_LANES = 512


def _s5_kernel(h_ref, g_ref, bre_ref, bim_ref, are_ref, aim_ref, cre_ref, cim_ref, d_ref, s0r_ref, s0i_ref,
               zg_ref, fr_ref, fi_ref, xr, xi, sr, si, u_tm, z_tm, *, s, steps):
    c = pl.program_id(1)

    @pl.when(c == 0)
    def _():
        sr[...] = s0r_ref[...]
        si[...] = s0i_ref[...]

    d = h_ref.shape[-1]
    n_ch = d // S5_CH
    u_nat = _rms_rows(h_ref[...].reshape(s * steps, d), g_ref[...])
    for n in range(s):
        for j in range(n_ch):
            u_tm[j, pl.ds(n, steps, stride=s), :] = u_nat[n * steps:(n + 1) * steps, j * S5_CH:(j + 1) * S5_CH]
    for j in range(n_ch):
        uj = u_tm[j].astype(BF16)
        xr[:, j * S5_ST:(j + 1) * S5_ST] = _mm(uj, bre_ref[j])
        xi[:, j * S5_ST:(j + 1) * S5_ST] = _mm(uj, bim_ref[j])

    for lo in range(0, xr.shape[1], S5_SCAN_LANES):
        ls = slice(lo, lo + S5_SCAN_LANES)
        ar = jnp.broadcast_to(are_ref[:, ls], (s, S5_SCAN_LANES))
        ai = jnp.broadcast_to(aim_ref[:, ls], (s, S5_SCAN_LANES))

        pr, pi = sr[:, ls], si[:, ls]
        for t in range(steps):
            rows = slice(t * s, (t + 1) * s)
            pr, pi = ar * pr - ai * pi + xr[rows, ls], ar * pi + ai * pr + xi[rows, ls]
            xr[rows, ls] = pr
            xi[rows, ls] = pi
        sr[:, ls] = pr
        si[:, ls] = pi

    for j in range(n_ch):
        cs = slice(j * S5_CH, (j + 1) * S5_CH)
        ss = slice(j * S5_ST, (j + 1) * S5_ST)
        y = _mm(xr[:, ss].astype(BF16), cre_ref[j]) - _mm(xi[:, ss].astype(BF16), cim_ref[j])
        z_tm[j] = jax.nn.gelu(y + d_ref[:, cs] * u_tm[j])
    for n in range(s):
        for j in range(n_ch):
            zg_ref[n, :, j * S5_CH:(j + 1) * S5_CH] = z_tm[j, pl.ds(n, steps, stride=s), :]

    @pl.when(c == pl.num_programs(1) - 1)
    def _():
        fr_ref[...] = sr[...]
        fi_ref[...] = si[...]


def _s5_weights(lam_re, lam_im, log_dt, b_re, b_im, c_re, c_im):
    dt = jnp.exp(log_dt)[:, None]
    mag = jnp.exp(lam_re * dt)
    ang = lam_im * dt
    ab_re = mag * jnp.cos(ang)
    ab_im = mag * jnp.sin(ang)
    den = lam_re * lam_re + lam_im * lam_im
    nr = ab_re - 1.0
    co_re = (nr * lam_re + ab_im * lam_im) / den
    co_im = (ab_im * lam_re - nr * lam_im) / den
    bb_re = co_re[..., None] * b_re - co_im[..., None] * b_im
    bb_im = co_re[..., None] * b_im + co_im[..., None] * b_re
    per = S5_CH // S5_GROUP
    n_ch = lam_re.shape[0] // per
    eye = jnp.eye(per, dtype=F32)

    def in_blocks(bb):
        w = bb.transpose(0, 2, 1).reshape(n_ch, per, S5_GROUP, S5_STATE)
        return jnp.einsum('jgcp,gh->jgchp', w, eye).reshape(n_ch, S5_CH, S5_ST).astype(BF16)

    def out_blocks(cc):
        w = cc.transpose(0, 2, 1).reshape(n_ch, per, S5_STATE, S5_GROUP)
        return jnp.einsum('jgpc,gh->jgphc', w, eye).reshape(n_ch, S5_ST, S5_CH).astype(BF16)

    return (in_blocks(bb_re), in_blocks(bb_im), ab_re.reshape(1, -1), ab_im.reshape(1, -1),
            out_blocks(c_re), out_blocks(c_im))


def _s5_scan(h, g, weights, d_skip, s0_re, s0_im, s, steps):
    n_seq, t_len, d = h.shape
    n_groups = n_seq // s
    rows = steps * s
    chunks = t_len // steps
    bre, bim, are, aim, cre, cim = weights
    n_state = are.shape[1]
    full = lambda a: pl.BlockSpec(a.shape, lambda gi, c: (0,) * a.ndim)
    row = pl.BlockSpec((s, steps, d), lambda gi, c: (gi, c, 0))
    state = pl.BlockSpec((s, n_state), lambda gi, c: (gi, 0))
    return pl.pallas_call(
        functools.partial(_s5_kernel, s=s, steps=steps),
        grid=(n_groups, chunks),
        in_specs=[row, pl.BlockSpec((1, d), lambda gi, c: (0, 0)), full(bre), full(bim), full(are), full(aim),
                  full(cre), full(cim), pl.BlockSpec((1, d), lambda gi, c: (0, 0)), state, state],
        out_specs=[row, state, state],
        out_shape=[jax.ShapeDtypeStruct(h.shape, F32), jax.ShapeDtypeStruct(s0_re.shape, F32),
                   jax.ShapeDtypeStruct(s0_re.shape, F32)],
        scratch_shapes=[pltpu.VMEM((rows, n_state), F32), pltpu.VMEM((rows, n_state), F32),
                        pltpu.VMEM((s, n_state), F32), pltpu.VMEM((s, n_state), F32),
                        pltpu.VMEM((d // S5_CH, rows, S5_CH), F32), pltpu.VMEM((d // S5_CH, rows, S5_CH), F32)],
        compiler_params=_params(("arbitrary", "arbitrary")),
        name="s5_scan",
    )(h, g.reshape(1, d), bre, bim, are, aim, cre, cim, d_skip.reshape(1, d), s0_re, s0_im)


S5_PROMPT_STEPS = 32
S5_SAMPLE_SEQS = 32


def kernel(x_prompt, x_sample, cache_k, cache_v, state_ret, state_s5_re, state_s5_im, page_table, norm1, norm2, norm_f, w_in_even, w_out_even, s5_lam_re, s5_lam_im, s5_log_dt, s5_b_re, s5_b_im, s5_c_re, s5_c_im, s5_d, s5_w_glu, s5_b_glu, moe_w_group, moe_b_group, moe_w_expert, moe_b_expert, moe_w13, moe_w2):
    n_p, seq, d = x_prompt.shape
    n_s, dec_seq, _ = x_sample.shape
    pool, page = cache_k.shape[1], cache_k.shape[2]
    past_len = page_table.shape[1] * page
    np_rows, ns_rows = n_p * seq, n_s * dec_seq
    routers = [_router_weights(moe_w_group[li], moe_b_group[li], moe_w_expert[li], moe_b_expert[li])
               for li in range(2)]

    def moe(li, xn, eid):
        return _moe_experts(xn, eid, moe_w13, moe_w2, li)

    xp, xs = x_prompt.reshape(np_rows, d), x_sample.reshape(ns_rows, d)
    w_in = w_in_even[0].astype(BF16)
    qa_p, ka_p, va_p, qr_p, kr_p, vr_p, gr_p, kt_p, vt_p = _in_proj(xp, norm1[0], w_in, jnp.arange(seq), page)
    qa_s, ka_s, va_s, qr_s, kr_s, vr_s, gr_s = _in_proj(xs, norm1[0], w_in,
                                                        past_len + jnp.arange(ROW_TILE) % dec_seq)
    oa_p = _moba_prompt(qa_p, ka_p, va_p, n_p, seq)
    or_p, ret_p = _ret_prompt(qr_p, kr_p, vr_p, gr_p, n_p, seq)
    oa_s = _moba_sample(qa_s, ka_s, va_s, cache_k[0].transpose(0, 2, 3, 1), cache_v[0].transpose(0, 2, 3, 1),
                        page_table, dec_seq)
    or_s, ret_s = _ret_sample(qr_s, kr_s, vr_s, gr_s, state_ret[0], dec_seq)
    w_out = w_out_even[0].astype(BF16)
    h, xn, eid, we = _proj_router([(oa_p, oa_s), (or_p, or_s)], [w_out[:A_W], w_out[A_W:]], None, (xp, xs),
                                  norm2[0], routers[0], False)
    y = moe(0, xn, eid)
    hp = _moe_combine(h, y, we, norm2[0], False, 0, np_rows)
    hs = _moe_combine(h, y, we, norm2[0], False, np_rows, ns_rows)

    s5w = _s5_weights(s5_lam_re[0], s5_lam_im[0], s5_log_dt[0], s5_b_re[0], s5_b_im[0], s5_c_re[0], s5_c_im[0])
    n_state = s5_lam_re.shape[1] * s5_lam_re.shape[2]
    zeros = jnp.zeros((n_p, n_state), F32)
    zg_p, s5r_p, s5i_p = _s5_scan(hp.reshape(n_p, seq, d), norm1[1], s5w, s5_d[0], zeros, zeros, n_p, S5_PROMPT_STEPS)
    zg_s, s5r_s, s5i_s = _s5_scan(hs.reshape(n_s, dec_seq, d), norm1[1], s5w, s5_d[0],
                                  state_s5_re[0].reshape(n_s, n_state), state_s5_im[0].reshape(n_s, n_state),
                                  S5_SAMPLE_SEQS, dec_seq)
    h, xn, eid, we = _proj_router([(zg_p.reshape(np_rows, d), zg_s.reshape(ns_rows, d))], [s5_w_glu[0].astype(BF16)],
                                  s5_b_glu[0], (hp, hs), norm2[1], routers[1], True)
    y = moe(1, xn, eid)
    y_prompt = _moe_combine(h, y, we, norm_f, True, 0, np_rows).reshape(n_p, seq, d)
    y_sample = _moe_combine(h, y, we, norm_f, True, np_rows, ns_rows).reshape(n_s, dec_seq, d)

    kv_p = lambda a: a.reshape(1, n_p, seq // page, H_A, HD_A, page).transpose(0, 1, 2, 5, 3, 4)
    kv_s = lambda a: a.reshape(1, n_s, dec_seq, H_A, HD_A)
    st = lambda a, n: a.reshape((1, n) + s5_lam_re.shape[1:])
    return (y_prompt, y_sample, kv_p(kt_p), kv_p(vt_p), kv_s(ka_s), kv_s(va_s), ret_p[None], ret_s[None],
            st(s5r_p, n_p), st(s5i_p, n_p), st(s5r_s, n_s), st(s5i_s, n_s))
```

```python
import functools
import math

import jax
import jax.numpy as jnp
from jax import lax
from jax.experimental import pallas as pl
from jax.experimental.pallas import tpu as pltpu

F32 = jnp.float32
BF16 = jnp.bfloat16

H_A, HD_A = 8, 64
ROT_DIM = HD_A // 4
ROPE_THETA = 500000.0
MOBA_BLOCK = 256
MOBA_TOPK = 3
H_R, DK_R, DV_R = 8, 64, 128
RET_CHUNK = 128
S5_GROUP, S5_STATE = 16, 64
MOE_GROUPS, MOE_PER_GROUP, MOE_TOPK = 4, 8, 2
N_EXPERTS = MOE_GROUPS * MOE_PER_GROUP
A_W = H_A * HD_A
R_QK = H_R * DK_R
R_V = H_R * DV_R
NEG_INF = -1e30
EPS = 1e-6

LANES = 128
SUBLANES = 8
VMEM_LIMIT = 56 * 1024 * 1024

ROW_TILE = 256
PROJ_TILE = 512
MOE_ROWS = 256
TOKEN_SUBLANES = 8


def _nt(a, b):
    return lax.dot_general(a, b, (((1,), (1,)), ((), ())), preferred_element_type=F32)


def _tn(a, b):
    return lax.dot_general(a, b, (((0,), (0,)), ((), ())), preferred_element_type=F32)


def _mm(a, b):
    return jnp.dot(a, b, preferred_element_type=F32)


def _rms_rows(x, g):
    return x * lax.rsqrt(jnp.mean(x * x, axis=-1, keepdims=True) + EPS) * g


def _split_bf16(x):
    hi = x.astype(BF16)
    lo = (x - hi.astype(F32)).astype(BF16)
    return hi, lo


def _params(sem):
    return pltpu.CompilerParams(dimension_semantics=sem, vmem_limit_bytes=VMEM_LIMIT)


def _rotate_into(out_ref, z, c_ref, s_ref, shift, first, scale):
    for j in range(z.shape[1] // LANES):
        sl = slice(j * LANES, (j + 1) * LANES)
        zc = z[:, sl]
        up = pltpu.roll(zc, LANES - shift, axis=1)
        dn = pltpu.roll(zc, shift, axis=1)
        r = zc * c_ref[:, sl] + jnp.where(first, up, dn) * s_ref[:, sl]
        out_ref[:, sl] = r if scale is None else r * scale


def _pick_rows(n_ptiles, p_ref, s_ref):
    return jnp.where(pl.program_id(0) < n_ptiles, p_ref[...], s_ref[...])


def _split_rows(tm, width, n_ptiles):
    return (pl.BlockSpec((tm, width), lambda i: (jnp.minimum(i, n_ptiles - 1), 0)),
            pl.BlockSpec((tm, width), lambda i: (jnp.maximum(i - n_ptiles, 0), 0)))


def _store_pages_t(src_ref, dst_ref):
    n_pages, _, _, page = dst_ref.shape
    per = LANES // HD_A
    for j in range(A_W // LANES):
        t = src_ref[:, j * LANES:(j + 1) * LANES].T
        for p in range(n_pages):
            dst_ref[p, j * per:(j + 1) * per] = t[:, p * page:(p + 1) * page].reshape(per, HD_A, page)


def _in_proj_kernel(x_ref, g_ref, w_ref, ca_ref, sa_ref, cr_ref, sr_ref,
                    qa_ref, ka_ref, va_ref, qr_ref, kr_ref, vr_ref, gr_ref, *page_refs):
    xn = _rms_rows(x_ref[...], g_ref[...]).astype(BF16)
    lane = lax.broadcasted_iota(jnp.int32, (1, LANES), 1)
    first_a = (lane % HD_A) < (ROT_DIM // 2)
    first_r = (lane % 2) == 0

    def sec(lo, width):
        return _mm(xn, w_ref[:, lo:lo + width])

    _rotate_into(qa_ref, sec(0, A_W), ca_ref, sa_ref, ROT_DIM // 2, first_a, None)
    _rotate_into(ka_ref, sec(A_W, A_W), ca_ref, sa_ref, ROT_DIM // 2, first_a, None)
    va_ref[...] = sec(2 * A_W, A_W)
    _rotate_into(qr_ref, sec(3 * A_W, R_QK), cr_ref, sr_ref, 1, first_r, None)
    _rotate_into(kr_ref, sec(3 * A_W + R_QK, R_QK), cr_ref, sr_ref, 1, first_r, DK_R ** -0.5)
    vr_ref[...] = sec(3 * A_W + 2 * R_QK, R_V)
    gr_ref[...] = sec(3 * A_W + 2 * R_QK + R_V, R_V)
    if page_refs:
        _store_pages_t(ka_ref, page_refs[0])
        _store_pages_t(va_ref, page_refs[1])


def _rope_tables(pos):
    half = ROT_DIM // 2
    inv = ROPE_THETA ** (-jnp.arange(half, dtype=F32) / half)
    ang = pos.astype(F32)[:, None] * inv[None, :]
    cos, sin = jnp.cos(ang), jnp.sin(ang)
    rest = HD_A - ROT_DIM
    c = jnp.concatenate([cos, cos, jnp.ones((pos.shape[0], rest), F32)], axis=-1)
    s = jnp.concatenate([-sin, sin, jnp.zeros((pos.shape[0], rest), F32)], axis=-1)
    return jnp.tile(c, (1, H_A)), jnp.tile(s, (1, H_A))


def _retnet_tables(pos):
    n = DK_R // 2
    inv = 1.0 / (10000.0 ** jnp.linspace(0.0, 1.0, n, dtype=F32))
    ang = pos.astype(F32)[:, None] * inv[None, :]
    cos, sin = jnp.cos(ang), jnp.sin(ang)
    c = jnp.repeat(cos, 2, axis=-1)
    s = jnp.stack([-sin, sin], axis=-1).reshape(pos.shape[0], DK_R)
    return jnp.tile(c, (1, H_R)), jnp.tile(s, (1, H_R))


def _in_proj(x, g, w_bf16, pos, page=None):
    n, d = x.shape
    tm = ROW_TILE
    period_tiles = pos.shape[0] // tm
    ca, sa = _rope_tables(pos)
    cr, sr = _retnet_tables(pos)
    row = lambda w: pl.BlockSpec((tm, w), lambda i: (i, 0))
    tab = pl.BlockSpec((tm, A_W), lambda i: (i % period_tiles, 0))
    widths = (A_W, A_W, A_W, R_QK, R_QK, R_V, R_V)
    out_specs = [row(w) for w in widths]
    out_shape = [jax.ShapeDtypeStruct((n, w), F32) for w in widths]
    if page is not None:
        out_specs += [pl.BlockSpec((tm // page, H_A, HD_A, page), lambda i: (i, 0, 0, 0))] * 2
        out_shape += [jax.ShapeDtypeStruct((n // page, H_A, HD_A, page), F32)] * 2
    return pl.pallas_call(
        _in_proj_kernel,
        grid=(n // tm,),
        in_specs=[row(d), pl.BlockSpec((1, d), lambda i: (0, 0)),
                  pl.BlockSpec(w_bf16.shape, lambda i: (0, 0)), tab, tab, tab, tab],
        out_specs=out_specs,
        out_shape=out_shape,
        compiler_params=_params(("arbitrary",)),
        name="in_proj",
    )(x, g.reshape(1, d), w_bf16, ca, sa, cr, sr)


def _moba_select(q_f32, kmean, n_valid, eye):
    n_blk = kmean.shape[0]
    qh, ql = _split_bf16(q_f32)
    kh, kl = _split_bf16(kmean)
    st = _nt(jnp.concatenate([kh, kl, kh], axis=1), jnp.concatenate([qh, qh, ql], axis=1))
    jrow = lax.broadcasted_iota(jnp.int32, st.shape, 0)
    rank = jnp.zeros(st.shape, F32)
    for jp in range(n_blk):
        sj = st[jp:jp + 1, :]
        beats = (sj > st) | ((sj == st) & (jp < jrow))
        rank = rank + jnp.where(beats & (jp < n_valid), 1.0, 0.0)
    sel_t = jnp.where((jrow < n_valid) & (rank < MOBA_TOPK), 1.0, 0.0)
    sel_t = jnp.concatenate([sel_t, jnp.zeros((LANES - n_blk, st.shape[1]), F32)], axis=0).astype(BF16)
    return _nt(eye, sel_t)


def _moba_prompt_kernel(q_ref, k_ref, v_ref, o_ref, kb_ref, vb_ref, km_ref, bias_ref, s_ref):
    b = pl.program_id(2)
    blk = MOBA_BLOCK
    n_blk = k_ref.shape[0] // blk
    half = blk // 2

    lane = lax.broadcasted_iota(jnp.int32, (1, LANES), 1)
    r_i = lax.broadcasted_iota(jnp.int32, (blk, blk), 0)
    c_i = lax.broadcasted_iota(jnp.int32, (blk, blk), 1)
    heads = range(LANES // HD_A)
    hms = [(lane // HD_A) == hh for hh in heads]

    @pl.when(b == 0)
    def _():
        kb_ref[...] = k_ref[...].astype(BF16)
        vb_ref[...] = v_ref[...].astype(BF16)
        for j in range(n_blk):
            km_ref[j:j + 1, :] = jnp.mean(k_ref[j * blk:(j + 1) * blk, :], axis=0, keepdims=True)
        eye = jnp.where(r_i == c_i, 1.0, 0.0).astype(BF16)
        for hh in heads:
            bias_ref[hh, 0:blk, :] = jnp.full((blk, LANES), NEG_INF, BF16)
        for qb in range(1, n_blk):
            rows = slice(qb * blk, (qb + 1) * blk)
            for hh in heads:
                selq = _moba_select(jnp.where(hms[hh], q_ref[rows, :], 0.0), km_ref[...], qb, eye)
                bias_ref[hh, rows, :] = jnp.where(selq > 0.5, 0.0, NEG_INF).astype(BF16)

    own = pl.ds(pl.multiple_of(b * blk, blk), blk)
    q = q_ref[own, :]
    fold = lambda x: (x[:, :half], x[:, half:])
    q_aug, s_own, mx = [], [], []
    for hh in heads:
        qs = (jnp.where(hms[hh], q, 0.0) * (HD_A ** -0.5)).astype(BF16)
        so = jnp.where(c_i <= r_i, _nt(qs, kb_ref[own, :]), NEG_INF)
        q_aug.append(jnp.concatenate([qs, bias_ref[hh, own, :]], axis=1))
        s_own.append(so)
        mx.append(jnp.maximum(*fold(so)))

    def rows_of(j):
        return pl.ds(pl.multiple_of(j * blk, blk), blk)

    def scores(j, mx):
        onehot = jnp.broadcast_to(jnp.where(lane == j, 1.0, 0.0).astype(BF16), (blk, LANES))
        k_aug = jnp.concatenate([kb_ref[rows_of(j), :], onehot], axis=1)
        out = []
        for hh in heads:
            s = _nt(q_aug[hh], k_aug)
            s_ref[hh, j] = s
            out.append(jnp.maximum(mx[hh], jnp.maximum(*fold(s))))
        return tuple(out)

    mx = lax.fori_loop(0, b, scores, tuple(mx))
    ms = [jnp.max(mx[hh], axis=1, keepdims=True) for hh in heads]

    def weights(s, hh, vj):
        p = jnp.exp(s - ms[hh])
        lo, hi = fold(p)
        return lo + hi, _mm(p.astype(BF16), vj)

    def accumulate(j, carry):
        vj = vb_ref[rows_of(j), :]
        out = []
        for hh in heads:
            l_part, pv = weights(s_ref[hh, j], hh, vj)
            out.append((carry[hh][0] + l_part, carry[hh][1] + pv))
        return tuple(out)

    carry = lax.fori_loop(0, b, accumulate, tuple(weights(s_own[hh], hh, vb_ref[own, :]) for hh in heads))
    out = jnp.zeros(q.shape, F32)
    for hh in heads:
        l_part, acc = carry[hh]
        out = out + jnp.where(hms[hh], acc / jnp.sum(l_part, axis=1, keepdims=True), 0.0)
    o_ref[...] = out


def _moba_prompt(q_a, k_a, v_a, n_seq, seq):
    blk = MOBA_BLOCK
    n_blk = seq // blk
    ospec = pl.BlockSpec((blk, LANES), lambda s, h, b: (s * n_blk + b, h))
    kspec = pl.BlockSpec((seq, LANES), lambda s, h, b: (s, h))
    per = LANES // HD_A
    return pl.pallas_call(
        _moba_prompt_kernel,
        grid=(n_seq, A_W // LANES, n_blk),
        in_specs=[kspec, kspec, kspec],
        out_specs=ospec,
        out_shape=jax.ShapeDtypeStruct((n_seq * seq, A_W), F32),
        scratch_shapes=[pltpu.VMEM((seq, LANES), BF16), pltpu.VMEM((seq, LANES), BF16),
                        pltpu.VMEM((n_blk, LANES), F32), pltpu.VMEM((per, seq, LANES), BF16),
                        pltpu.VMEM((per, n_blk, blk, blk), F32)],
        compiler_params=_params(("arbitrary", "arbitrary", "arbitrary")),
        name="moba_prompt",
    )(q_a, k_a, v_a)


def _ret_decay_tables(c):
    log_g = jnp.log(1.0 - 2.0 ** (-5.0 - jnp.arange(H_R, dtype=F32)))
    i = jnp.arange(c, dtype=F32)
    diff = i[:, None] - i[None, :]
    dmat = jnp.where(diff >= 0, jnp.exp(jnp.maximum(diff, 0.0)[None] * log_g[:, None, None]), 0.0)
    dq = jnp.exp((i + 1.0)[None, :] * log_g[:, None])
    dk = jnp.exp((c - 1.0 - i)[None, :] * log_g[:, None])
    dc = jnp.exp(c * log_g)
    return dmat, dq, dk, dc


def _ret_chunk(q, k, v, g, s, dmat, dq, dk, dc):
    att = _nt(q.astype(BF16), k.astype(BF16)) * dmat
    o = _mm(att.astype(BF16), v.astype(BF16)) + _mm((q * dq).astype(BF16), s.astype(BF16))
    s = s * dc + _tn((k * dk).astype(BF16), v.astype(BF16))
    o = o * lax.rsqrt(jnp.mean(o * o, axis=-1, keepdims=True) + EPS)
    return o * (g * jax.nn.sigmoid(g)), s


def _ret_prompt_kernel(q_ref, k_ref, v_ref, g_ref, dmat_ref, dq_ref, dk_ref, dc_ref, o_ref, st_ref):
    c = RET_CHUNK
    per = LANES // DK_R
    lane = lax.broadcasted_iota(jnp.int32, (1, LANES), 1)

    def chunk(i, states):
        rows = pl.ds(pl.multiple_of(i * c, c), c)
        q_all, k_all = q_ref[rows, :], k_ref[rows, :]
        out = []
        for hh in range(per):
            hm = (lane // DK_R) == hh
            vl = slice(hh * DV_R, (hh + 1) * DV_R)
            o, s = _ret_chunk(jnp.where(hm, q_all, 0.0), jnp.where(hm, k_all, 0.0), v_ref[rows, vl], g_ref[rows, vl],
                              states[hh], dmat_ref[hh], dq_ref[hh], dk_ref[hh], dc_ref[hh])
            o_ref[rows, vl] = o
            out.append(s)
        return tuple(out)

    states = lax.fori_loop(0, q_ref.shape[0] // c, chunk, tuple(jnp.zeros((LANES, DV_R), F32) for _ in range(per)),
                           unroll=4)
    for hh in range(per):
        st_ref[hh] = states[hh][hh * DK_R:(hh + 1) * DK_R, :]


def _ret_tables_bcast(c, rows):
    dmat, dq, dk, dc = _ret_decay_tables(c)
    pad = rows - c
    dmat = jnp.pad(dmat, ((0, 0), (0, pad), (0, pad)))
    dq = jnp.broadcast_to(jnp.pad(dq, ((0, 0), (0, pad)))[:, :, None], (H_R, rows, LANES))
    dk = jnp.broadcast_to(jnp.pad(dk, ((0, 0), (0, pad)))[:, :, None], (H_R, rows, LANES))
    dc = jnp.broadcast_to(dc[:, None, None], (H_R, 1, LANES))
    return dmat, dq, dk, dc


def _ret_prompt(q_r, k_r, v_r, g_r, n_seq, seq):
    c = RET_CHUNK
    dmat, dq, dk, dc = _ret_tables_bcast(c, c)
    per = LANES // DK_R
    qk = pl.BlockSpec((seq, LANES), lambda s, h: (s, h))
    vg = pl.BlockSpec((seq, per * DV_R), lambda s, h: (s, h))
    tab = lambda r: pl.BlockSpec((per, r, LANES), lambda s, h: (h, 0, 0))
    return pl.pallas_call(
        _ret_prompt_kernel,
        grid=(n_seq, H_R // per),
        in_specs=[qk, qk, vg, vg, tab(c), tab(c), tab(c), tab(1)],
        out_specs=[vg, pl.BlockSpec((None, per, DK_R, DV_R), lambda s, h: (s, h, 0, 0))],
        out_shape=[jax.ShapeDtypeStruct((n_seq * seq, R_V), F32),
                   jax.ShapeDtypeStruct((n_seq, H_R, DK_R, DV_R), F32)],
        compiler_params=_params(("arbitrary", "arbitrary")),
        name="ret_prompt",
    )(q_r, k_r, v_r, g_r, dmat, dq, dk, dc)


def _pad_rows(x, rows):
    return jnp.concatenate([x, jnp.zeros((rows - x.shape[0], x.shape[1]), x.dtype)], axis=0)


def _moba_sample_kernel(pt_ref, q_ref, kn_ref, vn_ref, *rest, n_pages):
    k_refs, v_refs = rest[:n_pages], rest[n_pages:2 * n_pages]
    o_ref, s_ref = rest[2 * n_pages], rest[2 * n_pages + 1]
    q = q_ref[...]
    t = q.shape[0]
    page = k_refs[0].shape[-1]
    per_blk = MOBA_BLOCK // page
    n_blk = n_pages // per_blk
    lane = lax.broadcasted_iota(jnp.int32, (1, A_W), 1)
    qbd = jnp.concatenate([jnp.where((lane // HD_A) == h, q, 0.0) for h in range(H_A)], axis=0)
    qs = (qbd * (HD_A ** -0.5)).astype(BF16)
    rows = qbd.shape[0]

    bsum = [jnp.zeros((rows, 1), F32) for _ in range(n_blk)]
    for p in range(n_pages):
        sp = _mm(qs, k_refs[p][...].reshape(A_W, page).astype(BF16))
        s_ref[:, p * page:(p + 1) * page] = sp
        bsum[p // per_blk] = bsum[p // per_blk] + jnp.sum(sp, axis=1, keepdims=True)
    sel = []
    for j in range(n_blk):
        rank = jnp.zeros((rows, 1), F32)
        for jp in range(n_blk):
            if jp != j:
                beats = (bsum[jp] > bsum[j]) | (bsum[jp] == bsum[j]) if jp < j else bsum[jp] > bsum[j]
                rank = rank + jnp.where(beats, 1.0, 0.0)
        sel.append(rank < MOBA_TOPK)

    qi = lax.broadcasted_iota(jnp.int32, (rows, LANES), 0) % t
    causal = lax.broadcasted_iota(jnp.int32, (rows, LANES), 1) <= qi
    s0 = jnp.where(causal, _nt(qs, _pad_rows(kn_ref[...], LANES).astype(BF16)), NEG_INF)
    m = jnp.max(s0, axis=1, keepdims=True)
    p0 = jnp.exp(s0 - m)
    carry = (m, jnp.sum(p0, axis=1, keepdims=True), _mm(p0.astype(BF16), _pad_rows(vn_ref[...], LANES).astype(BF16)))
    for p in range(n_pages):
        m, l, acc = carry
        s = jnp.where(sel[p // per_blk], s_ref[:, p * page:(p + 1) * page], NEG_INF)
        m_new = jnp.maximum(m, jnp.max(s, axis=1, keepdims=True))
        alpha = jnp.exp(m - m_new)
        pr = jnp.exp(s - m_new)
        carry = (m_new, alpha * l + jnp.sum(pr, axis=1, keepdims=True),
                 alpha * acc + _nt(pr.astype(BF16), v_refs[p][...].reshape(A_W, page).astype(BF16)))
    _, l, acc = carry
    o = acc / l
    out = jnp.zeros((t, A_W), F32)
    for h in range(H_A):
        out = out + jnp.where((lane // HD_A) == h, o[h * t:(h + 1) * t, :], 0.0)
    o_ref[...] = out


def _moba_sample(q_a, k_a, v_a, cache_kt, cache_vt, page_table, dec_seq):
    n_s, n_pages = page_table.shape
    new = pl.BlockSpec((dec_seq, A_W), lambda n, pt: (n, 0))
    pages = [pl.BlockSpec((None,) + cache_kt.shape[1:], lambda n, pt, p=p: (pt[n * n_pages + p], 0, 0, 0))
             for p in range(n_pages)]
    grid_spec = pltpu.PrefetchScalarGridSpec(
        num_scalar_prefetch=1,
        grid=(n_s,),
        in_specs=[new, new, new] + pages + pages,
        out_specs=pl.BlockSpec((dec_seq, A_W), lambda n, pt: (n, 0)),
        scratch_shapes=[pltpu.VMEM((H_A * dec_seq, n_pages * cache_kt.shape[-1]), F32)],
    )
    return pl.pallas_call(
        functools.partial(_moba_sample_kernel, n_pages=n_pages),
        grid_spec=grid_spec,
        out_shape=jax.ShapeDtypeStruct((n_s * dec_seq, A_W), F32),
        compiler_params=_params(("arbitrary",)),
        name="moba_sample",
    )(page_table.reshape(-1), q_a, k_a, v_a, *([cache_kt] * n_pages), *([cache_vt] * n_pages))


RET_SAMPLE_SEQS = 4


def _ret_sample_kernel(q_ref, k_ref, v_ref, g_ref, s0_ref, dmat_ref, dq_ref, dk_ref, dc_ref, o_ref, st_ref):
    n_seq = s0_ref.shape[0]
    t = q_ref.shape[0] // n_seq
    lane = lax.broadcasted_iota(jnp.int32, (1, LANES), 1)
    per = LANES // DK_R
    zero_half = jnp.zeros((DK_R, DV_R), F32)
    for i in range(n_seq):
        rows = slice(i * t, (i + 1) * t)
        for h in range(H_R):
            hh = h % per
            hm = (lane // DK_R) == hh
            qk_l = slice((h // per) * LANES, (h // per + 1) * LANES)
            v_l = slice(h * DV_R, (h + 1) * DV_R)
            q = _pad_rows(jnp.where(hm, q_ref[rows, qk_l], 0.0), LANES)
            k = _pad_rows(jnp.where(hm, k_ref[rows, qk_l], 0.0), LANES)
            v = _pad_rows(v_ref[rows, v_l], LANES)
            g = _pad_rows(g_ref[rows, v_l], LANES)
            halves = [zero_half] * per
            halves[hh] = s0_ref[i, h]
            o, s = _ret_chunk(q, k, v, g, jnp.concatenate(halves, axis=0),
                              dmat_ref[h], dq_ref[h], dk_ref[h], dc_ref[h])
            o_ref[rows, v_l] = o[:t, :]
            st_ref[i, h] = s[hh * DK_R:(hh + 1) * DK_R, :]


def _ret_sample(q_r, k_r, v_r, g_r, s0, dec_seq):
    n_s = s0.shape[0]
    ns = RET_SAMPLE_SEQS
    dmat, dq, dk, dc = _ret_tables_bcast(dec_seq, LANES)
    qk = pl.BlockSpec((ns * dec_seq, R_QK), lambda n: (n, 0))
    vg = pl.BlockSpec((ns * dec_seq, R_V), lambda n: (n, 0))
    st = pl.BlockSpec((ns, H_R, DK_R, DV_R), lambda n: (n, 0, 0, 0))
    full = lambda a: pl.BlockSpec(a.shape, lambda n: (0,) * a.ndim)
    return pl.pallas_call(
        _ret_sample_kernel,
        grid=(n_s // ns,),
        in_specs=[qk, qk, vg, vg, st, full(dmat), full(dq), full(dk), full(dc)],
        out_specs=[vg, st],
        out_shape=[jax.ShapeDtypeStruct((n_s * dec_seq, R_V), F32), jax.ShapeDtypeStruct(s0.shape, F32)],
        compiler_params=_params(("arbitrary",)),
        name="ret_sample",
    )(q_r, k_r, v_r, g_r, s0, dmat, dq, dk, dc)


ROUTER_ROWS = SUBLANES + N_EXPERTS


def _route(xn, wr_ref, br_ref, eid_ref, we_ref):
    xh, xl = _split_bf16(xn)
    lt = _nt(wr_ref[...], jnp.concatenate([xh, xh, xl], axis=1)) + br_ref[...]
    tm = lt.shape[1]
    r8 = lax.broadcasted_iota(jnp.int32, (SUBLANES, tm), 0)
    lg = jnp.where(r8 < MOE_GROUPS, lt[:SUBLANES, :], NEG_INF)
    mg = jnp.max(lg, axis=0, keepdims=True)
    wg = 1.0 / jnp.sum(jnp.exp(lg - mg), axis=0, keepdims=True)
    gidx = jnp.min(jnp.where(lg == mg, r8, SUBLANES), axis=0, keepdims=True)
    le = jnp.zeros((MOE_PER_GROUP, tm), F32)
    for gi in range(MOE_GROUPS):
        lo = SUBLANES + gi * MOE_PER_GROUP
        le = le + jnp.where(gidx == gi, lt[lo:lo + MOE_PER_GROUP, :], 0.0)
    v1 = jnp.max(le, axis=0, keepdims=True)
    i1 = jnp.min(jnp.where(le == v1, r8, MOE_PER_GROUP), axis=0, keepdims=True)
    le2 = jnp.where(r8 == i1, -jnp.inf, le)
    v2 = jnp.max(le2, axis=0, keepdims=True)
    i2 = jnp.min(jnp.where(le2 == v2, r8, MOE_PER_GROUP), axis=0, keepdims=True)
    e21 = jnp.exp(v2 - v1)
    w1 = wg / (1.0 + e21)
    eid_ref[0:1, :] = gidx * MOE_PER_GROUP + i1
    eid_ref[1:2, :] = gidx * MOE_PER_GROUP + i2
    we_ref[0:1, :] = w1
    we_ref[1:2, :] = w1 * e21


def _proj_router_kernel(*refs, n_in, glu, n_ptiles):
    a_refs, w_refs = refs[:2 * n_in], refs[2 * n_in:3 * n_in]
    k = 3 * n_in
    b_ref = refs[k] if glu else None
    k += int(glu)
    hp_ref, hs_ref, g_ref, wr_ref, br_ref, ho_ref, xn_ref, eid_ref, we_ref = refs[k:k + 9]
    acc = None
    for j, w_ref in enumerate(w_refs):
        a = _pick_rows(n_ptiles, a_refs[2 * j], a_refs[2 * j + 1])
        part = _mm(a.astype(BF16), w_ref[...])
        acc = part if acc is None else acc + part
    if glu:
        acc = acc + b_ref[...]
        half = acc.shape[1] // 2
        acc = acc[:, :half] * jax.nn.sigmoid(acc[:, half:])
    hn = _pick_rows(n_ptiles, hp_ref, hs_ref) + acc
    ho_ref[...] = hn
    xn = _rms_rows(hn, g_ref[...])
    for j in range(TOKEN_SUBLANES):
        xn_ref[pl.ds(j, xn.shape[0], stride=TOKEN_SUBLANES), :] = xn[:, j * LANES:(j + 1) * LANES]
    _route(xn, wr_ref, br_ref, eid_ref, we_ref)


def _router_weights(w_group, b_group, w_expert, b_expert):
    d = w_group.shape[0]
    wt = jnp.concatenate([w_group.T, jnp.zeros((SUBLANES - MOE_GROUPS, d), F32), w_expert.T], axis=0)
    hi = wt.astype(BF16)
    lo = (wt - hi.astype(F32)).astype(BF16)
    bias = jnp.concatenate([b_group, jnp.zeros((SUBLANES - MOE_GROUPS,), F32), b_expert]).reshape(-1, 1)
    return jnp.concatenate([hi, lo, hi], axis=1), bias


def _proj_router(acts, weights, bias, h, g, router, glu):
    d = h[0].shape[1]
    n = h[0].shape[0] + h[1].shape[0]
    tm = PROJ_TILE
    n_ptiles = h[0].shape[0] // tm
    wr, br = router
    row = lambda w: pl.BlockSpec((tm, w), lambda i: (i, 0))
    full = lambda a: pl.BlockSpec(a.shape, lambda i: (0,) * a.ndim)
    tok = pl.BlockSpec((MOE_TOPK, tm), lambda i: (0, i))
    operands = ([a for pair in acts for a in pair] + list(weights) + ([bias.reshape(1, -1)] if glu else [])
                + [h[0], h[1], g.reshape(1, d), wr, br])
    in_specs = ([spec for pair in acts for spec in _split_rows(tm, pair[0].shape[1], n_ptiles)]
                + [full(w) for w in weights]
                + ([pl.BlockSpec((1, bias.shape[0]), lambda i: (0, 0))] if glu else [])
                + [*_split_rows(tm, d, n_ptiles), pl.BlockSpec((1, d), lambda i: (0, 0)), full(wr), full(br)])
    return pl.pallas_call(
        functools.partial(_proj_router_kernel, n_in=len(acts), glu=glu, n_ptiles=n_ptiles),
        grid=(n // tm,),
        in_specs=in_specs,
        out_specs=[row(d), pl.BlockSpec((tm * TOKEN_SUBLANES, LANES), lambda i: (i, 0)), tok, tok],
        out_shape=[jax.ShapeDtypeStruct((n, d), F32), jax.ShapeDtypeStruct((n * TOKEN_SUBLANES, LANES), F32),
                   jax.ShapeDtypeStruct((MOE_TOPK, n), jnp.int32), jax.ShapeDtypeStruct((MOE_TOPK, n), F32)],
        compiler_params=_params(("arbitrary",)),
        name="glu_router" if glu else "out_proj_router",
    )(*operands)


def _moe_plan(eid):
    n_tok = eid.shape[1]
    n_assign = MOE_TOPK * n_tok
    blk = MOE_ROWS
    n_blocks = -(-n_assign // blk) + N_EXPERTS
    e_flat = eid.reshape(-1)
    experts = jnp.arange(N_EXPERTS, dtype=jnp.int32)
    counts = jnp.sum((e_flat[:, None] == experts[None, :]).astype(jnp.int32), axis=0)
    order = jnp.argsort(e_flat).astype(jnp.int32)
    pc = (counts + blk - 1) // blk * blk
    pend = jnp.cumsum(pc)
    pstart = pend - pc
    start = jnp.cumsum(counts) - counts
    n_steps = n_blocks + 2
    first_row = jnp.arange(n_steps, dtype=jnp.int32) * blk
    blk_e = jnp.minimum(jnp.sum((pend[None, :] <= first_row[:, None]).astype(jnp.int32), axis=1), N_EXPERTS - 1)
    lane = jnp.arange(blk, dtype=jnp.int32)[None, :]
    off = first_row[:, None] + lane - pstart[blk_e][:, None]
    valid = off < counts[blk_e][:, None]
    a_row = order[jnp.clip(start[blk_e][:, None] + off, 0, n_assign - 1)]
    a_row = jnp.where(valid, a_row, 0)
    src = a_row % n_tok
    dst = jnp.concatenate([n_assign + lane, jnp.where(valid, a_row, n_assign + lane)], axis=0)
    n_used = (pend[-1] // blk).astype(jnp.int32).reshape(1)
    return blk_e, n_used, src.reshape(-1), dst.reshape(-1), n_steps


def _moe_kernel(blk_e_ref, n_used_ref, src_ref, dst_ref, x_hbm, w13_ref, w2_ref, y_hbm,
                xbuf, ybuf, w13b, w2b, gsem, ssem):
    b = pl.program_id(0)
    n_used = n_used_ref[0]
    blk = MOE_ROWS
    slot = b % 2

    ts = TOKEN_SUBLANES

    def gather_row(block, slot, r):
        tok_row = pl.multiple_of(src_ref[block * blk + r] * ts, ts)
        return pltpu.make_async_copy(x_hbm.at[pl.ds(tok_row, ts), :],
                                     xbuf.at[slot, pl.ds(r * ts, ts), :], gsem.at[slot])

    def scatter_row(block, slot, r):
        out_row = pl.multiple_of(dst_ref[block * blk + r] * ts, ts)
        return pltpu.make_async_copy(ybuf.at[slot, pl.ds(r * ts, ts), :],
                                     y_hbm.at[pl.ds(out_row, ts), :], ssem.at[slot])

    def gather_all(slot):
        return pltpu.make_async_copy(x_hbm.at[pl.ds(0, blk * ts), :], xbuf.at[slot], gsem.at[slot])

    def scatter_all(slot):
        return pltpu.make_async_copy(ybuf.at[slot], y_hbm.at[pl.ds(0, blk * ts), :], ssem.at[slot])

    def issue_row_copies():
        for r in range(blk):
            gather_row(b + 1, 1 - slot, r).start(priority=r % 2)
        for r in range(blk):
            scatter_row(b, 1 - slot, r).start(priority=r % 2)

    @pl.when(b == 0)
    def _():
        xbuf[...] = jnp.zeros(xbuf.shape, xbuf.dtype)
        ybuf[...] = jnp.zeros(ybuf.shape, ybuf.dtype)
        fill = pltpu.make_async_copy(ybuf.at[0], y_hbm.at[pl.ds(y_hbm.shape[0] - blk * ts, blk * ts), :], ssem.at[0])
        fill.start()
        fill.wait()
        for r in range(blk):
            gather_row(0, 0, r).start(priority=r % 2)

    @pl.when(b <= n_used)
    def _():
        gather_all(slot).wait()

    @pl.when((b >= 1) & (b <= n_used + 1))
    def _():
        scatter_all(slot).wait()

    @pl.when(b < n_used)
    def _():
        @pl.when((b == 0) | (blk_e_ref[b] != blk_e_ref[jnp.maximum(b - 1, 0)]))
        def _():
            w13b[...] = w13_ref[...].astype(BF16)
            w2b[...] = w2_ref[...].astype(BF16)

        issue_row_copies()
        x = jnp.concatenate([xbuf[slot, pl.ds(j, blk, stride=ts), :] for j in range(ts)], axis=1)
        hb = _mm(x.astype(BF16), w13b[...])
        half = hb.shape[1] // 2
        gate = hb[:, :half]
        act = gate * jax.nn.sigmoid(gate) * hb[:, half:]
        y = _mm(act.astype(BF16), w2b[...])
        for j in range(ts):
            ybuf[slot, pl.ds(j, blk, stride=ts), :] = y[:, j * LANES:(j + 1) * LANES]

    @pl.when(b == n_used)
    def _():
        for r in range(blk):
            scatter_row(b, 1 - slot, r).start(priority=r % 2)


def _moe_experts(xn, eid, w13, w2, layer):
    n_tok, d = xn.shape[0] // TOKEN_SUBLANES, w13.shape[2]
    blk = MOE_ROWS
    blk_e, n_used, src_tok, dst_row, n_steps = _moe_plan(eid)
    grid_spec = pltpu.PrefetchScalarGridSpec(
        num_scalar_prefetch=4,
        grid=(n_steps,),
        in_specs=[pl.BlockSpec(memory_space=pl.ANY),
                  pl.BlockSpec((None, None) + w13.shape[2:], lambda b, be, *_: (layer, be[b], 0, 0)),
                  pl.BlockSpec((None, None) + w2.shape[2:], lambda b, be, *_: (layer, be[b], 0, 0))],
        out_specs=pl.BlockSpec(memory_space=pl.ANY),
        scratch_shapes=[pltpu.VMEM((2, blk * TOKEN_SUBLANES, LANES), F32),
                        pltpu.VMEM((2, blk * TOKEN_SUBLANES, LANES), F32),
                        pltpu.VMEM(w13.shape[2:], BF16), pltpu.VMEM(w2.shape[2:], BF16),
                        pltpu.SemaphoreType.DMA((2,)), pltpu.SemaphoreType.DMA((2,))],
    )
    return pl.pallas_call(
        _moe_kernel,
        grid_spec=grid_spec,
        out_shape=jax.ShapeDtypeStruct(((MOE_TOPK * n_tok + blk) * TOKEN_SUBLANES, LANES), F32),
        compiler_params=_params(("arbitrary",)),
        name="moe_experts",
    )(blk_e, n_used, src_tok, dst_row, xn, w13, w2)


def _combine_kernel(h_ref, y0_ref, y1_ref, w_ref, g_ref, o_ref, *, norm):
    w = w_ref[...]
    rows, ts = h_ref.shape[0], TOKEN_SUBLANES

    def token_rows(y_ref):
        return jnp.concatenate([y_ref[pl.ds(j, rows, stride=ts), :] for j in range(ts)], axis=1)

    h = h_ref[...] + (token_rows(y0_ref) * w[:, 0:1] + token_rows(y1_ref) * w[:, 1:2])
    o_ref[...] = _rms_rows(h, g_ref[...]) if norm else h


def _moe_combine(h, y, we, g, norm, row_lo, n_rows):
    n, d = h.shape
    tm = ROW_TILE
    lo = row_lo // tm
    return pl.pallas_call(
        functools.partial(_combine_kernel, norm=norm),
        grid=(n_rows // tm,),
        in_specs=[pl.BlockSpec((tm, d), lambda i: (lo + i, 0)),
                  pl.BlockSpec((tm * TOKEN_SUBLANES, LANES), lambda i: (lo + i, 0)),
                  pl.BlockSpec((tm * TOKEN_SUBLANES, LANES), lambda i: (n // tm + lo + i, 0)),
                  pl.BlockSpec((tm, MOE_TOPK), lambda i: (lo + i, 0)), pl.BlockSpec((1, d), lambda i: (0, 0))],
        out_specs=pl.BlockSpec((tm, d), lambda i: (i, 0)),
        out_shape=jax.ShapeDtypeStruct((n_rows, d), F32),
        compiler_params=_params(("arbitrary",)),
        name="moe_combine_norm" if norm else "moe_combine",
    )(h, y, y, we.T, g.reshape(1, d))


S5_CH = LANES
S5_ST = (LANES // S5_GROUP) * S5_STATE
S5_SCAN_LANES = 512


def _s5_kernel(h_ref, g_ref, bre_ref, bim_ref, are_ref, aim_ref, cre_ref, cim_ref, d_ref, s0r_ref, s0i_ref,
               zg_ref, fr_ref, fi_ref, xr, xi, sr, si, u_tm, z_tm, *, s, steps):
    c = pl.program_id(1)

    @pl.when(c == 0)
    def _():
        sr[...] = s0r_ref[...]
        si[...] = s0i_ref[...]

    d = h_ref.shape[-1]
    n_ch = d // S5_CH
    u_nat = _rms_rows(h_ref[...].reshape(s * steps, d), g_ref[...])
    for n in range(s):
        for j in range(n_ch):
            u_tm[j, pl.ds(n, steps, stride=s), :] = u_nat[n * steps:(n + 1) * steps, j * S5_CH:(j + 1) * S5_CH]
    for j in range(n_ch):
        uj = u_tm[j].astype(BF16)
        xr[:, j * S5_ST:(j + 1) * S5_ST] = _mm(uj, bre_ref[j])
        xi[:, j * S5_ST:(j + 1) * S5_ST] = _mm(uj, bim_ref[j])

    for lo in range(0, xr.shape[1], S5_SCAN_LANES):
        ls = slice(lo, lo + S5_SCAN_LANES)
        ar = jnp.broadcast_to(are_ref[:, ls], (s, S5_SCAN_LANES))
        ai = jnp.broadcast_to(aim_ref[:, ls], (s, S5_SCAN_LANES))

        pr, pi = sr[:, ls], si[:, ls]
        for t in range(steps):
            rows = slice(t * s, (t + 1) * s)
            pr, pi = ar * pr - ai * pi + xr[rows, ls], ar * pi + ai * pr + xi[rows, ls]
            xr[rows, ls] = pr
            xi[rows, ls] = pi
        sr[:, ls] = pr
        si[:, ls] = pi

    for j in range(n_ch):
        cs = slice(j * S5_CH, (j + 1) * S5_CH)
        ss = slice(j * S5_ST, (j + 1) * S5_ST)
        y = _mm(xr[:, ss].astype(BF16), cre_ref[j]) - _mm(xi[:, ss].astype(BF16), cim_ref[j])
        z_tm[j] = jax.nn.gelu(y + d_ref[:, cs] * u_tm[j])
    for n in range(s):
        for j in range(n_ch):
            zg_ref[n, :, j * S5_CH:(j + 1) * S5_CH] = z_tm[j, pl.ds(n, steps, stride=s), :]

    @pl.when(c == pl.num_programs(1) - 1)
    def _():
        fr_ref[...] = sr[...]
        fi_ref[...] = si[...]


def _s5_weights(lam_re, lam_im, log_dt, b_re, b_im, c_re, c_im):
    dt = jnp.exp(log_dt)[:, None]
    mag = jnp.exp(lam_re * dt)
    ang = lam_im * dt
    ab_re = mag * jnp.cos(ang)
    ab_im = mag * jnp.sin(ang)
    den = lam_re * lam_re + lam_im * lam_im
    nr = ab_re - 1.0
    co_re = (nr * lam_re + ab_im * lam_im) / den
    co_im = (ab_im * lam_re - nr * lam_im) / den
    bb_re = co_re[..., None] * b_re - co_im[..., None] * b_im
    bb_im = co_re[..., None] * b_im + co_im[..., None] * b_re
    per = S5_CH // S5_GROUP
    n_ch = lam_re.shape[0] // per
    eye = jnp.eye(per, dtype=F32)

    def in_blocks(bb):
        w = bb.transpose(0, 2, 1).reshape(n_ch, per, S5_GROUP, S5_STATE)
        return jnp.einsum('jgcp,gh->jgchp', w, eye).reshape(n_ch, S5_CH, S5_ST).astype(BF16)

    def out_blocks(cc):
        w = cc.transpose(0, 2, 1).reshape(n_ch, per, S5_STATE, S5_GROUP)
        return jnp.einsum('jgpc,gh->jgphc', w, eye).reshape(n_ch, S5_ST, S5_CH).astype(BF16)

    return (in_blocks(bb_re), in_blocks(bb_im), ab_re.reshape(1, -1), ab_im.reshape(1, -1),
            out_blocks(c_re), out_blocks(c_im))


def _s5_scan(h, g, weights, d_skip, s0_re, s0_im, s, steps):
    n_seq, t_len, d = h.shape
    n_groups = n_seq // s
    rows = steps * s
    chunks = t_len // steps
    bre, bim, are, aim, cre, cim = weights
    n_state = are.shape[1]
    full = lambda a: pl.BlockSpec(a.shape, lambda gi, c: (0,) * a.ndim)
    row = pl.BlockSpec((s, steps, d), lambda gi, c: (gi, c, 0))
    state = pl.BlockSpec((s, n_state), lambda gi, c: (gi, 0))
    return pl.pallas_call(
        functools.partial(_s5_kernel, s=s, steps=steps),
        grid=(n_groups, chunks),
        in_specs=[row, pl.BlockSpec((1, d), lambda gi, c: (0, 0)), full(bre), full(bim), full(are), full(aim),
                  full(cre), full(cim), pl.BlockSpec((1, d), lambda gi, c: (0, 0)), state, state],
        out_specs=[row, state, state],
        out_shape=[jax.ShapeDtypeStruct(h.shape, F32), jax.ShapeDtypeStruct(s0_re.shape, F32),
                   jax.ShapeDtypeStruct(s0_re.shape, F32)],
        scratch_shapes=[pltpu.VMEM((rows, n_state), F32), pltpu.VMEM((rows, n_state), F32),
                        pltpu.VMEM((s, n_state), F32), pltpu.VMEM((s, n_state), F32),
                        pltpu.VMEM((d // S5_CH, rows, S5_CH), F32), pltpu.VMEM((d // S5_CH, rows, S5_CH), F32)],
        compiler_params=_params(("arbitrary", "arbitrary")),
        name="s5_scan",
    )(h, g.reshape(1, d), bre, bim, are, aim, cre, cim, d_skip.reshape(1, d), s0_re, s0_im)


S5_PROMPT_STEPS = 32
S5_SAMPLE_SEQS = 32


def kernel(x_prompt, x_sample, cache_k, cache_v, state_ret, state_s5_re, state_s5_im, page_table, norm1, norm2, norm_f, w_in_even, w_out_even, s5_lam_re, s5_lam_im, s5_log_dt, s5_b_re, s5_b_im, s5_c_re, s5_c_im, s5_d, s5_w_glu, s5_b_glu, moe_w_group, moe_b_group, moe_w_expert, moe_b_expert, moe_w13, moe_w2):
    n_p, seq, d = x_prompt.shape
    n_s, dec_seq, _ = x_sample.shape
    pool, page = cache_k.shape[1], cache_k.shape[2]
    past_len = page_table.shape[1] * page
    np_rows, ns_rows = n_p * seq, n_s * dec_seq
    routers = [_router_weights(moe_w_group[li], moe_b_group[li], moe_w_expert[li], moe_b_expert[li])
               for li in range(2)]

    def moe(li, xn, eid):
        return _moe_experts(xn, eid, moe_w13, moe_w2, li)

    xp, xs = x_prompt.reshape(np_rows, d), x_sample.reshape(ns_rows, d)
    w_in = w_in_even[0].astype(BF16)
    qa_p, ka_p, va_p, qr_p, kr_p, vr_p, gr_p, kt_p, vt_p = _in_proj(xp, norm1[0], w_in, jnp.arange(seq), page)
    qa_s, ka_s, va_s, qr_s, kr_s, vr_s, gr_s = _in_proj(xs, norm1[0], w_in,
                                                        past_len + jnp.arange(ROW_TILE) % dec_seq)
    oa_p = _moba_prompt(qa_p, ka_p, va_p, n_p, seq)
    or_p, ret_p = _ret_prompt(qr_p, kr_p, vr_p, gr_p, n_p, seq)
    oa_s = _moba_sample(qa_s, ka_s, va_s, cache_k[0].transpose(0, 2, 3, 1), cache_v[0].transpose(0, 2, 3, 1),
                        page_table, dec_seq)
    or_s, ret_s = _ret_sample(qr_s, kr_s, vr_s, gr_s, state_ret[0], dec_seq)
    w_out = w_out_even[0].astype(BF16)
    h, xn, eid, we = _proj_router([(oa_p, oa_s), (or_p, or_s)], [w_out[:A_W], w_out[A_W:]], None, (xp, xs),
                                  norm2[0], routers[0], False)
    y = moe(0, xn, eid)
    hp = _moe_combine(h, y, we, norm2[0], False, 0, np_rows)
    hs = _moe_combine(h, y, we, norm2[0], False, np_rows, ns_rows)

    s5w = _s5_weights(s5_lam_re[0], s5_lam_im[0], s5_log_dt[0], s5_b_re[0], s5_b_im[0], s5_c_re[0], s5_c_im[0])
    n_state = s5_lam_re.shape[1] * s5_lam_re.shape[2]
    zeros = jnp.zeros((n_p, n_state), F32)
    zg_p, s5r_p, s5i_p = _s5_scan(hp.reshape(n_p, seq, d), norm1[1], s5w, s5_d[0], zeros, zeros, n_p, S5_PROMPT_STEPS)
    zg_s, s5r_s, s5i_s = _s5_scan(hs.reshape(n_s, dec_seq, d), norm1[1], s5w, s5_d[0],
                                  state_s5_re[0].reshape(n_s, n_state), state_s5_im[0].reshape(n_s, n_state),
                                  S5_SAMPLE_SEQS, dec_seq)
    h, xn, eid, we = _proj_router([(zg_p.reshape(np_rows, d), zg_s.reshape(ns_rows, d))], [s5_w_glu[0].astype(BF16)],
                                  s5_b_glu[0], (hp, hs), norm2[1], routers[1], True)
    y = moe(1, xn, eid)
    y_prompt = _moe_combine(h, y, we, norm_f, True, 0, np_rows).reshape(n_p, seq, d)
    y_sample = _moe_combine(h, y, we, norm_f, True, np_rows, ns_rows).reshape(n_s, dec_seq, d)

    kv_p = lambda a: a.reshape(1, n_p, seq // page, H_A, HD_A, page).transpose(0, 1, 2, 5, 3, 4)
    kv_s = lambda a: a.reshape(1, n_s, dec_seq, H_A, HD_A)
    st = lambda a, n: a.reshape((1, n) + s5_lam_re.shape[1:])
    return (y_prompt, y_sample, kv_p(kt_p), kv_p(vt_p), kv_s(ka_s), kv_s(va_s), ret_p[None], ret_s[None],
            st(s5r_p, n_p), st(s5i_p, n_p), st(s5r_s, n_s), st(s5i_s, n_s))
```

```python
import functools

import jax
import jax.numpy as jnp
from jax import lax
from jax.experimental import pallas as pl
from jax.experimental.pallas import tpu as pltpu

F32 = jnp.float32
BF16 = jnp.bfloat16

H_A, HD_A = 8, 64
ROT_DIM = HD_A // 4
ROPE_THETA = 500000.0
MOBA_BLOCK = 256
MOBA_TOPK = 3
H_R, DK_R, DV_R = 8, 64, 128
RET_CHUNK = 128
S5_GROUP, S5_STATE = 16, 64
MOE_GROUPS, MOE_PER_GROUP, MOE_TOPK = 4, 8, 2
N_EXPERTS = MOE_GROUPS * MOE_PER_GROUP
A_W = H_A * HD_A
R_QK = H_R * DK_R
R_V = H_R * DV_R
NEG_INF = -1e30
EPS = 1e-6

LANES = 128
SUBLANES = 8
VMEM_LIMIT = 56 * 1024 * 1024

ROW_TILE = 256
PROJ_TILE = 512
MOE_ROWS = 256
TOKEN_SUBLANES = 8


def _nt(a, b):
    return lax.dot_general(a, b, (((1,), (1,)), ((), ())), preferred_element_type=F32)


def _tn(a, b):
    return lax.dot_general(a, b, (((0,), (0,)), ((), ())), preferred_element_type=F32)


def _mm(a, b):
    return jnp.dot(a, b, preferred_element_type=F32)


def _rms_rows(x, g):
    return x * lax.rsqrt(jnp.mean(x * x, axis=-1, keepdims=True) + EPS) * g


def _split_bf16(x):
    hi = x.astype(BF16)
    lo = (x - hi.astype(F32)).astype(BF16)
    return hi, lo


def _params(sem):
    return pltpu.CompilerParams(dimension_semantics=sem, vmem_limit_bytes=VMEM_LIMIT)


def _rotate_into(out_ref, z, c_ref, s_ref, shift, first, scale):
    for j in range(z.shape[1] // LANES):
        sl = slice(j * LANES, (j + 1) * LANES)
        zc = z[:, sl]
        up = pltpu.roll(zc, LANES - shift, axis=1)
        dn = pltpu.roll(zc, shift, axis=1)
        r = zc * c_ref[:, sl] + jnp.where(first, up, dn) * s_ref[:, sl]
        out_ref[:, sl] = r if scale is None else r * scale


def _pick_rows(n_ptiles, p_ref, s_ref):
    return jnp.where(pl.program_id(0) < n_ptiles, p_ref[...], s_ref[...])


def _split_rows(tm, width, n_ptiles):
    return (pl.BlockSpec((tm, width), lambda i: (jnp.minimum(i, n_ptiles - 1), 0)),
            pl.BlockSpec((tm, width), lambda i: (jnp.maximum(i - n_ptiles, 0), 0)))


def _store_pages_t(src_ref, dst_ref):
    n_pages, _, _, page = dst_ref.shape
    per = LANES // HD_A
    for j in range(A_W // LANES):
        t = src_ref[:, j * LANES:(j + 1) * LANES].T
        for p in range(n_pages):
            dst_ref[p, j * per:(j + 1) * per] = t[:, p * page:(p + 1) * page].reshape(per, HD_A, page)


def _in_proj_kernel(x_ref, g_ref, w_ref, ca_ref, sa_ref, cr_ref, sr_ref,
                    qa_ref, ka_ref, va_ref, qr_ref, kr_ref, vr_ref, gr_ref, *page_refs):
    xn = _rms_rows(x_ref[...], g_ref[...]).astype(BF16)
    lane = lax.broadcasted_iota(jnp.int32, (1, LANES), 1)
    first_a = (lane % HD_A) < (ROT_DIM // 2)
    first_r = (lane % 2) == 0

    def sec(lo, width):
        return _mm(xn, w_ref[:, lo:lo + width])

    _rotate_into(qa_ref, sec(0, A_W), ca_ref, sa_ref, ROT_DIM // 2, first_a, None)
    _rotate_into(ka_ref, sec(A_W, A_W), ca_ref, sa_ref, ROT_DIM // 2, first_a, None)
    va_ref[...] = sec(2 * A_W, A_W)
    _rotate_into(qr_ref, sec(3 * A_W, R_QK), cr_ref, sr_ref, 1, first_r, None)
    _rotate_into(kr_ref, sec(3 * A_W + R_QK, R_QK), cr_ref, sr_ref, 1, first_r, DK_R ** -0.5)
    vr_ref[...] = sec(3 * A_W + 2 * R_QK, R_V)
    gr_ref[...] = sec(3 * A_W + 2 * R_QK + R_V, R_V)
    if page_refs:
        _store_pages_t(ka_ref, page_refs[0])
        _store_pages_t(va_ref, page_refs[1])


def _rope_tables(pos):
    half = ROT_DIM // 2
    inv = ROPE_THETA ** (-jnp.arange(half, dtype=F32) / half)
    ang = pos.astype(F32)[:, None] * inv[None, :]
    cos, sin = jnp.cos(ang), jnp.sin(ang)
    rest = HD_A - ROT_DIM
    c = jnp.concatenate([cos, cos, jnp.ones((pos.shape[0], rest), F32)], axis=-1)
    s = jnp.concatenate([-sin, sin, jnp.zeros((pos.shape[0], rest), F32)], axis=-1)
    return jnp.tile(c, (1, H_A)), jnp.tile(s, (1, H_A))


def _retnet_tables(pos):
    n = DK_R // 2
    inv = 1.0 / (10000.0 ** jnp.linspace(0.0, 1.0, n, dtype=F32))
    ang = pos.astype(F32)[:, None] * inv[None, :]
    cos, sin = jnp.cos(ang), jnp.sin(ang)
    c = jnp.repeat(cos, 2, axis=-1)
    s = jnp.stack([-sin, sin], axis=-1).reshape(pos.shape[0], DK_R)
    return jnp.tile(c, (1, H_R)), jnp.tile(s, (1, H_R))


def _in_proj(x, g, w_bf16, pos, page=None):
    n, d = x.shape
    tm = ROW_TILE
    period_tiles = pos.shape[0] // tm
    ca, sa = _rope_tables(pos)
    cr, sr = _retnet_tables(pos)
    row = lambda w: pl.BlockSpec((tm, w), lambda i: (i, 0))
    tab = pl.BlockSpec((tm, A_W), lambda i: (i % period_tiles, 0))
    widths = (A_W, A_W, A_W, R_QK, R_QK, R_V, R_V)
    out_specs = [row(w) for w in widths]
    out_shape = [jax.ShapeDtypeStruct((n, w), F32) for w in widths]
    if page is not None:
        out_specs += [pl.BlockSpec((tm // page, H_A, HD_A, page), lambda i: (i, 0, 0, 0))] * 2
        out_shape += [jax.ShapeDtypeStruct((n // page, H_A, HD_A, page), F32)] * 2
    return pl.pallas_call(
        _in_proj_kernel,
        grid=(n // tm,),
        in_specs=[row(d), pl.BlockSpec((1, d), lambda i: (0, 0)),
                  pl.BlockSpec(w_bf16.shape, lambda i: (0, 0)), tab, tab, tab, tab],
        out_specs=out_specs,
        out_shape=out_shape,
        compiler_params=_params(("arbitrary",)),
        name="in_proj",
    )(x, g.reshape(1, d), w_bf16, ca, sa, cr, sr)


def _moba_select(q_f32, kmean, n_valid, eye):
    n_blk = kmean.shape[0]
    qh, ql = _split_bf16(q_f32)
    kh, kl = _split_bf16(kmean)
    st = _nt(jnp.concatenate([kh, kl, kh], axis=1), jnp.concatenate([qh, qh, ql], axis=1))
    jrow = lax.broadcasted_iota(jnp.int32, st.shape, 0)
    rank = jnp.zeros(st.shape, F32)
    for jp in range(n_blk):
        sj = st[jp:jp + 1, :]
        beats = (sj > st) | ((sj == st) & (jp < jrow))
        rank = rank + jnp.where(beats & (jp < n_valid), 1.0, 0.0)
    sel_t = jnp.where((jrow < n_valid) & (rank < MOBA_TOPK), 1.0, 0.0)
    sel_t = jnp.concatenate([sel_t, jnp.zeros((LANES - n_blk, st.shape[1]), F32)], axis=0).astype(BF16)
    return _nt(eye, sel_t)


def _moba_prompt_kernel(q_ref, k_ref, v_ref, o_ref, kb_ref, vb_ref, km_ref, bias_ref, s_ref):
    b = pl.program_id(2)
    blk = MOBA_BLOCK
    n_blk = k_ref.shape[0] // blk
    half = blk // 2

    lane = lax.broadcasted_iota(jnp.int32, (1, LANES), 1)
    r_i = lax.broadcasted_iota(jnp.int32, (blk, blk), 0)
    c_i = lax.broadcasted_iota(jnp.int32, (blk, blk), 1)
    heads = range(LANES // HD_A)
    hms = [(lane // HD_A) == hh for hh in heads]

    @pl.when(b == 0)
    def _():
        kb_ref[...] = k_ref[...].astype(BF16)
        vb_ref[...] = v_ref[...].astype(BF16)
        for j in range(n_blk):
            km_ref[j:j + 1, :] = jnp.mean(k_ref[j * blk:(j + 1) * blk, :], axis=0, keepdims=True)
        eye = jnp.where(r_i == c_i, 1.0, 0.0).astype(BF16)
        for hh in heads:
            bias_ref[hh, 0:blk, :] = jnp.full((blk, LANES), NEG_INF, BF16)
        for qb in range(1, n_blk):
            rows = slice(qb * blk, (qb + 1) * blk)
            for hh in heads:
                selq = _moba_select(jnp.where(hms[hh], q_ref[rows, :], 0.0), km_ref[...], qb, eye)
                bias_ref[hh, rows, :] = jnp.where(selq > 0.5, 0.0, NEG_INF).astype(BF16)

    own = pl.ds(pl.multiple_of(b * blk, blk), blk)
    q = q_ref[own, :]
    fold = lambda x: (x[:, :half], x[:, half:])
    q_aug, s_own, mx = [], [], []
    for hh in heads:
        qs = (jnp.where(hms[hh], q, 0.0) * (HD_A ** -0.5)).astype(BF16)
        so = jnp.where(c_i <= r_i, _nt(qs, kb_ref[own, :]), NEG_INF)
        q_aug.append(jnp.concatenate([qs, bias_ref[hh, own, :]], axis=1))
        s_own.append(so)
        mx.append(jnp.maximum(*fold(so)))

    def rows_of(j):
        return pl.ds(pl.multiple_of(j * blk, blk), blk)

    def scores(j, mx):
        onehot = jnp.broadcast_to(jnp.where(lane == j, 1.0, 0.0).astype(BF16), (blk, LANES))
        k_aug = jnp.concatenate([kb_ref[rows_of(j), :], onehot], axis=1)
        out = []
        for hh in heads:
            s = _nt(q_aug[hh], k_aug)
            s_ref[hh, j] = s
            out.append(jnp.maximum(mx[hh], jnp.maximum(*fold(s))))
        return tuple(out)

    mx = lax.fori_loop(0, b, scores, tuple(mx))
    ms = [jnp.max(mx[hh], axis=1, keepdims=True) for hh in heads]

    def weights(s, hh, vj):
        p = jnp.exp(s - ms[hh])
        lo, hi = fold(p)
        return lo + hi, _mm(p.astype(BF16), vj)

    def accumulate(j, carry):
        vj = vb_ref[rows_of(j), :]
        out = []
        for hh in heads:
            l_part, pv = weights(s_ref[hh, j], hh, vj)
            out.append((carry[hh][0] + l_part, carry[hh][1] + pv))
        return tuple(out)

    carry = lax.fori_loop(0, b, accumulate, tuple(weights(s_own[hh], hh, vb_ref[own, :]) for hh in heads))
    out = jnp.zeros(q.shape, F32)
    for hh in heads:
        l_part, acc = carry[hh]
        out = out + jnp.where(hms[hh], acc / jnp.sum(l_part, axis=1, keepdims=True), 0.0)
    o_ref[...] = out


def _moba_prompt(q_a, k_a, v_a, n_seq, seq):
    blk = MOBA_BLOCK
    n_blk = seq // blk
    ospec = pl.BlockSpec((blk, LANES), lambda s, h, b: (s * n_blk + b, h))
    kspec = pl.BlockSpec((seq, LANES), lambda s, h, b: (s, h))
    per = LANES // HD_A
    return pl.pallas_call(
        _moba_prompt_kernel,
        grid=(n_seq, A_W // LANES, n_blk),
        in_specs=[kspec, kspec, kspec],
        out_specs=ospec,
        out_shape=jax.ShapeDtypeStruct((n_seq * seq, A_W), F32),
        scratch_shapes=[pltpu.VMEM((seq, LANES), BF16), pltpu.VMEM((seq, LANES), BF16),
                        pltpu.VMEM((n_blk, LANES), F32), pltpu.VMEM((per, seq, LANES), BF16),
                        pltpu.VMEM((per, n_blk, blk, blk), F32)],
        compiler_params=_params(("arbitrary", "arbitrary", "arbitrary")),
        name="moba_prompt",
    )(q_a, k_a, v_a)


def _ret_decay_tables(c):
    log_g = jnp.log(1.0 - 2.0 ** (-5.0 - jnp.arange(H_R, dtype=F32)))
    i = jnp.arange(c, dtype=F32)
    diff = i[:, None] - i[None, :]
    dmat = jnp.where(diff >= 0, jnp.exp(jnp.maximum(diff, 0.0)[None] * log_g[:, None, None]), 0.0)
    dq = jnp.exp((i + 1.0)[None, :] * log_g[:, None])
    dk = jnp.exp((c - 1.0 - i)[None, :] * log_g[:, None])
    dc = jnp.exp(c * log_g)
    return dmat, dq, dk, dc


def _ret_chunk(q, k, v, g, s, dmat, dq, dk, dc):
    att = _nt(q.astype(BF16), k.astype(BF16)) * dmat
    o = _mm(att.astype(BF16), v.astype(BF16)) + _mm((q * dq).astype(BF16), s.astype(BF16))
    s = s * dc + _tn((k * dk).astype(BF16), v.astype(BF16))
    o = o * lax.rsqrt(jnp.mean(o * o, axis=-1, keepdims=True) + EPS)
    return o * (g * jax.nn.sigmoid(g)), s


RET_PROMPT_HEADS = 4


def _ret_prompt_kernel(q_ref, k_ref, v_ref, g_ref, dmat_ref, dq_ref, dk_ref, dc_ref, o_ref, st_ref):
    c = RET_CHUNK
    per = LANES // DK_R
    n_heads = st_ref.shape[0]
    lane = lax.broadcasted_iota(jnp.int32, (1, LANES), 1)

    def chunk(i, states):
        rows = pl.ds(pl.multiple_of(i * c, c), c)
        out = []
        for h in range(n_heads):
            ql = slice((h // per) * LANES, (h // per + 1) * LANES)
            hm = (lane // DK_R) == h % per
            vl = slice(h * DV_R, (h + 1) * DV_R)
            o, s = _ret_chunk(jnp.where(hm, q_ref[rows, ql], 0.0), jnp.where(hm, k_ref[rows, ql], 0.0),
                              v_ref[rows, vl], g_ref[rows, vl], states[h], dmat_ref[h], dq_ref[h], dk_ref[h], dc_ref[h])
            o_ref[rows, vl] = o
            out.append(s)
        return tuple(out)

    states = lax.fori_loop(0, q_ref.shape[0] // c, chunk, tuple(jnp.zeros((LANES, DV_R), F32) for _ in range(n_heads)),
                           unroll=4)
    for h in range(n_heads):
        st_ref[h] = states[h][(h % per) * DK_R:(h % per + 1) * DK_R, :]


def _ret_tables_bcast(c, rows):
    dmat, dq, dk, dc = _ret_decay_tables(c)
    pad = rows - c
    dmat = jnp.pad(dmat, ((0, 0), (0, pad), (0, pad)))
    dq = jnp.broadcast_to(jnp.pad(dq, ((0, 0), (0, pad)))[:, :, None], (H_R, rows, LANES))
    dk = jnp.broadcast_to(jnp.pad(dk, ((0, 0), (0, pad)))[:, :, None], (H_R, rows, LANES))
    dc = jnp.broadcast_to(dc[:, None, None], (H_R, 1, LANES))
    return dmat, dq, dk, dc


def _ret_prompt(q_r, k_r, v_r, g_r, n_seq, seq):
    c = RET_CHUNK
    dmat, dq, dk, dc = _ret_tables_bcast(c, c)
    nh = RET_PROMPT_HEADS
    qk = pl.BlockSpec((seq, nh * DK_R), lambda s, h: (s, h))
    vg = pl.BlockSpec((seq, nh * DV_R), lambda s, h: (s, h))
    tab = lambda r: pl.BlockSpec((nh, r, LANES), lambda s, h: (h, 0, 0))
    return pl.pallas_call(
        _ret_prompt_kernel,
        grid=(n_seq, H_R // nh),
        in_specs=[qk, qk, vg, vg, tab(c), tab(c), tab(c), tab(1)],
        out_specs=[vg, pl.BlockSpec((None, nh, DK_R, DV_R), lambda s, h: (s, h, 0, 0))],
        out_shape=[jax.ShapeDtypeStruct((n_seq * seq, R_V), F32),
                   jax.ShapeDtypeStruct((n_seq, H_R, DK_R, DV_R), F32)],
        compiler_params=_params(("arbitrary", "arbitrary")),
        name="ret_prompt",
    )(q_r, k_r, v_r, g_r, dmat, dq, dk, dc)


def _pad_rows(x, rows):
    return jnp.concatenate([x, jnp.zeros((rows - x.shape[0], x.shape[1]), x.dtype)], axis=0)


def _moba_sample_kernel(pt_ref, q_ref, kn_ref, vn_ref, *rest, n_pages):
    k_refs, v_refs = rest[:n_pages], rest[n_pages:2 * n_pages]
    o_ref, s_ref = rest[2 * n_pages], rest[2 * n_pages + 1]
    q = q_ref[...]
    t = q.shape[0]
    page = k_refs[0].shape[-1]
    per_blk = MOBA_BLOCK // page
    n_blk = n_pages // per_blk
    lane = lax.broadcasted_iota(jnp.int32, (1, A_W), 1)
    qbd = jnp.concatenate([jnp.where((lane // HD_A) == h, q, 0.0) for h in range(H_A)], axis=0)
    qs = (qbd * (HD_A ** -0.5)).astype(BF16)
    rows = qbd.shape[0]

    bsum = [jnp.zeros((rows, 1), F32) for _ in range(n_blk)]
    for p in range(n_pages):
        sp = _mm(qs, k_refs[p][...].reshape(A_W, page).astype(BF16))
        s_ref[:, p * page:(p + 1) * page] = sp
        bsum[p // per_blk] = bsum[p // per_blk] + jnp.sum(sp, axis=1, keepdims=True)
    sel = []
    for j in range(n_blk):
        rank = jnp.zeros((rows, 1), F32)
        for jp in range(n_blk):
            if jp != j:
                beats = (bsum[jp] > bsum[j]) | (bsum[jp] == bsum[j]) if jp < j else bsum[jp] > bsum[j]
                rank = rank + jnp.where(beats, 1.0, 0.0)
        sel.append(rank < MOBA_TOPK)

    qi = lax.broadcasted_iota(jnp.int32, (rows, LANES), 0) % t
    causal = lax.broadcasted_iota(jnp.int32, (rows, LANES), 1) <= qi
    s0 = jnp.where(causal, _nt(qs, _pad_rows(kn_ref[...], LANES).astype(BF16)), NEG_INF)
    m = jnp.max(s0, axis=1, keepdims=True)
    p0 = jnp.exp(s0 - m)
    carry = (m, jnp.sum(p0, axis=1, keepdims=True), _mm(p0.astype(BF16), _pad_rows(vn_ref[...], LANES).astype(BF16)))
    for p in range(n_pages):
        m, l, acc = carry
        s = jnp.where(sel[p // per_blk], s_ref[:, p * page:(p + 1) * page], NEG_INF)
        m_new = jnp.maximum(m, jnp.max(s, axis=1, keepdims=True))
        alpha = jnp.exp(m - m_new)
        pr = jnp.exp(s - m_new)
        carry = (m_new, alpha * l + jnp.sum(pr, axis=1, keepdims=True),
                 alpha * acc + _nt(pr.astype(BF16), v_refs[p][...].reshape(A_W, page).astype(BF16)))
    _, l, acc = carry
    o = acc / l
    out = jnp.zeros((t, A_W), F32)
    for h in range(H_A):
        out = out + jnp.where((lane // HD_A) == h, o[h * t:(h + 1) * t, :], 0.0)
    o_ref[...] = out


def _moba_sample(q_a, k_a, v_a, cache_kt, cache_vt, page_table, dec_seq):
    n_s, n_pages = page_table.shape
    new = pl.BlockSpec((dec_seq, A_W), lambda n, pt: (n, 0))
    pages = [pl.BlockSpec((None,) + cache_kt.shape[1:], lambda n, pt, p=p: (pt[n * n_pages + p], 0, 0, 0))
             for p in range(n_pages)]
    grid_spec = pltpu.PrefetchScalarGridSpec(
        num_scalar_prefetch=1,
        grid=(n_s,),
        in_specs=[new, new, new] + pages + pages,
        out_specs=pl.BlockSpec((dec_seq, A_W), lambda n, pt: (n, 0)),
        scratch_shapes=[pltpu.VMEM((H_A * dec_seq, n_pages * cache_kt.shape[-1]), F32)],
    )
    return pl.pallas_call(
        functools.partial(_moba_sample_kernel, n_pages=n_pages),
        grid_spec=grid_spec,
        out_shape=jax.ShapeDtypeStruct((n_s * dec_seq, A_W), F32),
        compiler_params=_params(("arbitrary",)),
        name="moba_sample",
    )(page_table.reshape(-1), q_a, k_a, v_a, *([cache_kt] * n_pages), *([cache_vt] * n_pages))


RET_SAMPLE_SEQS = 4


def _ret_sample_kernel(q_ref, k_ref, v_ref, g_ref, s0_ref, dmat_ref, dq_ref, dk_ref, dc_ref, o_ref, st_ref):
    n_seq = s0_ref.shape[0]
    t = q_ref.shape[0] // n_seq
    lane = lax.broadcasted_iota(jnp.int32, (1, LANES), 1)
    per = LANES // DK_R
    zero_half = jnp.zeros((DK_R, DV_R), F32)
    for i in range(n_seq):
        rows = slice(i * t, (i + 1) * t)
        for h in range(H_R):
            hh = h % per
            hm = (lane // DK_R) == hh
            qk_l = slice((h // per) * LANES, (h // per + 1) * LANES)
            v_l = slice(h * DV_R, (h + 1) * DV_R)
            q = _pad_rows(jnp.where(hm, q_ref[rows, qk_l], 0.0), LANES)
            k = _pad_rows(jnp.where(hm, k_ref[rows, qk_l], 0.0), LANES)
            v = _pad_rows(v_ref[rows, v_l], LANES)
            g = _pad_rows(g_ref[rows, v_l], LANES)
            halves = [zero_half] * per
            halves[hh] = s0_ref[i, h]
            o, s = _ret_chunk(q, k, v, g, jnp.concatenate(halves, axis=0),
                              dmat_ref[h], dq_ref[h], dk_ref[h], dc_ref[h])
            o_ref[rows, v_l] = o[:t, :]
            st_ref[i, h] = s[hh * DK_R:(hh + 1) * DK_R, :]


def _ret_sample(q_r, k_r, v_r, g_r, s0, dec_seq):
    n_s = s0.shape[0]
    ns = RET_SAMPLE_SEQS
    dmat, dq, dk, dc = _ret_tables_bcast(dec_seq, LANES)
    qk = pl.BlockSpec((ns * dec_seq, R_QK), lambda n: (n, 0))
    vg = pl.BlockSpec((ns * dec_seq, R_V), lambda n: (n, 0))
    st = pl.BlockSpec((ns, H_R, DK_R, DV_R), lambda n: (n, 0, 0, 0))
    full = lambda a: pl.BlockSpec(a.shape, lambda n: (0,) * a.ndim)
    return pl.pallas_call(
        _ret_sample_kernel,
        grid=(n_s // ns,),
        in_specs=[qk, qk, vg, vg, st, full(dmat), full(dq), full(dk), full(dc)],
        out_specs=[vg, st],
        out_shape=[jax.ShapeDtypeStruct((n_s * dec_seq, R_V), F32), jax.ShapeDtypeStruct(s0.shape, F32)],
        compiler_params=_params(("arbitrary",)),
        name="ret_sample",
    )(q_r, k_r, v_r, g_r, s0, dmat, dq, dk, dc)


ROUTER_ROWS = SUBLANES + N_EXPERTS


def _route(xn, wr_ref, br_ref, eid_ref, we_ref):
    xh, xl = _split_bf16(xn)
    lt = _nt(wr_ref[...], jnp.concatenate([xh, xh, xl], axis=1)) + br_ref[...]
    tm = lt.shape[1]
    r8 = lax.broadcasted_iota(jnp.int32, (SUBLANES, tm), 0)
    lg = jnp.where(r8 < MOE_GROUPS, lt[:SUBLANES, :], NEG_INF)
    mg = jnp.max(lg, axis=0, keepdims=True)
    wg = 1.0 / jnp.sum(jnp.exp(lg - mg), axis=0, keepdims=True)
    gidx = jnp.min(jnp.where(lg == mg, r8, SUBLANES), axis=0, keepdims=True)
    le = jnp.zeros((MOE_PER_GROUP, tm), F32)
    for gi in range(MOE_GROUPS):
        lo = SUBLANES + gi * MOE_PER_GROUP
        le = le + jnp.where(gidx == gi, lt[lo:lo + MOE_PER_GROUP, :], 0.0)
    v1 = jnp.max(le, axis=0, keepdims=True)
    i1 = jnp.min(jnp.where(le == v1, r8, MOE_PER_GROUP), axis=0, keepdims=True)
    le2 = jnp.where(r8 == i1, -jnp.inf, le)
    v2 = jnp.max(le2, axis=0, keepdims=True)
    i2 = jnp.min(jnp.where(le2 == v2, r8, MOE_PER_GROUP), axis=0, keepdims=True)
    e21 = jnp.exp(v2 - v1)
    w1 = wg / (1.0 + e21)
    eid_ref[0:1, :] = gidx * MOE_PER_GROUP + i1
    eid_ref[1:2, :] = gidx * MOE_PER_GROUP + i2
    we_ref[0:1, :] = w1
    we_ref[1:2, :] = w1 * e21


def _proj_router_kernel(*refs, n_in, glu, n_ptiles):
    a_refs, w_refs = refs[:2 * n_in], refs[2 * n_in:3 * n_in]
    k = 3 * n_in
    b_ref = refs[k] if glu else None
    k += int(glu)
    hp_ref, hs_ref, g_ref, wr_ref, br_ref, ho_ref, xn_ref, eid_ref, we_ref = refs[k:k + 9]
    acc = None
    for j, w_ref in enumerate(w_refs):
        a = _pick_rows(n_ptiles, a_refs[2 * j], a_refs[2 * j + 1])
        part = _mm(a.astype(BF16), w_ref[...])
        acc = part if acc is None else acc + part
    if glu:
        acc = acc + b_ref[...]
        half = acc.shape[1] // 2
        acc = acc[:, :half] * jax.nn.sigmoid(acc[:, half:])
    hn = _pick_rows(n_ptiles, hp_ref, hs_ref) + acc
    ho_ref[...] = hn
    xn = _rms_rows(hn, g_ref[...])
    for j in range(TOKEN_SUBLANES):
        xn_ref[pl.ds(j, xn.shape[0], stride=TOKEN_SUBLANES), :] = xn[:, j * LANES:(j + 1) * LANES]
    _route(xn, wr_ref, br_ref, eid_ref, we_ref)


def _router_weights(w_group, b_group, w_expert, b_expert):
    d = w_group.shape[0]
    wt = jnp.concatenate([w_group.T, jnp.zeros((SUBLANES - MOE_GROUPS, d), F32), w_expert.T], axis=0)
    hi = wt.astype(BF16)
    lo = (wt - hi.astype(F32)).astype(BF16)
    bias = jnp.concatenate([b_group, jnp.zeros((SUBLANES - MOE_GROUPS,), F32), b_expert]).reshape(-1, 1)
    return jnp.concatenate([hi, lo, hi], axis=1), bias


def _proj_router(acts, weights, bias, h, g, router, glu):
    d = h[0].shape[1]
    n = h[0].shape[0] + h[1].shape[0]
    tm = PROJ_TILE
    n_ptiles = h[0].shape[0] // tm
    wr, br = router
    row = lambda w: pl.BlockSpec((tm, w), lambda i: (i, 0))
    full = lambda a: pl.BlockSpec(a.shape, lambda i: (0,) * a.ndim)
    tok = pl.BlockSpec((MOE_TOPK, tm), lambda i: (0, i))
    operands = ([a for pair in acts for a in pair] + list(weights) + ([bias.reshape(1, -1)] if glu else [])
                + [h[0], h[1], g.reshape(1, d), wr, br])
    in_specs = ([spec for pair in acts for spec in _split_rows(tm, pair[0].shape[1], n_ptiles)]
                + [full(w) for w in weights]
                + ([pl.BlockSpec((1, bias.shape[0]), lambda i: (0, 0))] if glu else [])
                + [*_split_rows(tm, d, n_ptiles), pl.BlockSpec((1, d), lambda i: (0, 0)), full(wr), full(br)])
    return pl.pallas_call(
        functools.partial(_proj_router_kernel, n_in=len(acts), glu=glu, n_ptiles=n_ptiles),
        grid=(n // tm,),
        in_specs=in_specs,
        out_specs=[row(d), pl.BlockSpec((tm * TOKEN_SUBLANES, LANES), lambda i: (i, 0)), tok, tok],
        out_shape=[jax.ShapeDtypeStruct((n, d), F32), jax.ShapeDtypeStruct((n * TOKEN_SUBLANES, LANES), F32),
                   jax.ShapeDtypeStruct((MOE_TOPK, n), jnp.int32), jax.ShapeDtypeStruct((MOE_TOPK, n), F32)],
        compiler_params=_params(("arbitrary",)),
        name="glu_router" if glu else "out_proj_router",
    )(*operands)


def _moe_plan(eid):
    n_tok = eid.shape[1]
    n_assign = MOE_TOPK * n_tok
    blk = MOE_ROWS
    n_blocks = -(-n_assign // blk) + N_EXPERTS
    e_flat = eid.reshape(-1)
    experts = jnp.arange(N_EXPERTS, dtype=jnp.int32)
    counts = jnp.sum((e_flat[:, None] == experts[None, :]).astype(jnp.int32), axis=0)
    order = jnp.argsort(e_flat).astype(jnp.int32)
    pc = (counts + blk - 1) // blk * blk
    pend = jnp.cumsum(pc)
    pstart = pend - pc
    start = jnp.cumsum(counts) - counts
    n_steps = n_blocks + 2
    first_row = jnp.arange(n_steps, dtype=jnp.int32) * blk
    blk_e = jnp.minimum(jnp.sum((pend[None, :] <= first_row[:, None]).astype(jnp.int32), axis=1), N_EXPERTS - 1)
    lane = jnp.arange(blk, dtype=jnp.int32)[None, :]
    off = first_row[:, None] + lane - pstart[blk_e][:, None]
    valid = off < counts[blk_e][:, None]
    a_row = order[jnp.clip(start[blk_e][:, None] + off, 0, n_assign - 1)]
    a_row = jnp.where(valid, a_row, 0)
    src = a_row % n_tok
    dst = jnp.concatenate([n_assign + lane, jnp.where(valid, a_row, n_assign + lane)], axis=0)
    n_used = (pend[-1] // blk).astype(jnp.int32).reshape(1)
    return blk_e, n_used, src.reshape(-1), dst.reshape(-1), n_steps


def _moe_kernel(blk_e_ref, n_used_ref, src_ref, dst_ref, x_hbm, w13_ref, w2_ref, y_hbm,
                xbuf, ybuf, w13b, w2b, gsem, ssem):
    b = pl.program_id(0)
    n_used = n_used_ref[0]
    blk = MOE_ROWS
    slot = b % 2

    ts = TOKEN_SUBLANES

    def gather_row(block, slot, r):
        tok_row = pl.multiple_of(src_ref[block * blk + r] * ts, ts)
        return pltpu.make_async_copy(x_hbm.at[pl.ds(tok_row, ts), :],
                                     xbuf.at[slot, pl.ds(r * ts, ts), :], gsem.at[slot])

    def scatter_row(block, slot, r):
        out_row = pl.multiple_of(dst_ref[block * blk + r] * ts, ts)
        return pltpu.make_async_copy(ybuf.at[slot, pl.ds(r * ts, ts), :],
                                     y_hbm.at[pl.ds(out_row, ts), :], ssem.at[slot])

    def gather_all(slot):
        return pltpu.make_async_copy(x_hbm.at[pl.ds(0, blk * ts), :], xbuf.at[slot], gsem.at[slot])

    def scatter_all(slot):
        return pltpu.make_async_copy(ybuf.at[slot], y_hbm.at[pl.ds(0, blk * ts), :], ssem.at[slot])

    def issue_row_copies():
        for r in range(blk):
            gather_row(b + 1, 1 - slot, r).start(priority=r % 2)
        for r in range(blk):
            scatter_row(b, 1 - slot, r).start(priority=r % 2)

    @pl.when(b == 0)
    def _():
        xbuf[...] = jnp.zeros(xbuf.shape, xbuf.dtype)
        ybuf[...] = jnp.zeros(ybuf.shape, ybuf.dtype)
        fill = pltpu.make_async_copy(ybuf.at[0], y_hbm.at[pl.ds(y_hbm.shape[0] - blk * ts, blk * ts), :], ssem.at[0])
        fill.start()
        fill.wait()
        for r in range(blk):
            gather_row(0, 0, r).start(priority=r % 2)

    @pl.when(b <= n_used)
    def _():
        gather_all(slot).wait()

    @pl.when((b >= 1) & (b <= n_used + 1))
    def _():
        scatter_all(slot).wait()

    @pl.when(b < n_used)
    def _():
        @pl.when((b == 0) | (blk_e_ref[b] != blk_e_ref[jnp.maximum(b - 1, 0)]))
        def _():
            w13b[...] = w13_ref[...].astype(BF16)
            w2b[...] = w2_ref[...].astype(BF16)

        issue_row_copies()
        x = jnp.concatenate([xbuf[slot, pl.ds(j, blk, stride=ts), :] for j in range(ts)], axis=1)
        hb = _mm(x.astype(BF16), w13b[...])
        half = hb.shape[1] // 2
        gate = hb[:, :half]
        act = gate * jax.nn.sigmoid(gate) * hb[:, half:]
        y = _mm(act.astype(BF16), w2b[...])
        for j in range(ts):
            ybuf[slot, pl.ds(j, blk, stride=ts), :] = y[:, j * LANES:(j + 1) * LANES]

    @pl.when(b == n_used)
    def _():
        for r in range(blk):
            scatter_row(b, 1 - slot, r).start(priority=r % 2)


def _moe_experts(xn, eid, w13, w2, layer):
    n_tok, d = xn.shape[0] // TOKEN_SUBLANES, w13.shape[2]
    blk = MOE_ROWS
    blk_e, n_used, src_tok, dst_row, n_steps = _moe_plan(eid)
    grid_spec = pltpu.PrefetchScalarGridSpec(
        num_scalar_prefetch=4,
        grid=(n_steps,),
        in_specs=[pl.BlockSpec(memory_space=pl.ANY),
                  pl.BlockSpec((None, None) + w13.shape[2:], lambda b, be, *_: (layer, be[b], 0, 0)),
                  pl.BlockSpec((None, None) + w2.shape[2:], lambda b, be, *_: (layer, be[b], 0, 0))],
        out_specs=pl.BlockSpec(memory_space=pl.ANY),
        scratch_shapes=[pltpu.VMEM((2, blk * TOKEN_SUBLANES, LANES), F32),
                        pltpu.VMEM((2, blk * TOKEN_SUBLANES, LANES), F32),
                        pltpu.VMEM(w13.shape[2:], BF16), pltpu.VMEM(w2.shape[2:], BF16),
                        pltpu.SemaphoreType.DMA((2,)), pltpu.SemaphoreType.DMA((2,))],
    )
    return pl.pallas_call(
        _moe_kernel,
        grid_spec=grid_spec,
        out_shape=jax.ShapeDtypeStruct(((MOE_TOPK * n_tok + blk) * TOKEN_SUBLANES, LANES), F32),
        compiler_params=_params(("arbitrary",)),
        name="moe_experts",
    )(blk_e, n_used, src_tok, dst_row, xn, w13, w2)


def _combine_kernel(h_ref, y0_ref, y1_ref, w_ref, g_ref, o_ref, *, norm):
    w = w_ref[...]
    rows, ts = h_ref.shape[0], TOKEN_SUBLANES

    def token_rows(y_ref):
        return jnp.concatenate([y_ref[pl.ds(j, rows, stride=ts), :] for j in range(ts)], axis=1)

    h = h_ref[...] + (token_rows(y0_ref) * w[:, 0:1] + token_rows(y1_ref) * w[:, 1:2])
    o_ref[...] = _rms_rows(h, g_ref[...]) if norm else h


def _moe_combine(h, y, we, g, norm, row_lo, n_rows):
    n, d = h.shape
    tm = ROW_TILE
    lo = row_lo // tm
    return pl.pallas_call(
        functools.partial(_combine_kernel, norm=norm),
        grid=(n_rows // tm,),
        in_specs=[pl.BlockSpec((tm, d), lambda i: (lo + i, 0)),
                  pl.BlockSpec((tm * TOKEN_SUBLANES, LANES), lambda i: (lo + i, 0)),
                  pl.BlockSpec((tm * TOKEN_SUBLANES, LANES), lambda i: (n // tm + lo + i, 0)),
                  pl.BlockSpec((tm, MOE_TOPK), lambda i: (lo + i, 0)), pl.BlockSpec((1, d), lambda i: (0, 0))],
        out_specs=pl.BlockSpec((tm, d), lambda i: (i, 0)),
        out_shape=jax.ShapeDtypeStruct((n_rows, d), F32),
        compiler_params=_params(("arbitrary",)),
        name="moe_combine_norm" if norm else "moe_combine",
    )(h, y, y, we.T, g.reshape(1, d))


S5_CH = LANES
S5_ST = (LANES // S5_GROUP) * S5_STATE
S5_SCAN_LANES = 512


def _s5_kernel(h_ref, g_ref, bre_ref, bim_ref, are_ref, aim_ref, cre_ref, cim_ref, d_ref, s0r_ref, s0i_ref,
               zg_ref, fr_ref, fi_ref, xr, xi, sr, si, u_tm, z_tm, *, s, steps):
    c = pl.program_id(1)

    @pl.when(c == 0)
    def _():
        sr[...] = s0r_ref[...]
        si[...] = s0i_ref[...]

    d = h_ref.shape[-1]
    n_ch = d // S5_CH
    u_nat = _rms_rows(h_ref[...].reshape(s * steps, d), g_ref[...])
    for n in range(s):
        for j in range(n_ch):
            u_tm[j, pl.ds(n, steps, stride=s), :] = u_nat[n * steps:(n + 1) * steps, j * S5_CH:(j + 1) * S5_CH]
    for j in range(n_ch):
        uj = u_tm[j].astype(BF16)
        xr[:, j * S5_ST:(j + 1) * S5_ST] = _mm(uj, bre_ref[j])
        xi[:, j * S5_ST:(j + 1) * S5_ST] = _mm(uj, bim_ref[j])

    for lo in range(0, xr.shape[1], S5_SCAN_LANES):
        ls = slice(lo, lo + S5_SCAN_LANES)
        ar = jnp.broadcast_to(are_ref[:, ls], (s, S5_SCAN_LANES))
        ai = jnp.broadcast_to(aim_ref[:, ls], (s, S5_SCAN_LANES))

        pr, pi = sr[:, ls], si[:, ls]
        for t in range(steps):
            rows = slice(t * s, (t + 1) * s)
            pr, pi = ar * pr - ai * pi + xr[rows, ls], ar * pi + ai * pr + xi[rows, ls]
            xr[rows, ls] = pr
            xi[rows, ls] = pi
        sr[:, ls] = pr
        si[:, ls] = pi

    for j in range(n_ch):
        cs = slice(j * S5_CH, (j + 1) * S5_CH)
        ss = slice(j * S5_ST, (j + 1) * S5_ST)
        y = _mm(xr[:, ss].astype(BF16), cre_ref[j]) - _mm(xi[:, ss].astype(BF16), cim_ref[j])
        z_tm[j] = jax.nn.gelu(y + d_ref[:, cs] * u_tm[j])
    for n in range(s):
        for j in range(n_ch):
            zg_ref[n, :, j * S5_CH:(j + 1) * S5_CH] = z_tm[j, pl.ds(n, steps, stride=s), :]

    @pl.when(c == pl.num_programs(1) - 1)
    def _():
        fr_ref[...] = sr[...]
        fi_ref[...] = si[...]


def _s5_weights(lam_re, lam_im, log_dt, b_re, b_im, c_re, c_im):
    dt = jnp.exp(log_dt)[:, None]
    mag = jnp.exp(lam_re * dt)
    ang = lam_im * dt
    ab_re = mag * jnp.cos(ang)
    ab_im = mag * jnp.sin(ang)
    den = lam_re * lam_re + lam_im * lam_im
    nr = ab_re - 1.0
    co_re = (nr * lam_re + ab_im * lam_im) / den
    co_im = (ab_im * lam_re - nr * lam_im) / den
    bb_re = co_re[..., None] * b_re - co_im[..., None] * b_im
    bb_im = co_re[..., None] * b_im + co_im[..., None] * b_re
    per = S5_CH // S5_GROUP
    n_ch = lam_re.shape[0] // per
    eye = jnp.eye(per, dtype=F32)

    def in_blocks(bb):
        w = bb.transpose(0, 2, 1).reshape(n_ch, per, S5_GROUP, S5_STATE)
        return jnp.einsum('jgcp,gh->jgchp', w, eye).reshape(n_ch, S5_CH, S5_ST).astype(BF16)

    def out_blocks(cc):
        w = cc.transpose(0, 2, 1).reshape(n_ch, per, S5_STATE, S5_GROUP)
        return jnp.einsum('jgpc,gh->jgphc', w, eye).reshape(n_ch, S5_ST, S5_CH).astype(BF16)

    return (in_blocks(bb_re), in_blocks(bb_im), ab_re.reshape(1, -1), ab_im.reshape(1, -1),
            out_blocks(c_re), out_blocks(c_im))


def _s5_scan(h, g, weights, d_skip, s0_re, s0_im, s, steps):
    n_seq, t_len, d = h.shape
    n_groups = n_seq // s
    rows = steps * s
    chunks = t_len // steps
    bre, bim, are, aim, cre, cim = weights
    n_state = are.shape[1]
    full = lambda a: pl.BlockSpec(a.shape, lambda gi, c: (0,) * a.ndim)
    row = pl.BlockSpec((s, steps, d), lambda gi, c: (gi, c, 0))
    state = pl.BlockSpec((s, n_state), lambda gi, c: (gi, 0))
    return pl.pallas_call(
        functools.partial(_s5_kernel, s=s, steps=steps),
        grid=(n_groups, chunks),
        in_specs=[row, pl.BlockSpec((1, d), lambda gi, c: (0, 0)), full(bre), full(bim), full(are), full(aim),
                  full(cre), full(cim), pl.BlockSpec((1, d), lambda gi, c: (0, 0)), state, state],
        out_specs=[row, state, state],
        out_shape=[jax.ShapeDtypeStruct(h.shape, F32), jax.ShapeDtypeStruct(s0_re.shape, F32),
                   jax.ShapeDtypeStruct(s0_re.shape, F32)],
        scratch_shapes=[pltpu.VMEM((rows, n_state), F32), pltpu.VMEM((rows, n_state), F32),
                        pltpu.VMEM((s, n_state), F32), pltpu.VMEM((s, n_state), F32),
                        pltpu.VMEM((d // S5_CH, rows, S5_CH), F32), pltpu.VMEM((d // S5_CH, rows, S5_CH), F32)],
        compiler_params=_params(("arbitrary", "arbitrary")),
        name="s5_scan",
    )(h, g.reshape(1, d), bre, bim, are, aim, cre, cim, d_skip.reshape(1, d), s0_re, s0_im)


S5_PROMPT_STEPS = 32
S5_SAMPLE_SEQS = 32


def kernel(x_prompt, x_sample, cache_k, cache_v, state_ret, state_s5_re, state_s5_im, page_table, norm1, norm2, norm_f, w_in_even, w_out_even, s5_lam_re, s5_lam_im, s5_log_dt, s5_b_re, s5_b_im, s5_c_re, s5_c_im, s5_d, s5_w_glu, s5_b_glu, moe_w_group, moe_b_group, moe_w_expert, moe_b_expert, moe_w13, moe_w2):
    n_p, seq, d = x_prompt.shape
    n_s, dec_seq, _ = x_sample.shape
    pool, page = cache_k.shape[1], cache_k.shape[2]
    past_len = page_table.shape[1] * page
    np_rows, ns_rows = n_p * seq, n_s * dec_seq
    routers = [_router_weights(moe_w_group[li], moe_b_group[li], moe_w_expert[li], moe_b_expert[li])
               for li in range(2)]

    def moe(li, xn, eid):
        return _moe_experts(xn, eid, moe_w13, moe_w2, li)

    xp, xs = x_prompt.reshape(np_rows, d), x_sample.reshape(ns_rows, d)
    w_in = w_in_even[0].astype(BF16)
    qa_p, ka_p, va_p, qr_p, kr_p, vr_p, gr_p, kt_p, vt_p = _in_proj(xp, norm1[0], w_in, jnp.arange(seq), page)
    qa_s, ka_s, va_s, qr_s, kr_s, vr_s, gr_s = _in_proj(xs, norm1[0], w_in,
                                                        past_len + jnp.arange(ROW_TILE) % dec_seq)
    oa_p = _moba_prompt(qa_p, ka_p, va_p, n_p, seq)
    or_p, ret_p = _ret_prompt(qr_p, kr_p, vr_p, gr_p, n_p, seq)
    oa_s = _moba_sample(qa_s, ka_s, va_s, cache_k[0].transpose(0, 2, 3, 1), cache_v[0].transpose(0, 2, 3, 1),
                        page_table, dec_seq)
    or_s, ret_s = _ret_sample(qr_s, kr_s, vr_s, gr_s, state_ret[0], dec_seq)
    w_out = w_out_even[0].astype(BF16)
    h, xn, eid, we = _proj_router([(oa_p, oa_s), (or_p, or_s)], [w_out[:A_W], w_out[A_W:]], None, (xp, xs),
                                  norm2[0], routers[0], False)
    y = moe(0, xn, eid)
    hp = _moe_combine(h, y, we, norm2[0], False, 0, np_rows)
    hs = _moe_combine(h, y, we, norm2[0], False, np_rows, ns_rows)

    s5w = _s5_weights(s5_lam_re[0], s5_lam_im[0], s5_log_dt[0], s5_b_re[0], s5_b_im[0], s5_c_re[0], s5_c_im[0])
    n_state = s5_lam_re.shape[1] * s5_lam_re.shape[2]
    zeros = jnp.zeros((n_p, n_state), F32)
    zg_p, s5r_p, s5i_p = _s5_scan(hp.reshape(n_p, seq, d), norm1[1], s5w, s5_d[0], zeros, zeros, n_p, S5_PROMPT_STEPS)
    zg_s, s5r_s, s5i_s = _s5_scan(hs.reshape(n_s, dec_seq, d), norm1[1], s5w, s5_d[0],
                                  state_s5_re[0].reshape(n_s, n_state), state_s5_im[0].reshape(n_s, n_state),
                                  S5_SAMPLE_SEQS, dec_seq)
    h, xn, eid, we = _proj_router([(zg_p.reshape(np_rows, d), zg_s.reshape(ns_rows, d))], [s5_w_glu[0].astype(BF16)],
                                  s5_b_glu[0], (hp, hs), norm2[1], routers[1], True)
    y = moe(1, xn, eid)
    y_prompt = _moe_combine(h, y, we, norm_f, True, 0, np_rows).reshape(n_p, seq, d)
    y_sample = _moe_combine(h, y, we, norm_f, True, np_rows, ns_rows).reshape(n_s, dec_seq, d)

    kv_p = lambda a: a.reshape(1, n_p, seq // page, H_A, HD_A, page).transpose(0, 1, 2, 5, 3, 4)
    kv_s = lambda a: a.reshape(1, n_s, dec_seq, H_A, HD_A)
    st = lambda a, n: a.reshape((1, n) + s5_lam_re.shape[1:])
    return (y_prompt, y_sample, kv_p(kt_p), kv_p(vt_p), kv_s(ka_s), kv_s(va_s), ret_p[None], ret_s[None],
            st(s5r_p, n_p), st(s5i_p, n_p), st(s5r_s, n_s), st(s5i_s, n_s))
```

```python
import functools

import jax
import jax.numpy as jnp
from jax import lax
from jax.experimental import pallas as pl
from jax.experimental.pallas import tpu as pltpu

F32 = jnp.float32
BF16 = jnp.bfloat16

H_A, HD_A = 8, 64
ROT_DIM = HD_A // 4
ROPE_THETA = 500000.0
MOBA_BLOCK = 256
MOBA_TOPK = 3
H_R, DK_R, DV_R = 8, 64, 128
RET_CHUNK = 128
S5_GROUP, S5_STATE = 16, 64
MOE_GROUPS, MOE_PER_GROUP, MOE_TOPK = 4, 8, 2
N_EXPERTS = MOE_GROUPS * MOE_PER_GROUP
A_W = H_A * HD_A
R_QK = H_R * DK_R
R_V = H_R * DV_R
NEG_INF = -1e30
EPS = 1e-6

LANES = 128
SUBLANES = 8
VMEM_LIMIT = 56 * 1024 * 1024

ROW_TILE = 256
PROJ_TILE = 512
MOE_ROWS = 256
TOKEN_SUBLANES = 8


def _nt(a, b):
    return lax.dot_general(a, b, (((1,), (1,)), ((), ())), preferred_element_type=F32)


def _tn(a, b):
    return lax.dot_general(a, b, (((0,), (0,)), ((), ())), preferred_element_type=F32)


def _mm(a, b):
    return jnp.dot(a, b, preferred_element_type=F32)


def _rms_rows(x, g):
    return x * lax.rsqrt(jnp.mean(x * x, axis=-1, keepdims=True) + EPS) * g


def _split_bf16(x):
    hi = x.astype(BF16)
    lo = (x - hi.astype(F32)).astype(BF16)
    return hi, lo


def _params(sem):
    return pltpu.CompilerParams(dimension_semantics=sem, vmem_limit_bytes=VMEM_LIMIT)


def _rotate_into(out_ref, z, c_ref, s_ref, shift, first, scale):
    for j in range(z.shape[1] // LANES):
        sl = slice(j * LANES, (j + 1) * LANES)
        zc = z[:, sl]
        up = pltpu.roll(zc, LANES - shift, axis=1)
        dn = pltpu.roll(zc, shift, axis=1)
        r = zc * c_ref[:, sl] + jnp.where(first, up, dn) * s_ref[:, sl]
        out_ref[:, sl] = r if scale is None else r * scale


def _pick_rows(n_ptiles, p_ref, s_ref):
    return jnp.where(pl.program_id(0) < n_ptiles, p_ref[...], s_ref[...])


def _split_rows(tm, width, n_ptiles):
    return (pl.BlockSpec((tm, width), lambda i: (jnp.minimum(i, n_ptiles - 1), 0)),
            pl.BlockSpec((tm, width), lambda i: (jnp.maximum(i - n_ptiles, 0), 0)))


def _store_pages_t(src_ref, dst_ref):
    n_pages, _, _, page = dst_ref.shape
    per = LANES // HD_A
    for j in range(A_W // LANES):
        t = src_ref[:, j * LANES:(j + 1) * LANES].T
        for p in range(n_pages):
            dst_ref[p, j * per:(j + 1) * per] = t[:, p * page:(p + 1) * page].reshape(per, HD_A, page)


def _in_proj_kernel(x_ref, g_ref, w_ref, ca_ref, sa_ref, cr_ref, sr_ref,
                    qa_ref, ka_ref, va_ref, qr_ref, kr_ref, vr_ref, gr_ref, *page_refs):
    xn = _rms_rows(x_ref[...], g_ref[...]).astype(BF16)
    lane = lax.broadcasted_iota(jnp.int32, (1, LANES), 1)
    first_a = (lane % HD_A) < (ROT_DIM // 2)
    first_r = (lane % 2) == 0

    def sec(lo, width):
        return _mm(xn, w_ref[:, lo:lo + width])

    _rotate_into(qa_ref, sec(0, A_W), ca_ref, sa_ref, ROT_DIM // 2, first_a, None)
    _rotate_into(ka_ref, sec(A_W, A_W), ca_ref, sa_ref, ROT_DIM // 2, first_a, None)
    va_ref[...] = sec(2 * A_W, A_W)
    _rotate_into(qr_ref, sec(3 * A_W, R_QK), cr_ref, sr_ref, 1, first_r, None)
    _rotate_into(kr_ref, sec(3 * A_W + R_QK, R_QK), cr_ref, sr_ref, 1, first_r, DK_R ** -0.5)
    vr_ref[...] = sec(3 * A_W + 2 * R_QK, R_V)
    gr_ref[...] = sec(3 * A_W + 2 * R_QK + R_V, R_V)
    if page_refs:
        _store_pages_t(ka_ref, page_refs[0])
        _store_pages_t(va_ref, page_refs[1])


def _rope_tables(pos):
    half = ROT_DIM // 2
    inv = ROPE_THETA ** (-jnp.arange(half, dtype=F32) / half)
    ang = pos.astype(F32)[:, None] * inv[None, :]
    cos, sin = jnp.cos(ang), jnp.sin(ang)
    rest = HD_A - ROT_DIM
    c = jnp.concatenate([cos, cos, jnp.ones((pos.shape[0], rest), F32)], axis=-1)
    s = jnp.concatenate([-sin, sin, jnp.zeros((pos.shape[0], rest), F32)], axis=-1)
    return jnp.tile(c, (1, H_A)), jnp.tile(s, (1, H_A))


def _retnet_tables(pos):
    n = DK_R // 2
    inv = 1.0 / (10000.0 ** jnp.linspace(0.0, 1.0, n, dtype=F32))
    ang = pos.astype(F32)[:, None] * inv[None, :]
    cos, sin = jnp.cos(ang), jnp.sin(ang)
    c = jnp.repeat(cos, 2, axis=-1)
    s = jnp.stack([-sin, sin], axis=-1).reshape(pos.shape[0], DK_R)
    return jnp.tile(c, (1, H_R)), jnp.tile(s, (1, H_R))


def _in_proj(x, g, w_bf16, pos, page=None):
    n, d = x.shape
    tm = ROW_TILE
    period_tiles = pos.shape[0] // tm
    ca, sa = _rope_tables(pos)
    cr, sr = _retnet_tables(pos)
    row = lambda w: pl.BlockSpec((tm, w), lambda i: (i, 0))
    tab = pl.BlockSpec((tm, A_W), lambda i: (i % period_tiles, 0))
    widths = (A_W, A_W, A_W, R_QK, R_QK, R_V, R_V)
    out_specs = [row(w) for w in widths]
    out_shape = [jax.ShapeDtypeStruct((n, w), F32) for w in widths]
    if page is not None:
        out_specs += [pl.BlockSpec((tm // page, H_A, HD_A, page), lambda i: (i, 0, 0, 0))] * 2
        out_shape += [jax.ShapeDtypeStruct((n // page, H_A, HD_A, page), F32)] * 2
    return pl.pallas_call(
        _in_proj_kernel,
        grid=(n // tm,),
        in_specs=[row(d), pl.BlockSpec((1, d), lambda i: (0, 0)),
                  pl.BlockSpec(w_bf16.shape, lambda i: (0, 0)), tab, tab, tab, tab],
        out_specs=out_specs,
        out_shape=out_shape,
        compiler_params=_params(("arbitrary",)),
        name="in_proj",
    )(x, g.reshape(1, d), w_bf16, ca, sa, cr, sr)


def _moba_select(q_f32, kmean, n_valid, eye):
    n_blk = kmean.shape[0]
    qh, ql = _split_bf16(q_f32)
    kh, kl = _split_bf16(kmean)
    st = _nt(jnp.concatenate([kh, kl, kh], axis=1), jnp.concatenate([qh, qh, ql], axis=1))
    jrow = lax.broadcasted_iota(jnp.int32, st.shape, 0)
    rank = jnp.zeros(st.shape, F32)
    for jp in range(n_blk):
        sj = st[jp:jp + 1, :]
        beats = (sj > st) | ((sj == st) & (jp < jrow))
        rank = rank + jnp.where(beats & (jp < n_valid), 1.0, 0.0)
    sel_t = jnp.where((jrow < n_valid) & (rank < MOBA_TOPK), 1.0, 0.0)
    sel_t = jnp.concatenate([sel_t, jnp.zeros((LANES - n_blk, st.shape[1]), F32)], axis=0).astype(BF16)
    return _nt(eye, sel_t)


def _moba_prompt_kernel(q_ref, k_ref, v_ref, o_ref, kb_ref, vb_ref, km_ref, bias_ref, s_ref):
    b = pl.program_id(2)
    blk = MOBA_BLOCK
    n_blk = k_ref.shape[0] // blk
    half = blk // 2

    lane = lax.broadcasted_iota(jnp.int32, (1, LANES), 1)
    r_i = lax.broadcasted_iota(jnp.int32, (blk, blk), 0)
    c_i = lax.broadcasted_iota(jnp.int32, (blk, blk), 1)
    heads = range(LANES // HD_A)
    hms = [(lane // HD_A) == hh for hh in heads]

    @pl.when(b == 0)
    def _():
        kb_ref[...] = k_ref[...].astype(BF16)
        vb_ref[...] = v_ref[...].astype(BF16)
        for j in range(n_blk):
            km_ref[j:j + 1, :] = jnp.mean(k_ref[j * blk:(j + 1) * blk, :], axis=0, keepdims=True)
        eye = jnp.where(r_i == c_i, 1.0, 0.0).astype(BF16)
        for hh in heads:
            bias_ref[hh, 0:blk, :] = jnp.full((blk, LANES), NEG_INF, BF16)
        for qb in range(1, n_blk):
            rows = slice(qb * blk, (qb + 1) * blk)
            for hh in heads:
                selq = _moba_select(jnp.where(hms[hh], q_ref[rows, :], 0.0), km_ref[...], qb, eye)
                bias_ref[hh, rows, :] = jnp.where(selq > 0.5, 0.0, NEG_INF).astype(BF16)

    own = pl.ds(pl.multiple_of(b * blk, blk), blk)
    q = q_ref[own, :]
    fold = lambda x: (x[:, :half], x[:, half:])
    q_aug, s_own, mx = [], [], []
    for hh in heads:
        qs = (jnp.where(hms[hh], q, 0.0) * (HD_A ** -0.5)).astype(BF16)
        so = jnp.where(c_i <= r_i, _nt(qs, kb_ref[own, :]), NEG_INF)
        q_aug.append(jnp.concatenate([qs, bias_ref[hh, own, :]], axis=1))
        s_own.append(so)
        mx.append(jnp.maximum(*fold(so)))

    def rows_of(j):
        return pl.ds(pl.multiple_of(j * blk, blk), blk)

    def scores(j, mx):
        onehot = jnp.broadcast_to(jnp.where(lane == j, 1.0, 0.0).astype(BF16), (blk, LANES))
        k_aug = jnp.concatenate([kb_ref[rows_of(j), :], onehot], axis=1)
        out = []
        for hh in heads:
            s = _nt(q_aug[hh], k_aug)
            s_ref[hh, j] = s
            out.append(jnp.maximum(mx[hh], jnp.maximum(*fold(s))))
        return tuple(out)

    mx = lax.fori_loop(0, b, scores, tuple(mx))
    ms = [jnp.max(mx[hh], axis=1, keepdims=True) for hh in heads]

    def weights(s, hh, vj):
        p = jnp.exp(s - ms[hh])
        lo, hi = fold(p)
        return lo + hi, _mm(p.astype(BF16), vj)

    def accumulate(j, carry):
        vj = vb_ref[rows_of(j), :]
        out = []
        for hh in heads:
            l_part, pv = weights(s_ref[hh, j], hh, vj)
            out.append((carry[hh][0] + l_part, carry[hh][1] + pv))
        return tuple(out)

    carry = lax.fori_loop(0, b, accumulate, tuple(weights(s_own[hh], hh, vb_ref[own, :]) for hh in heads))
    out = jnp.zeros(q.shape, F32)
    for hh in heads:
        l_part, acc = carry[hh]
        out = out + jnp.where(hms[hh], acc / jnp.sum(l_part, axis=1, keepdims=True), 0.0)
    o_ref[...] = out


def _moba_prompt(q_a, k_a, v_a, n_seq, seq):
    blk = MOBA_BLOCK
    n_blk = seq // blk
    ospec = pl.BlockSpec((blk, LANES), lambda s, h, b: (s * n_blk + b, h))
    kspec = pl.BlockSpec((seq, LANES), lambda s, h, b: (s, h))
    per = LANES // HD_A
    return pl.pallas_call(
        _moba_prompt_kernel,
        grid=(n_seq, A_W // LANES, n_blk),
        in_specs=[kspec, kspec, kspec],
        out_specs=ospec,
        out_shape=jax.ShapeDtypeStruct((n_seq * seq, A_W), F32),
        scratch_shapes=[pltpu.VMEM((seq, LANES), BF16), pltpu.VMEM((seq, LANES), BF16),
                        pltpu.VMEM((n_blk, LANES), F32), pltpu.VMEM((per, seq, LANES), BF16),
                        pltpu.VMEM((per, n_blk, blk, blk), F32)],
        compiler_params=_params(("arbitrary", "arbitrary", "arbitrary")),
        name="moba_prompt",
    )(q_a, k_a, v_a)


def _ret_decay_tables(c):
    log_g = jnp.log(1.0 - 2.0 ** (-5.0 - jnp.arange(H_R, dtype=F32)))
    i = jnp.arange(c, dtype=F32)
    diff = i[:, None] - i[None, :]
    dmat = jnp.where(diff >= 0, jnp.exp(jnp.maximum(diff, 0.0)[None] * log_g[:, None, None]), 0.0)
    dq = jnp.exp((i + 1.0)[None, :] * log_g[:, None])
    dk = jnp.exp((c - 1.0 - i)[None, :] * log_g[:, None])
    dc = jnp.exp(c * log_g)
    return dmat, dq, dk, dc


def _ret_chunk(q, k, v, g, s, dmat, dq, dk, dc):
    att = _nt(q.astype(BF16), k.astype(BF16)) * dmat
    o = _mm(att.astype(BF16), v.astype(BF16)) + _mm((q * dq).astype(BF16), s.astype(BF16))
    s = s * dc + _tn((k * dk).astype(BF16), v.astype(BF16))
    o = o * lax.rsqrt(jnp.mean(o * o, axis=-1, keepdims=True) + EPS)
    return o * (g * jax.nn.sigmoid(g)), s


def _ret_prompt_kernel(q_ref, k_ref, v_ref, g_ref, dmat_ref, dq_ref, dk_ref, dc_ref, o_ref, st_ref):
    c = RET_CHUNK
    per = LANES // DK_R
    lane = lax.broadcasted_iota(jnp.int32, (1, LANES), 1)

    def chunk(i, states):
        rows = pl.ds(pl.multiple_of(i * c, c), c)
        q_all, k_all = q_ref[rows, :], k_ref[rows, :]
        out = []
        for hh in range(per):
            hm = (lane // DK_R) == hh
            vl = slice(hh * DV_R, (hh + 1) * DV_R)
            o, s = _ret_chunk(jnp.where(hm, q_all, 0.0), jnp.where(hm, k_all, 0.0), v_ref[rows, vl], g_ref[rows, vl],
                              states[hh], dmat_ref[hh], dq_ref[hh], dk_ref[hh], dc_ref[hh])
            o_ref[rows, vl] = o
            out.append(s)
        return tuple(out)

    states = lax.fori_loop(0, q_ref.shape[0] // c, chunk, tuple(jnp.zeros((LANES, DV_R), F32) for _ in range(per)),
                           unroll=4)
    for hh in range(per):
        st_ref[hh] = states[hh][hh * DK_R:(hh + 1) * DK_R, :]


def _ret_tables_bcast(c, rows):
    dmat, dq, dk, dc = _ret_decay_tables(c)
    pad = rows - c
    dmat = jnp.pad(dmat, ((0, 0), (0, pad), (0, pad)))
    dq = jnp.broadcast_to(jnp.pad(dq, ((0, 0), (0, pad)))[:, :, None], (H_R, rows, LANES))
    dk = jnp.broadcast_to(jnp.pad(dk, ((0, 0), (0, pad)))[:, :, None], (H_R, rows, LANES))
    dc = jnp.broadcast_to(dc[:, None, None], (H_R, 1, LANES))
    return dmat, dq, dk, dc


def _ret_prompt(q_r, k_r, v_r, g_r, n_seq, seq):
    c = RET_CHUNK
    dmat, dq, dk, dc = _ret_tables_bcast(c, c)
    per = LANES // DK_R
    qk = pl.BlockSpec((seq, LANES), lambda s, h: (s, h))
    vg = pl.BlockSpec((seq, per * DV_R), lambda s, h: (s, h))
    tab = lambda r: pl.BlockSpec((per, r, LANES), lambda s, h: (h, 0, 0))
    return pl.pallas_call(
        _ret_prompt_kernel,
        grid=(n_seq, H_R // per),
        in_specs=[qk, qk, vg, vg, tab(c), tab(c), tab(c), tab(1)],
        out_specs=[vg, pl.BlockSpec((None, per, DK_R, DV_R), lambda s, h: (s, h, 0, 0))],
        out_shape=[jax.ShapeDtypeStruct((n_seq * seq, R_V), F32),
                   jax.ShapeDtypeStruct((n_seq, H_R, DK_R, DV_R), F32)],
        compiler_params=_params(("arbitrary", "arbitrary")),
        name="ret_prompt",
    )(q_r, k_r, v_r, g_r, dmat, dq, dk, dc)


def _pad_rows(x, rows):
    return jnp.concatenate([x, jnp.zeros((rows - x.shape[0], x.shape[1]), x.dtype)], axis=0)


def _moba_sample_kernel(pt_ref, q_ref, kn_ref, vn_ref, *rest, n_pages):
    k_refs, v_refs = rest[:n_pages], rest[n_pages:2 * n_pages]
    o_ref, s_ref = rest[2 * n_pages], rest[2 * n_pages + 1]
    q = q_ref[...]
    t = q.shape[0]
    page = k_refs[0].shape[-1]
    per_blk = MOBA_BLOCK // page
    n_blk = n_pages // per_blk
    lane = lax.broadcasted_iota(jnp.int32, (1, A_W), 1)
    qbd = jnp.concatenate([jnp.where((lane // HD_A) == h, q, 0.0) for h in range(H_A)], axis=0)
    qs = (qbd * (HD_A ** -0.5)).astype(BF16)
    rows = qbd.shape[0]

    bsum = [jnp.zeros((rows, 1), F32) for _ in range(n_blk)]
    for p in range(n_pages):
        sp = _mm(qs, k_refs[p][...].reshape(A_W, page).astype(BF16))
        s_ref[:, p * page:(p + 1) * page] = sp
        bsum[p // per_blk] = bsum[p // per_blk] + jnp.sum(sp, axis=1, keepdims=True)
    sel = []
    for j in range(n_blk):
        rank = jnp.zeros((rows, 1), F32)
        for jp in range(n_blk):
            if jp != j:
                beats = (bsum[jp] > bsum[j]) | (bsum[jp] == bsum[j]) if jp < j else bsum[jp] > bsum[j]
                rank = rank + jnp.where(beats, 1.0, 0.0)
        sel.append(rank < MOBA_TOPK)

    qi = lax.broadcasted_iota(jnp.int32, (rows, LANES), 0) % t
    causal = lax.broadcasted_iota(jnp.int32, (rows, LANES), 1) <= qi
    s0 = jnp.where(causal, _nt(qs, _pad_rows(kn_ref[...], LANES).astype(BF16)), NEG_INF)
    m = jnp.max(s0, axis=1, keepdims=True)
    p0 = jnp.exp(s0 - m)
    carry = (m, jnp.sum(p0, axis=1, keepdims=True), _mm(p0.astype(BF16), _pad_rows(vn_ref[...], LANES).astype(BF16)))
    for p in range(n_pages):
        m, l, acc = carry
        s = jnp.where(sel[p // per_blk], s_ref[:, p * page:(p + 1) * page], NEG_INF)
        m_new = jnp.maximum(m, jnp.max(s, axis=1, keepdims=True))
        alpha = jnp.exp(m - m_new)
        pr = jnp.exp(s - m_new)
        carry = (m_new, alpha * l + jnp.sum(pr, axis=1, keepdims=True),
                 alpha * acc + _nt(pr.astype(BF16), v_refs[p][...].reshape(A_W, page).astype(BF16)))
    _, l, acc = carry
    o = acc / l
    out = jnp.zeros((t, A_W), F32)
    for h in range(H_A):
        out = out + jnp.where((lane // HD_A) == h, o[h * t:(h + 1) * t, :], 0.0)
    o_ref[...] = out


def _moba_sample(q_a, k_a, v_a, cache_kt, cache_vt, page_table, dec_seq):
    n_s, n_pages = page_table.shape
    new = pl.BlockSpec((dec_seq, A_W), lambda n, pt: (n, 0))
    pages = [pl.BlockSpec((None,) + cache_kt.shape[1:], lambda n, pt, p=p: (pt[n * n_pages + p], 0, 0, 0))
             for p in range(n_pages)]
    grid_spec = pltpu.PrefetchScalarGridSpec(
        num_scalar_prefetch=1,
        grid=(n_s,),
        in_specs=[new, new, new] + pages + pages,
        out_specs=pl.BlockSpec((dec_seq, A_W), lambda n, pt: (n, 0)),
        scratch_shapes=[pltpu.VMEM((H_A * dec_seq, n_pages * cache_kt.shape[-1]), F32)],
    )
    return pl.pallas_call(
        functools.partial(_moba_sample_kernel, n_pages=n_pages),
        grid_spec=grid_spec,
        out_shape=jax.ShapeDtypeStruct((n_s * dec_seq, A_W), F32),
        compiler_params=_params(("arbitrary",)),
        name="moba_sample",
    )(page_table.reshape(-1), q_a, k_a, v_a, *([cache_kt] * n_pages), *([cache_vt] * n_pages))


RET_SAMPLE_SEQS = 4


def _ret_sample_kernel(q_ref, k_ref, v_ref, g_ref, s0_ref, dmat_ref, dq_ref, dk_ref, dc_ref, o_ref, st_ref):
    n_seq = s0_ref.shape[0]
    t = q_ref.shape[0] // n_seq
    lane = lax.broadcasted_iota(jnp.int32, (1, LANES), 1)
    per = LANES // DK_R
    zero_half = jnp.zeros((DK_R, DV_R), F32)
    for i in range(n_seq):
        rows = slice(i * t, (i + 1) * t)
        for h in range(H_R):
            hh = h % per
            hm = (lane // DK_R) == hh
            qk_l = slice((h // per) * LANES, (h // per + 1) * LANES)
            v_l = slice(h * DV_R, (h + 1) * DV_R)
            q = _pad_rows(jnp.where(hm, q_ref[rows, qk_l], 0.0), LANES)
            k = _pad_rows(jnp.where(hm, k_ref[rows, qk_l], 0.0), LANES)
            v = _pad_rows(v_ref[rows, v_l], LANES)
            g = _pad_rows(g_ref[rows, v_l], LANES)
            halves = [zero_half] * per
            halves[hh] = s0_ref[i, h]
            o, s = _ret_chunk(q, k, v, g, jnp.concatenate(halves, axis=0),
                              dmat_ref[h], dq_ref[h], dk_ref[h], dc_ref[h])
            o_ref[rows, v_l] = o[:t, :]
            st_ref[i, h] = s[hh * DK_R:(hh + 1) * DK_R, :]


def _ret_sample(q_r, k_r, v_r, g_r, s0, dec_seq):
    n_s = s0.shape[0]
    ns = RET_SAMPLE_SEQS
    dmat, dq, dk, dc = _ret_tables_bcast(dec_seq, LANES)
    qk = pl.BlockSpec((ns * dec_seq, R_QK), lambda n: (n, 0))
    vg = pl.BlockSpec((ns * dec_seq, R_V), lambda n: (n, 0))
    st = pl.BlockSpec((ns, H_R, DK_R, DV_R), lambda n: (n, 0, 0, 0))
    full = lambda a: pl.BlockSpec(a.shape, lambda n: (0,) * a.ndim)
    return pl.pallas_call(
        _ret_sample_kernel,
        grid=(n_s // ns,),
        in_specs=[qk, qk, vg, vg, st, full(dmat), full(dq), full(dk), full(dc)],
        out_specs=[vg, st],
        out_shape=[jax.ShapeDtypeStruct((n_s * dec_seq, R_V), F32), jax.ShapeDtypeStruct(s0.shape, F32)],
        compiler_params=_params(("arbitrary",)),
        name="ret_sample",
    )(q_r, k_r, v_r, g_r, s0, dmat, dq, dk, dc)


ROUTER_ROWS = SUBLANES + N_EXPERTS


def _route(xn, wr_ref, br_ref, eid_ref, we_ref):
    xh, xl = _split_bf16(xn)
    lt = _nt(wr_ref[...], jnp.concatenate([xh, xh, xl], axis=1)) + br_ref[...]
    tm = lt.shape[1]
    r8 = lax.broadcasted_iota(jnp.int32, (SUBLANES, tm), 0)
    lg = jnp.where(r8 < MOE_GROUPS, lt[:SUBLANES, :], NEG_INF)
    mg = jnp.max(lg, axis=0, keepdims=True)
    wg = 1.0 / jnp.sum(jnp.exp(lg - mg), axis=0, keepdims=True)
    gidx = jnp.min(jnp.where(lg == mg, r8, SUBLANES), axis=0, keepdims=True)
    le = jnp.zeros((MOE_PER_GROUP, tm), F32)
    for gi in range(MOE_GROUPS):
        lo = SUBLANES + gi * MOE_PER_GROUP
        le = le + jnp.where(gidx == gi, lt[lo:lo + MOE_PER_GROUP, :], 0.0)
    v1 = jnp.max(le, axis=0, keepdims=True)
    i1 = jnp.min(jnp.where(le == v1, r8, MOE_PER_GROUP), axis=0, keepdims=True)
    le2 = jnp.where(r8 == i1, -jnp.inf, le)
    v2 = jnp.max(le2, axis=0, keepdims=True)
    i2 = jnp.min(jnp.where(le2 == v2, r8, MOE_PER_GROUP), axis=0, keepdims=True)
    e21 = jnp.exp(v2 - v1)
    w1 = wg / (1.0 + e21)
    eid_ref[0:1, :] = gidx * MOE_PER_GROUP + i1
    eid_ref[1:2, :] = gidx * MOE_PER_GROUP + i2
    we_ref[0:1, :] = w1
    we_ref[1:2, :] = w1 * e21


def _proj_router_kernel(*refs, n_in, glu, n_ptiles):
    a_refs, w_refs = refs[:2 * n_in], refs[2 * n_in:3 * n_in]
    k = 3 * n_in
    b_ref = refs[k] if glu else None
    k += int(glu)
    hp_ref, hs_ref, g_ref, wr_ref, br_ref, ho_ref, xn_ref, eid_ref, we_ref = refs[k:k + 9]
    acc = None
    for j, w_ref in enumerate(w_refs):
        a = _pick_rows(n_ptiles, a_refs[2 * j], a_refs[2 * j + 1])
        part = _mm(a.astype(BF16), w_ref[...])
        acc = part if acc is None else acc + part
    if glu:
        acc = acc + b_ref[...]
        half = acc.shape[1] // 2
        acc = acc[:, :half] * jax.nn.sigmoid(acc[:, half:])
    hn = _pick_rows(n_ptiles, hp_ref, hs_ref) + acc
    ho_ref[...] = hn
    xn = _rms_rows(hn, g_ref[...])
    for j in range(TOKEN_SUBLANES):
        xn_ref[pl.ds(j, xn.shape[0], stride=TOKEN_SUBLANES), :] = xn[:, j * LANES:(j + 1) * LANES]
    _route(xn, wr_ref, br_ref, eid_ref, we_ref)


def _router_weights(w_group, b_group, w_expert, b_expert):
    d = w_group.shape[0]
    wt = jnp.concatenate([w_group.T, jnp.zeros((SUBLANES - MOE_GROUPS, d), F32), w_expert.T], axis=0)
    hi = wt.astype(BF16)
    lo = (wt - hi.astype(F32)).astype(BF16)
    bias = jnp.concatenate([b_group, jnp.zeros((SUBLANES - MOE_GROUPS,), F32), b_expert]).reshape(-1, 1)
    return jnp.concatenate([hi, lo, hi], axis=1), bias


def _proj_router(acts, weights, bias, h, g, router, glu):
    d = h[0].shape[1]
    n = h[0].shape[0] + h[1].shape[0]
    tm = PROJ_TILE
    n_ptiles = h[0].shape[0] // tm
    wr, br = router
    row = lambda w: pl.BlockSpec((tm, w), lambda i: (i, 0))
    full = lambda a: pl.BlockSpec(a.shape, lambda i: (0,) * a.ndim)
    tok = pl.BlockSpec((MOE_TOPK, tm), lambda i: (0, i))
    operands = ([a for pair in acts for a in pair] + list(weights) + ([bias.reshape(1, -1)] if glu else [])
                + [h[0], h[1], g.reshape(1, d), wr, br])
    in_specs = ([spec for pair in acts for spec in _split_rows(tm, pair[0].shape[1], n_ptiles)]
                + [full(w) for w in weights]
                + ([pl.BlockSpec((1, bias.shape[0]), lambda i: (0, 0))] if glu else [])
                + [*_split_rows(tm, d, n_ptiles), pl.BlockSpec((1, d), lambda i: (0, 0)), full(wr), full(br)])
    return pl.pallas_call(
        functools.partial(_proj_router_kernel, n_in=len(acts), glu=glu, n_ptiles=n_ptiles),
        grid=(n // tm,),
        in_specs=in_specs,
        out_specs=[row(d), pl.BlockSpec((tm * TOKEN_SUBLANES, LANES), lambda i: (i, 0)), tok, tok],
        out_shape=[jax.ShapeDtypeStruct((n, d), F32), jax.ShapeDtypeStruct((n * TOKEN_SUBLANES, LANES), F32),
                   jax.ShapeDtypeStruct((MOE_TOPK, n), jnp.int32), jax.ShapeDtypeStruct((MOE_TOPK, n), F32)],
        compiler_params=_params(("arbitrary",)),
        name="glu_router" if glu else "out_proj_router",
    )(*operands)


def _moe_plan(eid):
    n_tok = eid.shape[1]
    n_assign = MOE_TOPK * n_tok
    blk = MOE_ROWS
    n_blocks = -(-n_assign // blk) + N_EXPERTS
    e_flat = eid.reshape(-1)
    experts = jnp.arange(N_EXPERTS, dtype=jnp.int32)
    counts = jnp.sum((e_flat[:, None] == experts[None, :]).astype(jnp.int32), axis=0)
    order = jnp.argsort(e_flat).astype(jnp.int32)
    pc = (counts + blk - 1) // blk * blk
    pend = jnp.cumsum(pc)
    pstart = pend - pc
    start = jnp.cumsum(counts) - counts
    n_steps = n_blocks + 2
    first_row = jnp.arange(n_steps, dtype=jnp.int32) * blk
    blk_e = jnp.minimum(jnp.sum((pend[None, :] <= first_row[:, None]).astype(jnp.int32), axis=1), N_EXPERTS - 1)
    lane = jnp.arange(blk, dtype=jnp.int32)[None, :]
    off = first_row[:, None] + lane - pstart[blk_e][:, None]
    valid = off < counts[blk_e][:, None]
    a_row = order[jnp.clip(start[blk_e][:, None] + off, 0, n_assign - 1)]
    a_row = jnp.where(valid, a_row, 0)
    src = a_row % n_tok
    dst = jnp.concatenate([n_assign + lane, jnp.where(valid, a_row, n_assign + lane)], axis=0)
    n_used = (pend[-1] // blk).astype(jnp.int32).reshape(1)
    return blk_e, n_used, src.reshape(-1), dst.reshape(-1), n_steps


def _moe_kernel(blk_e_ref, n_used_ref, src_ref, dst_ref, x_hbm, w13_ref, w2_ref, y_hbm,
                xbuf, ybuf, w13b, w2b, gsem, ssem):
    b = pl.program_id(0)
    n_used = n_used_ref[0]
    blk = MOE_ROWS
    slot = b % 2

    ts = TOKEN_SUBLANES

    def gather_row(block, slot, r):
        tok_row = pl.multiple_of(src_ref[block * blk + r] * ts, ts)
        return pltpu.make_async_copy(x_hbm.at[pl.ds(tok_row, ts), :],
                                     xbuf.at[slot, pl.ds(r * ts, ts), :], gsem.at[slot])

    def scatter_row(block, slot, r):
        out_row = pl.multiple_of(dst_ref[block * blk + r] * ts, ts)
        return pltpu.make_async_copy(ybuf.at[slot, pl.ds(r * ts, ts), :],
                                     y_hbm.at[pl.ds(out_row, ts), :], ssem.at[slot])

    def gather_all(slot):
        return pltpu.make_async_copy(x_hbm.at[pl.ds(0, blk * ts), :], xbuf.at[slot], gsem.at[slot])

    def scatter_all(slot):
        return pltpu.make_async_copy(ybuf.at[slot], y_hbm.at[pl.ds(0, blk * ts), :], ssem.at[slot])

    def issue_row_copies():
        for r in range(blk):
            gather_row(b + 1, 1 - slot, r).start(priority=r % 2)
        for r in range(blk):
            scatter_row(b, 1 - slot, r).start(priority=r % 2)

    @pl.when(b == 0)
    def _():
        xbuf[...] = jnp.zeros(xbuf.shape, xbuf.dtype)
        ybuf[...] = jnp.zeros(ybuf.shape, ybuf.dtype)
        fill = pltpu.make_async_copy(ybuf.at[0], y_hbm.at[pl.ds(y_hbm.shape[0] - blk * ts, blk * ts), :], ssem.at[0])
        fill.start()
        fill.wait()
        for r in range(blk):
            gather_row(0, 0, r).start(priority=r % 2)

    @pl.when(b <= n_used)
    def _():
        gather_all(slot).wait()

    @pl.when((b >= 1) & (b <= n_used + 1))
    def _():
        scatter_all(slot).wait()

    @pl.when(b < n_used)
    def _():
        @pl.when((b == 0) | (blk_e_ref[b] != blk_e_ref[jnp.maximum(b - 1, 0)]))
        def _():
            w13b[...] = w13_ref[...].astype(BF16)
            w2b[...] = w2_ref[...].astype(BF16)

        issue_row_copies()
        x = jnp.concatenate([xbuf[slot, pl.ds(j, blk, stride=ts), :] for j in range(ts)], axis=1)
        hb = _mm(x.astype(BF16), w13b[...])
        half = hb.shape[1] // 2
        gate = hb[:, :half]
        act = gate * jax.nn.sigmoid(gate) * hb[:, half:]
        y = _mm(act.astype(BF16), w2b[...])
        for j in range(ts):
            ybuf[slot, pl.ds(j, blk, stride=ts), :] = y[:, j * LANES:(j + 1) * LANES]

    @pl.when(b == n_used)
    def _():
        for r in range(blk):
            scatter_row(b, 1 - slot, r).start(priority=r % 2)


def _moe_experts(xn, eid, w13, w2, layer):
    n_tok, d = xn.shape[0] // TOKEN_SUBLANES, w13.shape[2]
    blk = MOE_ROWS
    blk_e, n_used, src_tok, dst_row, n_steps = _moe_plan(eid)
    grid_spec = pltpu.PrefetchScalarGridSpec(
        num_scalar_prefetch=4,
        grid=(n_steps,),
        in_specs=[pl.BlockSpec(memory_space=pl.ANY),
                  pl.BlockSpec((None, None) + w13.shape[2:], lambda b, be, *_: (layer, be[b], 0, 0)),
                  pl.BlockSpec((None, None) + w2.shape[2:], lambda b, be, *_: (layer, be[b], 0, 0))],
        out_specs=pl.BlockSpec(memory_space=pl.ANY),
        scratch_shapes=[pltpu.VMEM((2, blk * TOKEN_SUBLANES, LANES), F32),
                        pltpu.VMEM((2, blk * TOKEN_SUBLANES, LANES), F32),
                        pltpu.VMEM(w13.shape[2:], BF16), pltpu.VMEM(w2.shape[2:], BF16),
                        pltpu.SemaphoreType.DMA((2,)), pltpu.SemaphoreType.DMA((2,))],
    )
    return pl.pallas_call(
        _moe_kernel,
        grid_spec=grid_spec,
        out_shape=jax.ShapeDtypeStruct(((MOE_TOPK * n_tok + blk) * TOKEN_SUBLANES, LANES), F32),
        compiler_params=_params(("arbitrary",)),
        name="moe_experts",
    )(blk_e, n_used, src_tok, dst_row, xn, w13, w2)


def _combine_kernel(h_ref, y0_ref, y1_ref, w_ref, g_ref, o_ref, *, norm):
    w = w_ref[...]
    rows, ts = h_ref.shape[0], TOKEN_SUBLANES

    def token_rows(y_ref):
        return jnp.concatenate([y_ref[pl.ds(j, rows, stride=ts), :] for j in range(ts)], axis=1)

    h = h_ref[...] + (token_rows(y0_ref) * w[:, 0:1] + token_rows(y1_ref) * w[:, 1:2])
    o_ref[...] = _rms_rows(h, g_ref[...]) if norm else h


def _moe_combine(h, y, we, g, norm, row_lo, n_rows):
    n, d = h.shape
    tm = ROW_TILE
    lo = row_lo // tm
    return pl.pallas_call(
        functools.partial(_combine_kernel, norm=norm),
        grid=(n_rows // tm,),
        in_specs=[pl.BlockSpec((tm, d), lambda i: (lo + i, 0)),
                  pl.BlockSpec((tm * TOKEN_SUBLANES, LANES), lambda i: (lo + i, 0)),
                  pl.BlockSpec((tm * TOKEN_SUBLANES, LANES), lambda i: (n // tm + lo + i, 0)),
                  pl.BlockSpec((tm, MOE_TOPK), lambda i: (lo + i, 0)), pl.BlockSpec((1, d), lambda i: (0, 0))],
        out_specs=pl.BlockSpec((tm, d), lambda i: (i, 0)),
        out_shape=jax.ShapeDtypeStruct((n_rows, d), F32),
        compiler_params=_params(("arbitrary",)),
        name="moe_combine_norm" if norm else "moe_combine",
    )(h, y, y, we.T, g.reshape(1, d))


S5_CH = LANES
S5_ST = (LANES // S5_GROUP) * S5_STATE
S5_SCAN_LANES = 512


def _s5_kernel(h_ref, g_ref, bre_ref, bim_ref, are_ref, aim_ref, cre_ref, cim_ref, d_ref, s0r_ref, s0i_ref,
               zg_ref, fr_ref, fi_ref, xr, xi, sr, si, u_tm, z_tm, *, s, steps):
    c = pl.program_id(1)

    @pl.when(c == 0)
    def _():
        sr[...] = s0r_ref[...]
        si[...] = s0i_ref[...]

    d = h_ref.shape[-1]
    n_ch = d // S5_CH
    u_nat = _rms_rows(h_ref[...].reshape(s * steps, d), g_ref[...])
    for n in range(s):
        for j in range(n_ch):
            u_tm[j, pl.ds(n, steps, stride=s), :] = u_nat[n * steps:(n + 1) * steps, j * S5_CH:(j + 1) * S5_CH]
    for j in range(n_ch):
        uj = u_tm[j].astype(BF16)
        xr[:, j * S5_ST:(j + 1) * S5_ST] = _mm(uj, bre_ref[j])
        xi[:, j * S5_ST:(j + 1) * S5_ST] = _mm(uj, bim_ref[j])

    for lo in range(0, xr.shape[1], S5_SCAN_LANES):
        ls = slice(lo, lo + S5_SCAN_LANES)
        ar = jnp.broadcast_to(are_ref[:, ls], (s, S5_SCAN_LANES))
        ai = jnp.broadcast_to(aim_ref[:, ls], (s, S5_SCAN_LANES))

        pr, pi = sr[:, ls], si[:, ls]
        for t in range(steps):
            rows = slice(t * s, (t + 1) * s)
            pr, pi = ar * pr - ai * pi + xr[rows, ls], ar * pi + ai * pr + xi[rows, ls]
            xr[rows, ls] = pr
            xi[rows, ls] = pi
        sr[:, ls] = pr
        si[:, ls] = pi

    for j in range(n_ch):
        cs = slice(j * S5_CH, (j + 1) * S5_CH)
        ss = slice(j * S5_ST, (j + 1) * S5_ST)
        y = _mm(xr[:, ss].astype(BF16), cre_ref[j]) - _mm(xi[:, ss].astype(BF16), cim_ref[j])
        z_tm[j] = jax.nn.gelu(y + d_ref[:, cs] * u_tm[j])
    for n in range(s):
        for j in range(n_ch):
            zg_ref[n, :, j * S5_CH:(j + 1) * S5_CH] = z_tm[j, pl.ds(n, steps, stride=s), :]

    @pl.when(c == pl.num_programs(1) - 1)
    def _():
        fr_ref[...] = sr[...]
        fi_ref[...] = si[...]


def _s5_weights(lam_re, lam_im, log_dt, b_re, b_im, c_re, c_im):
    dt = jnp.exp(log_dt)[:, None]
    mag = jnp.exp(lam_re * dt)
    ang = lam_im * dt
    ab_re = mag * jnp.cos(ang)
    ab_im = mag * jnp.sin(ang)
    den = lam_re * lam_re + lam_im * lam_im
    nr = ab_re - 1.0
    co_re = (nr * lam_re + ab_im * lam_im) / den
    co_im = (ab_im * lam_re - nr * lam_im) / den
    bb_re = co_re[..., None] * b_re - co_im[..., None] * b_im
    bb_im = co_re[..., None] * b_im + co_im[..., None] * b_re
    per = S5_CH // S5_GROUP
    n_ch = lam_re.shape[0] // per
    eye = jnp.eye(per, dtype=F32)

    def in_blocks(bb):
        w = bb.transpose(0, 2, 1).reshape(n_ch, per, S5_GROUP, S5_STATE)
        return jnp.einsum('jgcp,gh->jgchp', w, eye).reshape(n_ch, S5_CH, S5_ST).astype(BF16)

    def out_blocks(cc):
        w = cc.transpose(0, 2, 1).reshape(n_ch, per, S5_STATE, S5_GROUP)
        return jnp.einsum('jgpc,gh->jgphc', w, eye).reshape(n_ch, S5_ST, S5_CH).astype(BF16)

    return (in_blocks(bb_re), in_blocks(bb_im), ab_re.reshape(1, -1), ab_im.reshape(1, -1),
            out_blocks(c_re), out_blocks(c_im))


def _s5_scan(h, g, weights, d_skip, s0_re, s0_im, s, steps):
    n_seq, t_len, d = h.shape
    n_groups = n_seq // s
    rows = steps * s
    chunks = t_len // steps
    bre, bim, are, aim, cre, cim = weights
    n_state = are.shape[1]
    full = lambda a: pl.BlockSpec(a.shape, lambda gi, c: (0,) * a.ndim)
    row = pl.BlockSpec((s, steps, d), lambda gi, c: (gi, c, 0))
    state = pl.BlockSpec((s, n_state), lambda gi, c: (gi, 0))
    return pl.pallas_call(
        functools.partial(_s5_kernel, s=s, steps=steps),
        grid=(n_groups, chunks),
        in_specs=[row, pl.BlockSpec((1, d), lambda gi, c: (0, 0)), full(bre), full(bim), full(are), full(aim),
                  full(cre), full(cim), pl.BlockSpec((1, d), lambda gi, c: (0, 0)), state, state],
        out_specs=[row, state, state],
        out_shape=[jax.ShapeDtypeStruct(h.shape, F32), jax.ShapeDtypeStruct(s0_re.shape, F32),
                   jax.ShapeDtypeStruct(s0_re.shape, F32)],
        scratch_shapes=[pltpu.VMEM((rows, n_state), F32), pltpu.VMEM((rows, n_state), F32),
                        pltpu.VMEM((s, n_state), F32), pltpu.VMEM((s, n_state), F32),
                        pltpu.VMEM((d // S5_CH, rows, S5_CH), F32), pltpu.VMEM((d // S5_CH, rows, S5_CH), F32)],
        compiler_params=_params(("arbitrary", "arbitrary")),
        name="s5_scan",
    )(h, g.reshape(1, d), bre, bim, are, aim, cre, cim, d_skip.reshape(1, d), s0_re, s0_im)


S5_PROMPT_STEPS = 32
S5_SAMPLE_SEQS = 32


def kernel(x_prompt, x_sample, cache_k, cache_v, state_ret, state_s5_re, state_s5_im, page_table, norm1, norm2, norm_f, w_in_even, w_out_even, s5_lam_re, s5_lam_im, s5_log_dt, s5_b_re, s5_b_im, s5_c_re, s5_c_im, s5_d, s5_w_glu, s5_b_glu, moe_w_group, moe_b_group, moe_w_expert, moe_b_expert, moe_w13, moe_w2):
    n_p, seq, d = x_prompt.shape
    n_s, dec_seq, _ = x_sample.shape
    pool, page = cache_k.shape[1], cache_k.shape[2]
    past_len = page_table.shape[1] * page
    np_rows, ns_rows = n_p * seq, n_s * dec_seq
    routers = [_router_weights(moe_w_group[li], moe_b_group[li], moe_w_expert[li], moe_b_expert[li])
               for li in range(2)]

    def moe(li, xn, eid):
        return _moe_experts(xn, eid, moe_w13, moe_w2, li)

    xp, xs = x_prompt.reshape(np_rows, d), x_sample.reshape(ns_rows, d)
    w_in = w_in_even[0].astype(BF16)
    qa_p, ka_p, va_p, qr_p, kr_p, vr_p, gr_p, kt_p, vt_p = _in_proj(xp, norm1[0], w_in, jnp.arange(seq), page)
    qa_s, ka_s, va_s, qr_s, kr_s, vr_s, gr_s = _in_proj(xs, norm1[0], w_in,
                                                        past_len + jnp.arange(ROW_TILE) % dec_seq)
    oa_p = _moba_prompt(qa_p, ka_p, va_p, n_p, seq)
    or_p, ret_p = _ret_prompt(qr_p, kr_p, vr_p, gr_p, n_p, seq)
    oa_s = _moba_sample(qa_s, ka_s, va_s, cache_k[0].transpose(0, 2, 3, 1), cache_v[0].transpose(0, 2, 3, 1),
                        page_table, dec_seq)
    or_s, ret_s = _ret_sample(qr_s, kr_s, vr_s, gr_s, state_ret[0], dec_seq)
    w_out = w_out_even[0].astype(BF16)
    h, xn, eid, we = _proj_router([(oa_p, oa_s), (or_p, or_s)], [w_out[:A_W], w_out[A_W:]], None, (xp, xs),
                                  norm2[0], routers[0], False)
    y = moe(0, xn, eid)
    hp = _moe_combine(h, y, we, norm2[0], False, 0, np_rows)
    hs = _moe_combine(h, y, we, norm2[0], False, np_rows, ns_rows)

    s5w = _s5_weights(s5_lam_re[0], s5_lam_im[0], s5_log_dt[0], s5_b_re[0], s5_b_im[0], s5_c_re[0], s5_c_im[0])
    n_state = s5_lam_re.shape[1] * s5_lam_re.shape[2]
    zeros = jnp.zeros((n_p, n_state), F32)
    zg_p, s5r_p, s5i_p = _s5_scan(hp.reshape(n_p, seq, d), norm1[1], s5w, s5_d[0], zeros, zeros, n_p, S5_PROMPT_STEPS)
    zg_s, s5r_s, s5i_s = _s5_scan(hs.reshape(n_s, dec_seq, d), norm1[1], s5w, s5_d[0],
                                  state_s5_re[0].reshape(n_s, n_state), state_s5_im[0].reshape(n_s, n_state),
                                  S5_SAMPLE_SEQS, dec_seq)
    h, xn, eid, we = _proj_router([(zg_p.reshape(np_rows, d), zg_s.reshape(ns_rows, d))], [s5_w_glu[0].astype(BF16)],
                                  s5_b_glu[0], (hp, hs), norm2[1], routers[1], True)
    y = moe(1, xn, eid)
    y_prompt = _moe_combine(h, y, we, norm_f, True, 0, np_rows).reshape(n_p, seq, d)
    y_sample = _moe_combine(h, y, we, norm_f, True, np_rows, ns_rows).reshape(n_s, dec_seq, d)

    kv_p = lambda a: a.reshape(1, n_p, seq // page, H_A, HD_A, page).transpose(0, 1, 2, 5, 3, 4)
    kv_s = lambda a: a.reshape(1, n_s, dec_seq, H_A, HD_A)
    st = lambda a, n: a.reshape((1, n) + s5_lam_re.shape[1:])
    return (y_prompt, y_sample, kv_p(kt_p), kv_p(vt_p), kv_s(ka_s), kv_s(va_s), ret_p[None], ret_s[None],
            st(s5r_p, n_p), st(s5i_p, n_p), st(s5r_s, n_s), st(s5i_s, n_s))
```
